```python
import jax
import jax.numpy as jnp
from jax import lax
import numpy as np

D_MODEL = 1024
BATCH = 4
SEQ = 4096
DEPTH = 1

MOBA_HEADS = 8
MOBA_HEAD_DIM = 64
MOBA_BLOCK = 256
MOBA_TOPK = 3
MOBA_Q_CHUNK = 32
ROPE_THETA = 10000.0

RET_HEADS = 4
RET_QK_DIM = 128
RET_V_DIM = 256
RET_CHUNK = 128
RET_ANGLE_BASE = 10000.0

MOBA_W = MOBA_HEADS * MOBA_HEAD_DIM
RET_QK_W = RET_HEADS * RET_QK_DIM
RET_V_W = RET_HEADS * RET_V_DIM
IN_SPLITS = (MOBA_W, MOBA_W, MOBA_W, RET_QK_W, RET_QK_W, RET_V_W, RET_V_W, D_MODEL, D_MODEL)
IN_COLS = sum(IN_SPLITS)

D_FF = 2816
CONV_WIDTH = 3

N_MOD = 6
LN_EPS = 1e-5
DEEPNORM_ALPHA = (2.0 * DEPTH) ** 0.25
DEEPNORM_BETA = (8.0 * DEPTH) ** -0.25
NEG_BIG = -1e30

kernel_name = 'hybrid_moba_retention_convffn'


def layer_norm(x, g, b):
    xf = x.astype(jnp.float32)
    mu = jnp.mean(xf, -1, keepdims=True)
    var = jnp.mean(jnp.square(xf - mu), -1, keepdims=True)
    y = (xf - mu) * lax.rsqrt(var + LN_EPS)
    return (y * g.astype(jnp.float32) + b.astype(jnp.float32)).astype(x.dtype)


def head_norm(x):
    xf = x.astype(jnp.float32)
    mu = jnp.mean(xf, -1, keepdims=True)
    var = jnp.mean(jnp.square(xf - mu), -1, keepdims=True)
    return ((xf - mu) * lax.rsqrt(var + LN_EPS)).astype(x.dtype)


def rotary_half(x, pos):
    hd = x.shape[-1]
    inv_freq = ROPE_THETA ** (-jnp.arange(0, hd, 2, dtype=jnp.float32) / hd)
    ang = pos[:, None] * inv_freq[None, :]
    cos, sin = jnp.cos(ang), jnp.sin(ang)
    xf = x.astype(jnp.float32)
    x1, x2 = xf[..., : hd // 2], xf[..., hd // 2:]
    return jnp.concatenate([x1 * cos - x2 * sin, x2 * cos + x1 * sin], -1).astype(x.dtype)


def retnet_rotate(x, pos):
    dk = x.shape[-1]
    freq = 1.0 / (RET_ANGLE_BASE ** jnp.linspace(0.0, 1.0, dk // 2, dtype=jnp.float32))
    ang = pos[:, None] * freq[None, :]
    cos, sin = jnp.cos(ang), jnp.sin(ang)
    xf = x.astype(jnp.float32).reshape(x.shape[:-1] + (dk // 2, 2))
    x0, x1 = xf[..., 0], xf[..., 1]
    out = jnp.stack([x0 * cos - x1 * sin, x1 * cos + x0 * sin], -1)
    return out.reshape(x.shape).astype(x.dtype)


def moba_attention(q, k, v):
    b, h, s, hd = q.shape
    nb = -(-s // MOBA_BLOCK)
    pad = nb * MOBA_BLOCK - s
    k_pad = jnp.pad(k, ((0, 0), (0, 0), (0, pad), (0, 0)))
    v_pad = jnp.pad(v, ((0, 0), (0, 0), (0, pad), (0, 0)))
    k_blk = k_pad.reshape(b, h, nb, MOBA_BLOCK, hd)
    v_blk = v_pad.reshape(b, h, nb, MOBA_BLOCK, hd)
    k_mean = jnp.mean(k_blk.astype(jnp.float32), axis=3)
    topk = min(MOBA_TOPK, nb)
    scale = hd ** -0.5
    bi = jnp.arange(b)[:, None, None, None]
    hi = jnp.arange(h)[None, :, None, None]
    blk_ids = jnp.arange(nb)
    slot_ids = jnp.arange(topk)
    key_off = jnp.arange(MOBA_BLOCK)
    q_off = jnp.arange(MOBA_Q_CHUNK)

    def chunk(i):
        start = i * MOBA_Q_CHUNK
        n = start // MOBA_BLOCK
        qc = lax.dynamic_slice_in_dim(q, start, MOBA_Q_CHUNK, axis=2)
        k_own = lax.dynamic_slice_in_dim(k_pad, n * MOBA_BLOCK, MOBA_BLOCK, axis=2)
        v_own = lax.dynamic_slice_in_dim(v_pad, n * MOBA_BLOCK, MOBA_BLOCK, axis=2)
        causal = (n * MOBA_BLOCK + key_off)[None, :] <= (start + q_off)[:, None]
        l_own = jnp.einsum('bhqd,bhkd->bhqk', qc, k_own).astype(jnp.float32) * scale
        l_own = jnp.where(causal, l_own, NEG_BIG)
        gate = jnp.einsum('bhqd,bhnd->bhqn', qc.astype(jnp.float32), k_mean)
        gate = jnp.where(blk_ids < n, gate, NEG_BIG)
        _, sel = lax.top_k(gate, topk)
        valid = slot_ids < n
        k_sel = k_blk[bi, hi, sel]
        v_sel = v_blk[bi, hi, sel]
        l_past = jnp.einsum('bhqd,bhqtkd->bhqtk', qc, k_sel).astype(jnp.float32) * scale
        l_past = jnp.where(valid[:, None], l_past, NEG_BIG)
        logits = jnp.concatenate(
            [l_own, l_past.reshape(b, h, MOBA_Q_CHUNK, topk * MOBA_BLOCK)], axis=-1)
        p = jax.nn.softmax(logits, axis=-1).astype(v.dtype)
        p_own = p[..., :MOBA_BLOCK]
        p_past = p[..., MOBA_BLOCK:].reshape(b, h, MOBA_Q_CHUNK, topk, MOBA_BLOCK)
        return (jnp.einsum('bhqk,bhkd->bhqd', p_own, v_own)
                + jnp.einsum('bhqtk,bhqtkd->bhqd', p_past, v_sel))

    out = lax.map(chunk, jnp.arange(s // MOBA_Q_CHUNK))
    return jnp.transpose(out, (1, 0, 3, 2, 4)).reshape(b, s, h * hd)


def retention(q, k, v):
    b, h, s, dk = q.shape
    dv = v.shape[-1]
    c = RET_CHUNK
    nc = s // c
    dt = q.dtype
    log_g = jnp.log(1.0 - 2.0 ** (-5.0 - jnp.arange(h, dtype=jnp.float32)))
    idx = jnp.arange(c, dtype=jnp.float32)
    diff = idx[:, None] - idx[None, :]
    d_intra = jnp.where(diff >= 0, jnp.exp(log_g[:, None, None] * jnp.maximum(diff, 0.0)), 0.0)
    k_decay = jnp.exp(log_g[:, None] * (c - 1.0 - idx)[None, :])
    q_decay = jnp.exp(log_g[:, None] * (idx + 1.0)[None, :])
    chunk_decay = jnp.exp(log_g * c)
    qc = q.reshape(b, h, nc, c, dk)
    kc = k.reshape(b, h, nc, c, dk)
    vc = v.reshape(b, h, nc, c, dv)
    scores = jnp.einsum('bhncd,bhnkd->bhnck', qc, kc) * d_intra[None, :, None].astype(dt)
    intra = jnp.einsum('bhnck,bhnkv->bhncv', scores, vc)
    kv = jnp.einsum('bhnkd,bhnkv->nbhdv', kc * k_decay[None, :, None, :, None].astype(dt), vc)
    decay_c = chunk_decay[None, :, None, None].astype(dt)

    def step(state, kv_n):
        return decay_c * state + kv_n, state

    _, s_before = lax.scan(step, jnp.zeros((b, h, dk, dv), dt), kv)
    cross = jnp.einsum('bhncd,nbhdv->bhncv', qc * q_decay[None, :, None, :, None].astype(dt), s_before)
    return (intra + cross).reshape(b, h, s, dv)


def split_heads(t, n_heads):
    b, s, w = t.shape
    return t.reshape(b, s, n_heads, w // n_heads).transpose(0, 2, 1, 3)


def token_mixer(h, w_in, w_proj_moba, w_proj_ret, w_out):
    b, s, _ = h.shape
    proj = h @ w_in
    offsets = [int(o) for o in np.cumsum(IN_SPLITS)[:-1]]
    mq, mk, mv, rq, rk, rv, rg, g_a, g_r = jnp.split(proj, offsets, axis=-1)
    pos = jnp.arange(s, dtype=jnp.float32)
    mq = rotary_half(split_heads(mq, MOBA_HEADS), pos)
    mk = rotary_half(split_heads(mk, MOBA_HEADS), pos)
    y_a = moba_attention(mq, mk, split_heads(mv, MOBA_HEADS))
    rq = retnet_rotate(split_heads(rq, RET_HEADS), pos)
    rk = retnet_rotate(split_heads(rk, RET_HEADS), pos) * (RET_QK_DIM ** -0.5)
    y_r = retention(rq, rk, split_heads(rv, RET_HEADS))
    y_r = head_norm(y_r).transpose(0, 2, 1, 3).reshape(b, s, RET_V_W)
    y_r = jax.nn.silu(rg) * y_r
    merged = jax.nn.sigmoid(g_a) * (y_a @ w_proj_moba) + jax.nn.sigmoid(g_r) * (y_r @ w_proj_ret)
    return merged @ w_out


def conv_ffn(h, w_ff_gate, w_ff_up, ff_conv_w, ff_conv_b, w_ff_down):
    s = h.shape[1]
    g = h @ w_ff_gate
    u = h @ w_ff_up
    gp = jnp.pad(g, ((0, 0), (CONV_WIDTH - 1, 0), (0, 0)))
    g_conv = sum(gp[:, j:j + s, :] * ff_conv_w[j] for j in range(CONV_WIDTH)) + ff_conv_b
    a = jax.nn.gelu(g_conv, approximate=False) * u
    return a @ w_ff_down


def setup_inputs(seed: int = 0) -> dict:
    key = jax.random.key(seed)
    ks = jax.random.split(key, 17)
    f32 = jnp.float32
    beta = DEEPNORM_BETA

    def nrm(k, shape, scale):
        return jax.random.normal(k, shape, f32) * scale

    in_scales = (1.0, 1.0, beta, 1.0, 1.0, beta, 1.0, 1.0, 1.0)
    col_scale = jnp.concatenate([jnp.full((n,), sc, f32) for n, sc in zip(IN_SPLITS, in_scales)])
    return {
        'x': nrm(ks[0], (BATCH, SEQ, D_MODEL), 1.0),
        'c': nrm(ks[1], (BATCH, D_MODEL), 1.0),
        'w_ada': nrm(ks[2], (DEPTH, D_MODEL, N_MOD * D_MODEL), 0.5 * D_MODEL ** -0.5),
        'b_ada': nrm(ks[3], (DEPTH, N_MOD * D_MODEL), 0.02),
        'w_in': nrm(ks[4], (DEPTH, D_MODEL, IN_COLS), D_MODEL ** -0.5) * col_scale,
        'w_proj_moba': nrm(ks[5], (DEPTH, MOBA_W, D_MODEL), MOBA_W ** -0.5 * beta),
        'w_proj_ret': nrm(ks[6], (DEPTH, RET_V_W, D_MODEL), RET_V_W ** -0.5 * beta),
        'w_out': nrm(ks[7], (DEPTH, D_MODEL, D_MODEL), D_MODEL ** -0.5 * beta),
        'ln1_g': 1.0 + nrm(ks[8], (DEPTH, D_MODEL), 0.02),
        'ln1_b': nrm(ks[9], (DEPTH, D_MODEL), 0.02),
        'w_ff_gate': nrm(ks[10], (DEPTH, D_MODEL, D_FF), D_MODEL ** -0.5 * beta),
        'w_ff_up': nrm(ks[11], (DEPTH, D_MODEL, D_FF), D_MODEL ** -0.5 * beta),
        'ff_conv_w': nrm(ks[12], (DEPTH, CONV_WIDTH, D_FF), CONV_WIDTH ** -0.5),
        'ff_conv_b': nrm(ks[13], (DEPTH, D_FF), 0.02),
        'w_ff_down': nrm(ks[14], (DEPTH, D_FF, D_MODEL), D_FF ** -0.5 * beta),
        'ln2_g': 1.0 + nrm(ks[15], (DEPTH, D_MODEL), 0.02),
        'ln2_b': nrm(ks[16], (DEPTH, D_MODEL), 0.02),
    }


def reference(x, c, w_ada, b_ada, w_in, w_proj_moba, w_proj_ret, w_out, ln1_g, ln1_b,
              w_ff_gate, w_ff_up, ff_conv_w, ff_conv_b, w_ff_down, ln2_g, ln2_b):
    for l in range(DEPTH):
        mod = jax.nn.silu(c) @ w_ada[l] + b_ada[l]
        sh1, sc1, g1, sh2, sc2, g2 = jnp.split(mod[:, None, :], N_MOD, axis=-1)
        h = x * (1.0 + sc1) + sh1
        y = token_mixer(h, w_in[l], w_proj_moba[l], w_proj_ret[l], w_out[l])
        x = layer_norm(DEEPNORM_ALPHA * x + g1 * y, ln1_g[l], ln1_b[l])
        h = x * (1.0 + sc2) + sh2
        y = conv_ffn(h, w_ff_gate[l], w_ff_up[l], ff_conv_w[l], ff_conv_b[l], w_ff_down[l])
        x = layer_norm(DEEPNORM_ALPHA * x + g2 * y, ln2_g[l], ln2_b[l])
    return x
```

```python
import functools

import jax
import jax.numpy as jnp
import numpy as np
from jax import lax
from jax.experimental import pallas as pl
from jax.experimental.pallas import tpu as pltpu

F32 = jnp.float32
BF16 = jnp.bfloat16

MOBA_HEADS = 8
MOBA_HEAD_DIM = 64
MOBA_BLOCK = 256
MOBA_TOPK = 3
ROPE_THETA = 10000.0
RET_HEADS = 4
RET_QK_DIM = 128
RET_V_DIM = 256
RET_CHUNK = 128
RET_ANGLE_BASE = 10000.0
N_MOD = 6
LN_EPS = 1e-5
NEG_BIG = -1e30
CONV_WIDTH = 3

V7X_VMEM_LIMIT_BYTES = 56 * 1024 * 1024
LANES = 128
ROW_TILE = 512
FF_CHUNK = 256


def _resident(shape):
    nd = len(shape)
    return pl.BlockSpec(shape, lambda *_: (0,) * nd, pipeline_mode=pl.Buffered(1))


def _layer_norm_rows(z, g, b):
    mu = jnp.mean(z, axis=-1, keepdims=True)
    zc = z - mu
    var = jnp.mean(zc * zc, axis=-1, keepdims=True)
    return zc * lax.rsqrt(var + LN_EPS) * g + b


def _ada_kernel(c_ref, w_ref, b_ref, o_ref):
    c = c_ref[...]
    a = c * jax.nn.sigmoid(c)
    o_ref[...] = jnp.dot(a, w_ref[...], preferred_element_type=F32,
                         precision=lax.Precision.HIGHEST) + b_ref[...]


def _ada(c, w_ada, b_ada):
    b, d = c.shape
    n = w_ada.shape[1]
    rows = 8
    cp = jnp.pad(c, ((0, rows - b), (0, 0)))
    nblk = 1536
    out = pl.pallas_call(
        _ada_kernel,
        out_shape=jax.ShapeDtypeStruct((rows, n), F32),
        grid=(n // nblk,),
        in_specs=[pl.BlockSpec((rows, d), lambda j: (0, 0)),
                  pl.BlockSpec((d, nblk), lambda j: (0, j)),
                  pl.BlockSpec((1, nblk), lambda j: (0, j))],
        out_specs=pl.BlockSpec((rows, nblk), lambda j: (0, j)),
        compiler_params=pltpu.CompilerParams(dimension_semantics=("arbitrary",),
                                             vmem_limit_bytes=V7X_VMEM_LIMIT_BYTES),
        name="ada",
    )(cp, w_ada, b_ada.reshape(1, n))
    return out[:b].reshape(b, 1, n)


def _inproj_kernel(x_ref, mod_ref, wn_ref, wt_ref, ck_ref, sk_ref, cr_ref, sr_ref, cqt_ref, sqt_ref,
                   qt_ref, k_ref, vt_ref, rq_ref, rk_ref, rv_ref, sg_ref, ga_ref, gr_ref, *, d):
    tm = x_ref.shape[0]
    mw = MOBA_HEADS * MOBA_HEAD_DIM
    rw = RET_HEADS * RET_QK_DIM
    vw = RET_HEADS * RET_V_DIM
    sh1 = mod_ref[:, 0:d]
    sc1 = mod_ref[:, d:2 * d]
    h = (x_ref[...] * (1.0 + sc1) + sh1).astype(BF16)

    def nat(col0, width):
        return jnp.dot(h, wn_ref[:, col0:col0 + width], preferred_element_type=F32)

    def trans(row0, width):
        return lax.dot_general(wt_ref[row0:row0 + width, :], h, (((1,), (1,)), ((), ())),
                               preferred_element_type=F32)

    lane = lax.broadcasted_iota(jnp.int32, (tm, LANES), 1)

    qt = trans(0, mw)
    half = MOBA_HEAD_DIM // 2
    cq = cqt_ref[...]
    sq = sqt_ref[...]
    for hd in range(MOBA_HEADS):
        r0 = hd * MOBA_HEAD_DIM
        x1 = qt[r0:r0 + half, :]
        x2 = qt[r0 + half:r0 + 2 * half, :]
        o1 = (x1 * cq - x2 * sq).astype(BF16)
        o2 = (x2 * cq + x1 * sq).astype(BF16)
        for blk in range(tm // MOBA_BLOCK):
            cs = slice(blk * MOBA_BLOCK, (blk + 1) * MOBA_BLOCK)
            qt_ref[blk, r0:r0 + half, :] = o1[:, cs]
            qt_ref[blk, r0 + half:r0 + 2 * half, :] = o2[:, cs]

    kn = nat(0, mw)
    ck = ck_ref[...]
    sk = sk_ref[...]
    low = (lane & half) == 0
    for g in range(mw // LANES):
        xs = kn[:, g * LANES:(g + 1) * LANES]
        partner = jnp.where(low, pltpu.roll(xs, LANES - half, 1), pltpu.roll(xs, half, 1))
        res = (xs * ck + partner * sk).astype(BF16)
        for blk in range(tm // MOBA_BLOCK):
            k_ref[blk, :, g * LANES:(g + 1) * LANES] = res[blk * MOBA_BLOCK:(blk + 1) * MOBA_BLOCK, :]

    vt = trans(mw, mw).astype(BF16)
    for blk in range(tm // MOBA_BLOCK):
        vt_ref[blk] = vt[:, blk * MOBA_BLOCK:(blk + 1) * MOBA_BLOCK]

    cr = cr_ref[...]
    sr = sr_ref[...]
    even = (lane & 1) == 0
    for col0, dst, scale in ((mw, rq_ref, None), (mw + rw, rk_ref, RET_QK_DIM ** -0.5)):
        rn = nat(col0, rw)
        for g in range(rw // LANES):
            xs = rn[:, g * LANES:(g + 1) * LANES]
            partner = jnp.where(even, pltpu.roll(xs, LANES - 1, 1), pltpu.roll(xs, 1, 1))
            res = xs * cr + partner * sr
            if scale is not None:
                res = res * scale
            dst[:, g * LANES:(g + 1) * LANES] = res.astype(BF16)

    col = mw + 2 * rw
    for c0 in range(0, vw, 512):
        rv_ref[:, c0:c0 + 512] = nat(col + c0, 512).astype(BF16)
    col += vw
    for c0 in range(0, vw, 512):
        z = nat(col + c0, 512)
        sg_ref[:, c0:c0 + 512] = (z * jax.nn.sigmoid(z)).astype(BF16)
    col += vw
    for dst in (ga_ref, gr_ref):
        for c0 in range(0, d, 512):
            dst[:, c0:c0 + 512] = jax.nn.sigmoid(nat(col + c0, 512)).astype(BF16)
        col += d


def _rope_tables(s):
    pos = jnp.arange(s, dtype=F32)
    hd = MOBA_HEAD_DIM
    inv_freq = ROPE_THETA ** (-jnp.arange(0, hd, 2, dtype=F32) / hd)
    ang = pos[:, None] * inv_freq[None, :]
    cos, sin = jnp.cos(ang), jnp.sin(ang)
    ck = jnp.tile(cos, (1, 2 * LANES // hd))
    sk = jnp.tile(jnp.concatenate([-sin, sin], axis=1), (1, LANES // hd))
    dk = RET_QK_DIM
    freq = 1.0 / (RET_ANGLE_BASE ** jnp.linspace(0.0, 1.0, dk // 2, dtype=F32))
    angr = pos[:, None] * freq[None, :]
    cosr, sinr = jnp.cos(angr), jnp.sin(angr)
    cr = jnp.repeat(cosr, 2, axis=1)
    sr = jnp.stack([-sinr, sinr], axis=-1).reshape(s, dk)
    return ck, sk, cr, sr, cos.T, sin.T


def _inproj(x, mod, w_in):
    b, s, d = x.shape
    mw = MOBA_HEADS * MOBA_HEAD_DIM
    rw = RET_HEADS * RET_QK_DIM
    vw = RET_HEADS * RET_V_DIM
    tm = ROW_TILE
    nb = s // MOBA_BLOCK
    bpt = tm // MOBA_BLOCK
    w_t = jnp.concatenate([w_in[:, 0:mw], w_in[:, 2 * mw:3 * mw]], axis=1).T.astype(BF16)
    w_n = jnp.concatenate([w_in[:, mw:2 * mw], w_in[:, 3 * mw:]], axis=1).astype(BF16)
    ck, sk, cr, sr, cqt, sqt = _rope_tables(s)
    row = lambda bi, t: (bi, t, 0)
    blk4 = lambda bi, t: (bi, t, 0, 0)
    tab = lambda bi, t: (t, 0)
    out_shapes = (
        jax.ShapeDtypeStruct((b, nb, mw, MOBA_BLOCK), BF16),
        jax.ShapeDtypeStruct((b, nb, MOBA_BLOCK, mw), BF16),
        jax.ShapeDtypeStruct((b, nb, mw, MOBA_BLOCK), BF16),
        jax.ShapeDtypeStruct((b, s, rw), BF16),
        jax.ShapeDtypeStruct((b, s, rw), BF16),
        jax.ShapeDtypeStruct((b, s, vw), BF16),
        jax.ShapeDtypeStruct((b, s, vw), BF16),
        jax.ShapeDtypeStruct((b, s, d), BF16),
        jax.ShapeDtypeStruct((b, s, d), BF16),
    )
    out_specs = (
        pl.BlockSpec((None, bpt, mw, MOBA_BLOCK), blk4),
        pl.BlockSpec((None, bpt, MOBA_BLOCK, mw), blk4),
        pl.BlockSpec((None, bpt, mw, MOBA_BLOCK), blk4),
        pl.BlockSpec((None, tm, rw), row),
        pl.BlockSpec((None, tm, rw), row),
        pl.BlockSpec((None, tm, vw), row),
        pl.BlockSpec((None, tm, vw), row),
        pl.BlockSpec((None, tm, d), row),
        pl.BlockSpec((None, tm, d), row),
    )
    in_specs = [
        pl.BlockSpec((None, tm, d), row),
        pl.BlockSpec((None, 1, N_MOD * d), lambda bi, t: (bi, 0, 0)),
        _resident(w_n.shape),
        _resident(w_t.shape),
        pl.BlockSpec((tm, LANES), tab), pl.BlockSpec((tm, LANES), tab),
        pl.BlockSpec((tm, LANES), tab), pl.BlockSpec((tm, LANES), tab),
        pl.BlockSpec((MOBA_HEAD_DIM // 2, tm), lambda bi, t: (0, t)),
        pl.BlockSpec((MOBA_HEAD_DIM // 2, tm), lambda bi, t: (0, t)),
    ]
    return pl.pallas_call(
        functools.partial(_inproj_kernel, d=d),
        out_shape=out_shapes,
        grid=(b, s // tm),
        in_specs=in_specs,
        out_specs=out_specs,
        compiler_params=pltpu.CompilerParams(dimension_semantics=("arbitrary", "arbitrary"),
                                             vmem_limit_bytes=V7X_VMEM_LIMIT_BYTES),
        name="inproj",
    )(x, mod, w_n, w_t, ck, sk, cr, sr, cqt, sqt)


def _moba_kernel(qt_ref, k_ref, vt_ref, o_ref, kmean_ref, sel_ref):
    n = pl.program_id(2)
    nb = k_ref.shape[0]
    hd = MOBA_HEAD_DIM
    blk = MOBA_BLOCK

    @pl.when(n == 0)
    def _():
        for j in range(nb):
            kmean_ref[j:j + 1, :] = jnp.mean(k_ref[j].astype(F32), axis=0, keepdims=True)

    qt = qt_ref[...]
    feat = lax.broadcasted_iota(jnp.int32, qt.shape, 0)
    km = kmean_ref[...]
    km_lane = lax.broadcasted_iota(jnp.int32, km.shape, 1)
    blk_id = lax.broadcasted_iota(jnp.int32, (nb, blk), 0)
    valid = blk_id < n

    qh = []
    for hh in range(2):
        in_head = (feat >= hh * hd) & (feat < (hh + 1) * hd)
        kmh = jnp.where((km_lane >= hh * hd) & (km_lane < (hh + 1) * hd), km, 0.0)
        km_hi = kmh.astype(BF16)
        km_lo = (kmh - km_hi.astype(F32)).astype(BF16)
        gate = (jnp.dot(km_hi, qt, preferred_element_type=F32)
                + jnp.dot(km_lo, qt, preferred_element_type=F32))
        gate = jnp.where(valid, gate, NEG_BIG)
        rank = jnp.zeros((nb, blk), F32)
        for jp in range(nb):
            gj = gate[jp:jp + 1, :]
            earlier = jnp.where(blk_id > jp, 1.0, 0.0)
            rank = rank + jnp.where(gj > gate, 1.0, jnp.where(gj == gate, earlier, 0.0))
        sel_ref[hh] = jnp.where(valid & (rank < MOBA_TOPK), 1.0, 0.0)
        qh.append(jnp.where(in_head, qt * (hd ** -0.5), 0.0).astype(BF16))

    key_i = lax.broadcasted_iota(jnp.int32, (blk, blk), 0)
    qry_i = lax.broadcasted_iota(jnp.int32, (blk, blk), 1)
    causal = key_i <= qry_i

    k_own = k_ref[n]
    v_own = vt_ref[n]
    carry = []
    for hh in range(2):
        st = jnp.dot(k_own, qh[hh], preferred_element_type=F32)
        st = jnp.where(causal, st, NEG_BIG)
        m = jnp.max(st, axis=0, keepdims=True)
        p = jnp.exp(st - m)
        l = jnp.sum(p, axis=0, keepdims=True)
        acc = jnp.dot(v_own[hh * hd:(hh + 1) * hd, :], p.astype(BF16), preferred_element_type=F32)
        carry += [m, l, acc]

    def body(j, carry):
        kj = k_ref[j]
        vj = vt_ref[j]
        out = []
        for hh in range(2):
            m, l, acc = carry[3 * hh:3 * hh + 3]
            st = jnp.dot(kj, qh[hh], preferred_element_type=F32)
            chosen = sel_ref[hh, pl.ds(j, 1), :] > 0.5
            st = jnp.where(chosen, st, NEG_BIG)
            m_new = jnp.maximum(m, jnp.max(st, axis=0, keepdims=True))
            alpha = jnp.exp(m - m_new)
            p = jnp.exp(st - m_new)
            l = alpha * l + jnp.sum(p, axis=0, keepdims=True)
            acc = alpha * acc + jnp.dot(vj[hh * hd:(hh + 1) * hd, :], p.astype(BF16),
                                        preferred_element_type=F32)
            out += [m_new, l, acc]
        return tuple(out)

    carry = lax.fori_loop(0, n, body, tuple(carry))
    outs = [carry[3 * hh + 2] * (1.0 / carry[3 * hh + 1]) for hh in range(2)]
    o_ref[...] = jnp.concatenate(outs, axis=0).T.astype(BF16)


def _moba(qt, k, vt):
    b, nb, mw, blk = qt.shape
    pair = 2 * MOBA_HEAD_DIM
    return pl.pallas_call(
        _moba_kernel,
        out_shape=jax.ShapeDtypeStruct((b, nb * blk, mw), BF16),
        grid=(b, mw // pair, nb),
        in_specs=[pl.BlockSpec((None, None, pair, blk), lambda bi, hp, n: (bi, n, hp, 0)),
                  pl.BlockSpec((None, nb, blk, pair), lambda bi, hp, n: (bi, 0, 0, hp)),
                  pl.BlockSpec((None, nb, pair, blk), lambda bi, hp, n: (bi, 0, hp, 0))],
        out_specs=pl.BlockSpec((None, blk, pair), lambda bi, hp, n: (bi, n, hp)),
        scratch_shapes=[pltpu.VMEM((nb, pair), F32), pltpu.VMEM((2, nb, blk), F32)],
        compiler_params=pltpu.CompilerParams(dimension_semantics=("arbitrary", "arbitrary", "arbitrary"),
                                             vmem_limit_bytes=V7X_VMEM_LIMIT_BYTES),
        name="moba",
    )(qt, k, vt)


def _ret_kernel(cdec_ref, q_ref, k_ref, v_ref, sg_ref, dint_ref, kdec_ref, qdec_ref, o_ref, state_ref):
    hi = pl.program_id(1)

    @pl.when(pl.program_id(2) == 0)
    def _():
        state_ref[...] = jnp.zeros_like(state_ref)

    c = RET_CHUNK
    dint = dint_ref[...]
    kdec = kdec_ref[...]
    qdec = qdec_ref[...]
    cd = cdec_ref[hi]
    s = state_ref[...]
    for ci in range(q_ref.shape[0] // c):
        rows = slice(ci * c, (ci + 1) * c)
        q = q_ref[rows, :]
        k = k_ref[rows, :]
        v = v_ref[rows, :]
        scores = lax.dot_general(q, k, (((1,), (1,)), ((), ())), preferred_element_type=F32) * dint
        inner = jnp.dot(scores.astype(BF16), v, preferred_element_type=F32)
        qd = (q.astype(F32) * qdec).astype(BF16)
        inner = inner + jnp.dot(qd, s.astype(BF16), preferred_element_type=F32)
        kd = (k.astype(F32) * kdec).astype(BF16)
        kv = lax.dot_general(kd, v, (((0,), (0,)), ((), ())), preferred_element_type=F32)
        s = cd * s + kv
        mu = jnp.mean(inner, axis=-1, keepdims=True)
        xc = inner - mu
        var = jnp.mean(xc * xc, axis=-1, keepdims=True)
        y = xc * lax.rsqrt(var + LN_EPS) * sg_ref[rows, :].astype(F32)
        o_ref[rows, :] = y.astype(BF16)
    state_ref[...] = s


def _retention(rq, rk, rv, sg):
    b, s, _ = rq.shape
    h, dk, dv, c = RET_HEADS, RET_QK_DIM, RET_V_DIM, RET_CHUNK
    tm = ROW_TILE
    log_g = jnp.log(1.0 - 2.0 ** (-5.0 - jnp.arange(h, dtype=F32)))
    idx = jnp.arange(c, dtype=F32)
    diff = idx[:, None] - idx[None, :]
    d_intra = jnp.where(diff >= 0, jnp.exp(log_g[:, None, None] * jnp.maximum(diff, 0.0)), 0.0)
    k_decay = jnp.exp(log_g[:, None] * (c - 1.0 - idx)[None, :])
    q_decay = jnp.exp(log_g[:, None] * (idx + 1.0)[None, :])
    chunk_decay = jnp.exp(log_g * c)
    kdec = jnp.broadcast_to(k_decay[:, :, None], (h, c, dk))
    qdec = jnp.broadcast_to(q_decay[:, :, None], (h, c, dk))
    head_rows = lambda bi, hi, t: (bi, t, hi)
    head_tab = lambda bi, hi, t: (hi, 0, 0)
    return pl.pallas_call(
        _ret_kernel,
        out_shape=jax.ShapeDtypeStruct((b, s, h * dv), BF16),
        grid=(b, h, s // tm),
        in_specs=[pl.BlockSpec(memory_space=pltpu.SMEM),
                  pl.BlockSpec((None, tm, dk), head_rows),
                  pl.BlockSpec((None, tm, dk), head_rows),
                  pl.BlockSpec((None, tm, dv), head_rows),
                  pl.BlockSpec((None, tm, dv), head_rows),
                  pl.BlockSpec((None, c, c), head_tab),
                  pl.BlockSpec((None, c, dk), head_tab),
                  pl.BlockSpec((None, c, dk), head_tab)],
        out_specs=pl.BlockSpec((None, tm, dv), head_rows),
        scratch_shapes=[pltpu.VMEM((dk, dv), F32)],
        compiler_params=pltpu.CompilerParams(dimension_semantics=("arbitrary", "arbitrary", "arbitrary"),
                                             vmem_limit_bytes=V7X_VMEM_LIMIT_BYTES),
        name="ret",
    )(chunk_decay, rq, rk, rv, sg, d_intra, kdec, qdec)


def _merge_kernel(ya_ref, yr_ref, ga_ref, gr_ref, x_ref, mod_ref, pa_ref, pr_ref, wo_ref, lng_ref, lnb_ref,
                  x1_ref, h2_ref, *, d, alpha):
    a = jnp.dot(ya_ref[...], pa_ref[...], preferred_element_type=F32)
    r = jnp.dot(yr_ref[...], pr_ref[...], preferred_element_type=F32)
    merged = ga_ref[...].astype(F32) * a + gr_ref[...].astype(F32) * r
    y = jnp.dot(merged.astype(BF16), wo_ref[...], preferred_element_type=F32)
    g1 = mod_ref[:, 2 * d:3 * d]
    x1 = _layer_norm_rows(alpha * x_ref[...] + g1 * y, lng_ref[...], lnb_ref[...])
    x1_ref[...] = x1
    sh2 = mod_ref[:, 3 * d:4 * d]
    sc2 = mod_ref[:, 4 * d:5 * d]
    h2_ref[...] = (x1 * (1.0 + sc2) + sh2).astype(BF16)


def _merge(ya, yr, ga, gr, x, mod, w_pa, w_pr, w_o, ln_g, ln_b, alpha):
    b, s, d = x.shape
    tm = ROW_TILE
    row = lambda bi, t: (bi, t, 0)
    pa, pr, wo = w_pa.astype(BF16), w_pr.astype(BF16), w_o.astype(BF16)
    return pl.pallas_call(
        functools.partial(_merge_kernel, d=d, alpha=alpha),
        out_shape=(jax.ShapeDtypeStruct((b, s, d), F32), jax.ShapeDtypeStruct((b, s, d), BF16)),
        grid=(b, s // tm),
        in_specs=[pl.BlockSpec((None, tm, ya.shape[-1]), row),
                  pl.BlockSpec((None, tm, yr.shape[-1]), row),
                  pl.BlockSpec((None, tm, d), row),
                  pl.BlockSpec((None, tm, d), row),
                  pl.BlockSpec((None, tm, d), row),
                  pl.BlockSpec((None, 1, N_MOD * d), lambda bi, t: (bi, 0, 0)),
                  _resident(pa.shape), _resident(pr.shape), _resident(wo.shape),
                  _resident((1, d)), _resident((1, d))],
        out_specs=(pl.BlockSpec((None, tm, d), row), pl.BlockSpec((None, tm, d), row)),
        compiler_params=pltpu.CompilerParams(dimension_semantics=("arbitrary", "arbitrary"),
                                             vmem_limit_bytes=V7X_VMEM_LIMIT_BYTES),
        name="merge",
    )(ya, yr, ga, gr, x, mod, pa, pr, wo, ln_g.reshape(1, d), ln_b.reshape(1, d))


def _shift_rows(g, prev, k):
    rolled = pltpu.roll(g, k, 0)
    sub = lax.broadcasted_iota(jnp.int32, prev.shape, 0)
    top = jnp.where(sub < k, pltpu.roll(prev, k, 0), rolled[:prev.shape[0], :])
    return jnp.concatenate([top, rolled[prev.shape[0]:, :]], axis=0)


def _ffn_kernel(h_ref, x1_ref, mod_ref, wg_ref, wu_ref, wd_ref, cw_ref, cb_ref, lng_ref, lnb_ref,
                o_ref, tail_ref, acc_ref, *, d, alpha):
    tm = h_ref.shape[0]

    @pl.when(pl.program_id(1) == 0)
    def _():
        tail_ref[...] = jnp.zeros_like(tail_ref)

    h = h_ref[...]
    fc = FF_CHUNK
    for c in range(wg_ref.shape[1] // fc):
        cs = slice(c * fc, (c + 1) * fc)
        g = jnp.dot(h, wg_ref[:, cs], preferred_element_type=F32)
        u = jnp.dot(h, wu_ref[:, cs], preferred_element_type=F32)
        prev = tail_ref[c]
        tail_ref[c] = g[tm - 8:, :]
        gc = (_shift_rows(g, prev, 2) * cw_ref[0:1, cs] + _shift_rows(g, prev, 1) * cw_ref[1:2, cs]
              + g * cw_ref[2:3, cs] + cb_ref[:, cs])
        act = 0.5 * gc * (1.0 + lax.erf(gc * (2.0 ** -0.5))) * u
        part = jnp.dot(act.astype(BF16), wd_ref[cs, :], preferred_element_type=F32)
        if c == 0:
            acc_ref[...] = part
        else:
            acc_ref[...] += part
    g2 = mod_ref[:, 5 * d:6 * d]
    o_ref[...] = _layer_norm_rows(alpha * x1_ref[...] + g2 * acc_ref[...], lng_ref[...], lnb_ref[...])


def _ffn(h2, x1, mod, w_gate, w_up, conv_w, conv_b, w_down, ln_g, ln_b, alpha):
    b, s, d = x1.shape
    dff = w_gate.shape[1]
    tm = ROW_TILE
    row = lambda bi, t: (bi, t, 0)
    wg, wu, wd = w_gate.astype(BF16), w_up.astype(BF16), w_down.astype(BF16)
    return pl.pallas_call(
        functools.partial(_ffn_kernel, d=d, alpha=alpha),
        out_shape=jax.ShapeDtypeStruct((b, s, d), F32),
        grid=(b, s // tm),
        in_specs=[pl.BlockSpec((None, tm, d), row),
                  pl.BlockSpec((None, tm, d), row),
                  pl.BlockSpec((None, 1, N_MOD * d), lambda bi, t: (bi, 0, 0)),
                  _resident(wg.shape), _resident(wu.shape), _resident(wd.shape),
                  _resident(conv_w.shape), _resident((1, dff)),
                  _resident((1, d)), _resident((1, d))],
        out_specs=pl.BlockSpec((None, tm, d), row),
        scratch_shapes=[pltpu.VMEM((dff // FF_CHUNK, 8, FF_CHUNK), F32), pltpu.VMEM((tm, d), F32)],
        compiler_params=pltpu.CompilerParams(dimension_semantics=("arbitrary", "arbitrary"),
                                             vmem_limit_bytes=V7X_VMEM_LIMIT_BYTES),
        name="ffn",
    )(h2, x1, mod, wg, wu, wd, conv_w, conv_b.reshape(1, dff), ln_g.reshape(1, d), ln_b.reshape(1, d))


def kernel(x, c, w_ada, b_ada, w_in, w_proj_moba, w_proj_ret, w_out, ln1_g, ln1_b, w_ff_gate, w_ff_up,
           ff_conv_w, ff_conv_b, w_ff_down, ln2_g, ln2_b):
    depth = w_ada.shape[0]
    alpha = (2.0 * depth) ** 0.25
    for l in range(depth):
        mod = _ada(c, w_ada[l], b_ada[l])
        qt, k, vt, rq, rk, rv, sg, ga, gr = _inproj(x, mod, w_in[l])
        ya = _moba(qt, k, vt)
        yr = _retention(rq, rk, rv, sg)
        x1, h2 = _merge(ya, yr, ga, gr, x, mod, w_proj_moba[l], w_proj_ret[l], w_out[l],
                        ln1_g[l], ln1_b[l], alpha)
        x = _ffn(h2, x1, mod, w_ff_gate[l], w_ff_up[l], ff_conv_w[l], ff_conv_b[l], w_ff_down[l],
                 ln2_g[l], ln2_b[l], alpha)
    return x
```

```python
import functools

import jax
import jax.numpy as jnp
import numpy as np
from jax import lax
from jax.experimental import pallas as pl
from jax.experimental.pallas import tpu as pltpu

F32 = jnp.float32
BF16 = jnp.bfloat16

MOBA_HEADS = 8
MOBA_HEAD_DIM = 64
MOBA_BLOCK = 256
MOBA_TOPK = 3
ROPE_THETA = 10000.0
RET_HEADS = 4
RET_QK_DIM = 128
RET_V_DIM = 256
RET_CHUNK = 128
RET_ANGLE_BASE = 10000.0
N_MOD = 6
LN_EPS = 1e-5
NEG_BIG = -1e30
CONV_WIDTH = 3

V7X_VMEM_LIMIT_BYTES = 56 * 1024 * 1024
LANES = 128
ROW_TILE = 512
FF_CHUNK = 256
MOBA_HEADS_PER_STEP = 4
MOBA_GROUP_LOG2 = 1


def _resident(shape):
    nd = len(shape)
    return pl.BlockSpec(shape, lambda *_: (0,) * nd, pipeline_mode=pl.Buffered(1))


def _layer_norm_rows(z, g, b):
    mu = jnp.mean(z, axis=-1, keepdims=True)
    zc = z - mu
    var = jnp.mean(zc * zc, axis=-1, keepdims=True)
    return zc * lax.rsqrt(var + LN_EPS) * g + b


def _ada_kernel(c_ref, w_ref, b_ref, o_ref):
    c = c_ref[...]
    a = c * jax.nn.sigmoid(c)
    o_ref[...] = jnp.dot(a, w_ref[...], preferred_element_type=F32,
                         precision=lax.Precision.HIGHEST) + b_ref[...]


def _ada(c, w_ada, b_ada):
    b, d = c.shape
    n = w_ada.shape[1]
    rows = 8
    cp = jnp.pad(c, ((0, rows - b), (0, 0)))
    nblk = 1536
    out = pl.pallas_call(
        _ada_kernel,
        out_shape=jax.ShapeDtypeStruct((rows, n), F32),
        grid=(n // nblk,),
        in_specs=[pl.BlockSpec((rows, d), lambda j: (0, 0)),
                  pl.BlockSpec((d, nblk), lambda j: (0, j)),
                  pl.BlockSpec((1, nblk), lambda j: (0, j))],
        out_specs=pl.BlockSpec((rows, nblk), lambda j: (0, j)),
        compiler_params=pltpu.CompilerParams(dimension_semantics=("arbitrary",),
                                             vmem_limit_bytes=V7X_VMEM_LIMIT_BYTES),
        name="ada",
    )(cp, w_ada, b_ada.reshape(1, n))
    return out[:b].reshape(b, 1, n)


def _inproj_kernel(x_ref, mod_ref, wn_ref, wt_ref, ck_ref, sk_ref, cr_ref, sr_ref, cqt_ref, sqt_ref,
                   qt_ref, k_ref, vt_ref, rq_ref, rk_ref, rv_ref, sg_ref, ga_ref, gr_ref, *, d):
    tm = x_ref.shape[0]
    mw = MOBA_HEADS * MOBA_HEAD_DIM
    rw = RET_HEADS * RET_QK_DIM
    vw = RET_HEADS * RET_V_DIM
    sh1 = mod_ref[:, 0:d]
    sc1 = mod_ref[:, d:2 * d]
    h = (x_ref[...] * (1.0 + sc1) + sh1).astype(BF16)

    def nat(col0, width):
        return jnp.dot(h, wn_ref[:, col0:col0 + width], preferred_element_type=F32)

    def trans(row0, width):
        return lax.dot_general(wt_ref[row0:row0 + width, :], h, (((1,), (1,)), ((), ())),
                               preferred_element_type=F32)

    lane = lax.broadcasted_iota(jnp.int32, (tm, LANES), 1)

    qt = trans(0, mw)
    half = MOBA_HEAD_DIM // 2
    cq = cqt_ref[...]
    sq = sqt_ref[...]
    for hd in range(MOBA_HEADS):
        r0 = hd * MOBA_HEAD_DIM
        x1 = qt[r0:r0 + half, :]
        x2 = qt[r0 + half:r0 + 2 * half, :]
        o1 = (x1 * cq - x2 * sq).astype(BF16)
        o2 = (x2 * cq + x1 * sq).astype(BF16)
        for blk in range(tm // MOBA_BLOCK):
            cs = slice(blk * MOBA_BLOCK, (blk + 1) * MOBA_BLOCK)
            qt_ref[blk, r0:r0 + half, :] = o1[:, cs]
            qt_ref[blk, r0 + half:r0 + 2 * half, :] = o2[:, cs]

    kn = nat(0, mw)
    ck = ck_ref[...]
    sk = sk_ref[...]
    low = (lane & half) == 0
    for g in range(mw // LANES):
        xs = kn[:, g * LANES:(g + 1) * LANES]
        partner = jnp.where(low, pltpu.roll(xs, LANES - half, 1), pltpu.roll(xs, half, 1))
        res = (xs * ck + partner * sk).astype(BF16)
        for blk in range(tm // MOBA_BLOCK):
            k_ref[blk, :, g * LANES:(g + 1) * LANES] = res[blk * MOBA_BLOCK:(blk + 1) * MOBA_BLOCK, :]

    vt = trans(mw, mw).astype(BF16)
    for blk in range(tm // MOBA_BLOCK):
        vt_ref[blk] = vt[:, blk * MOBA_BLOCK:(blk + 1) * MOBA_BLOCK]

    cr = cr_ref[...]
    sr = sr_ref[...]
    even = (lane & 1) == 0
    for col0, dst, scale in ((mw, rq_ref, None), (mw + rw, rk_ref, RET_QK_DIM ** -0.5)):
        rn = nat(col0, rw)
        for g in range(rw // LANES):
            xs = rn[:, g * LANES:(g + 1) * LANES]
            partner = jnp.where(even, pltpu.roll(xs, LANES - 1, 1), pltpu.roll(xs, 1, 1))
            res = xs * cr + partner * sr
            if scale is not None:
                res = res * scale
            dst[:, g * LANES:(g + 1) * LANES] = res.astype(BF16)

    col = mw + 2 * rw
    for c0 in range(0, vw, 512):
        rv_ref[:, c0:c0 + 512] = nat(col + c0, 512).astype(BF16)
    col += vw
    for c0 in range(0, vw, 512):
        z = nat(col + c0, 512)
        sg_ref[:, c0:c0 + 512] = (z * jax.nn.sigmoid(z)).astype(BF16)
    col += vw
    for dst in (ga_ref, gr_ref):
        for c0 in range(0, d, 512):
            dst[:, c0:c0 + 512] = jax.nn.sigmoid(nat(col + c0, 512)).astype(BF16)
        col += d


def _rope_tables(s):
    pos = jnp.arange(s, dtype=F32)
    hd = MOBA_HEAD_DIM
    inv_freq = ROPE_THETA ** (-jnp.arange(0, hd, 2, dtype=F32) / hd)
    ang = pos[:, None] * inv_freq[None, :]
    cos, sin = jnp.cos(ang), jnp.sin(ang)
    ck = jnp.tile(cos, (1, 2 * LANES // hd))
    sk = jnp.tile(jnp.concatenate([-sin, sin], axis=1), (1, LANES // hd))
    dk = RET_QK_DIM
    freq = 1.0 / (RET_ANGLE_BASE ** jnp.linspace(0.0, 1.0, dk // 2, dtype=F32))
    angr = pos[:, None] * freq[None, :]
    cosr, sinr = jnp.cos(angr), jnp.sin(angr)
    cr = jnp.repeat(cosr, 2, axis=1)
    sr = jnp.stack([-sinr, sinr], axis=-1).reshape(s, dk)
    return ck, sk, cr, sr, cos.T, sin.T


def _inproj(x, mod, w_in):
    b, s, d = x.shape
    mw = MOBA_HEADS * MOBA_HEAD_DIM
    rw = RET_HEADS * RET_QK_DIM
    vw = RET_HEADS * RET_V_DIM
    tm = ROW_TILE
    nb = s // MOBA_BLOCK
    bpt = tm // MOBA_BLOCK
    w_t = jnp.concatenate([w_in[:, 0:mw], w_in[:, 2 * mw:3 * mw]], axis=1).T.astype(BF16)
    w_n = jnp.concatenate([w_in[:, mw:2 * mw], w_in[:, 3 * mw:]], axis=1).astype(BF16)
    ck, sk, cr, sr, cqt, sqt = _rope_tables(s)
    row = lambda bi, t: (bi, t, 0)
    blk4 = lambda bi, t: (bi, t, 0, 0)
    tab = lambda bi, t: (t, 0)
    out_shapes = (
        jax.ShapeDtypeStruct((b, nb, mw, MOBA_BLOCK), BF16),
        jax.ShapeDtypeStruct((b, nb, MOBA_BLOCK, mw), BF16),
        jax.ShapeDtypeStruct((b, nb, mw, MOBA_BLOCK), BF16),
        jax.ShapeDtypeStruct((b, s, rw), BF16),
        jax.ShapeDtypeStruct((b, s, rw), BF16),
        jax.ShapeDtypeStruct((b, s, vw), BF16),
        jax.ShapeDtypeStruct((b, s, vw), BF16),
        jax.ShapeDtypeStruct((b, s, d), BF16),
        jax.ShapeDtypeStruct((b, s, d), BF16),
    )
    out_specs = (
        pl.BlockSpec((None, bpt, mw, MOBA_BLOCK), blk4),
        pl.BlockSpec((None, bpt, MOBA_BLOCK, mw), blk4),
        pl.BlockSpec((None, bpt, mw, MOBA_BLOCK), blk4),
        pl.BlockSpec((None, tm, rw), row),
        pl.BlockSpec((None, tm, rw), row),
        pl.BlockSpec((None, tm, vw), row),
        pl.BlockSpec((None, tm, vw), row),
        pl.BlockSpec((None, tm, d), row),
        pl.BlockSpec((None, tm, d), row),
    )
    in_specs = [
        pl.BlockSpec((None, tm, d), row),
        pl.BlockSpec((None, 1, N_MOD * d), lambda bi, t: (bi, 0, 0)),
        _resident(w_n.shape),
        _resident(w_t.shape),
        pl.BlockSpec((tm, LANES), tab), pl.BlockSpec((tm, LANES), tab),
        pl.BlockSpec((tm, LANES), tab), pl.BlockSpec((tm, LANES), tab),
        pl.BlockSpec((MOBA_HEAD_DIM // 2, tm), lambda bi, t: (0, t)),
        pl.BlockSpec((MOBA_HEAD_DIM // 2, tm), lambda bi, t: (0, t)),
    ]
    return pl.pallas_call(
        functools.partial(_inproj_kernel, d=d),
        out_shape=out_shapes,
        grid=(b, s // tm),
        in_specs=in_specs,
        out_specs=out_specs,
        compiler_params=pltpu.CompilerParams(dimension_semantics=("arbitrary", "arbitrary"),
                                             vmem_limit_bytes=V7X_VMEM_LIMIT_BYTES),
        name="inproj",
    )(x, mod, w_n, w_t, ck, sk, cr, sr, cqt, sqt)


def _moba_kernel(qt_ref, k_ref, vt_ref, o_ref, kmean_ref, sel_ref, qh_ref, sown_ref, s_ref):
    n = pl.program_id(2)
    nb = k_ref.shape[0]
    hd = MOBA_HEAD_DIM
    blk = MOBA_BLOCK
    hs = MOBA_HEADS_PER_STEP
    grp = 1 << MOBA_GROUP_LOG2
    sub = 8

    @pl.when(n == 0)
    def _():
        for j in range(nb):
            kmean_ref[j:j + 1, :] = jnp.mean(k_ref[j].astype(F32), axis=0, keepdims=True)

    qt = qt_ref[...]
    feat = lax.broadcasted_iota(jnp.int32, qt.shape, 0)
    km = kmean_ref[...]
    km_lane = lax.broadcasted_iota(jnp.int32, km.shape, 1)
    blk_id = lax.broadcasted_iota(jnp.int32, (nb, blk), 0)
    valid = blk_id < n

    for hh in range(hs):
        in_head = (feat >= hh * hd) & (feat < (hh + 1) * hd)
        kmh = jnp.where((km_lane >= hh * hd) & (km_lane < (hh + 1) * hd), km, 0.0)
        km_hi = kmh.astype(BF16)
        km_lo = (kmh - km_hi.astype(F32)).astype(BF16)
        gate = (jnp.dot(km_hi, qt, preferred_element_type=F32)
                + jnp.dot(km_lo, qt, preferred_element_type=F32))
        gate = jnp.where(valid, gate, NEG_BIG)
        rank = jnp.zeros((nb, blk), F32)
        for jp in range(nb):
            gj = gate[jp:jp + 1, :]
            earlier = jnp.where(blk_id > jp, 1.0, 0.0)
            rank = rank + jnp.where(gj > gate, 1.0, jnp.where(gj == gate, earlier, 0.0))
        sel_ref[hh] = jnp.where(valid & (rank < MOBA_TOPK), 1.0, 0.0)
        qh_ref[hh] = jnp.where(in_head, qt * (hd ** -0.5), 0.0).astype(BF16)

    def col_max(st):
        return jnp.max(st.reshape(blk // sub, sub, blk), axis=0)

    def col_sum(p):
        return jnp.sum(p.reshape(blk // sub, sub, blk), axis=0)

    key_i = lax.broadcasted_iota(jnp.int32, (blk, blk), 0)
    qry_i = lax.broadcasted_iota(jnp.int32, (blk, blk), 1)
    causal = key_i <= qry_i

    k_own = k_ref[n]
    mparts = []
    for hh in range(hs):
        st = jnp.dot(k_own, qh_ref[hh], preferred_element_type=F32)
        st = jnp.where(causal, st, NEG_BIG)
        sown_ref[hh] = st
        mparts.append(col_max(st))

    n_groups = lax.shift_right_logical(n + (grp - 1), MOBA_GROUP_LOG2)

    def pass1(i, mparts):
        out = list(mparts)
        for g in range(grp):
            j = grp * i + g
            kj = k_ref[j]
            for hh in range(hs):
                st = jnp.dot(kj, qh_ref[hh], preferred_element_type=F32)
                chosen = sel_ref[hh, pl.ds(j, 1), :] > 0.5
                st = jnp.where(chosen, st, NEG_BIG)
                s_ref[hh, j] = st
                out[hh] = jnp.maximum(out[hh], col_max(st))
        return tuple(out)

    mparts = lax.fori_loop(0, n_groups, pass1, tuple(mparts))
    m = [jnp.max(mp, axis=0, keepdims=True) for mp in mparts]

    v_own = vt_ref[n]
    carry = []
    for hh in range(hs):
        p = jnp.exp(sown_ref[hh] - m[hh])
        carry += [col_sum(p),
                  jnp.dot(v_own[hh * hd:(hh + 1) * hd, :], p.astype(BF16), preferred_element_type=F32)]

    def pass2(i, carry):
        out = list(carry)
        for g in range(grp):
            j = grp * i + g
            vj = vt_ref[j]
            for hh in range(hs):
                p = jnp.exp(s_ref[hh, j] - m[hh])
                out[2 * hh] = out[2 * hh] + col_sum(p)
                out[2 * hh + 1] = out[2 * hh + 1] + jnp.dot(vj[hh * hd:(hh + 1) * hd, :], p.astype(BF16),
                                                            preferred_element_type=F32)
        return tuple(out)

    carry = lax.fori_loop(0, n_groups, pass2, tuple(carry))
    outs = [carry[2 * hh + 1] * (1.0 / jnp.sum(carry[2 * hh], axis=0, keepdims=True)) for hh in range(hs)]
    o_ref[...] = jnp.concatenate(outs, axis=0).T.astype(BF16)


def _moba(qt, k, vt):
    b, nb, mw, blk = qt.shape
    hs = MOBA_HEADS_PER_STEP
    pair = hs * MOBA_HEAD_DIM
    return pl.pallas_call(
        _moba_kernel,
        out_shape=jax.ShapeDtypeStruct((b, nb * blk, mw), BF16),
        grid=(b, mw // pair, nb),
        in_specs=[pl.BlockSpec((None, None, pair, blk), lambda bi, hp, n: (bi, n, hp, 0)),
                  pl.BlockSpec((None, nb, blk, pair), lambda bi, hp, n: (bi, 0, 0, hp)),
                  pl.BlockSpec((None, nb, pair, blk), lambda bi, hp, n: (bi, 0, hp, 0))],
        out_specs=pl.BlockSpec((None, blk, pair), lambda bi, hp, n: (bi, n, hp)),
        scratch_shapes=[pltpu.VMEM((nb, pair), F32),
                        pltpu.VMEM((hs, nb, blk), F32),
                        pltpu.VMEM((hs, pair, blk), BF16),
                        pltpu.VMEM((hs, blk, blk), F32),
                        pltpu.VMEM((hs, nb, blk, blk), F32)],
        compiler_params=pltpu.CompilerParams(dimension_semantics=("arbitrary", "arbitrary", "arbitrary"),
                                             vmem_limit_bytes=V7X_VMEM_LIMIT_BYTES),
        name="moba",
    )(qt, k, vt)


def _ret_kernel(cdec_ref, q_ref, k_ref, v_ref, sg_ref, dint_ref, kdec_ref, qdec_ref, o_ref, state_ref):
    hi = pl.program_id(1)

    @pl.when(pl.program_id(2) == 0)
    def _():
        state_ref[...] = jnp.zeros_like(state_ref)

    c = RET_CHUNK
    dint = dint_ref[...]
    kdec = kdec_ref[...]
    qdec = qdec_ref[...]
    cd = cdec_ref[hi]
    s = state_ref[...]
    for ci in range(q_ref.shape[0] // c):
        rows = slice(ci * c, (ci + 1) * c)
        q = q_ref[rows, :]
        k = k_ref[rows, :]
        v = v_ref[rows, :]
        scores = lax.dot_general(q, k, (((1,), (1,)), ((), ())), preferred_element_type=F32) * dint
        inner = jnp.dot(scores.astype(BF16), v, preferred_element_type=F32)
        qd = (q.astype(F32) * qdec).astype(BF16)
        inner = inner + jnp.dot(qd, s.astype(BF16), preferred_element_type=F32)
        kd = (k.astype(F32) * kdec).astype(BF16)
        kv = lax.dot_general(kd, v, (((0,), (0,)), ((), ())), preferred_element_type=F32)
        s = cd * s + kv
        mu = jnp.mean(inner, axis=-1, keepdims=True)
        xc = inner - mu
        var = jnp.mean(xc * xc, axis=-1, keepdims=True)
        y = xc * lax.rsqrt(var + LN_EPS) * sg_ref[rows, :].astype(F32)
        o_ref[rows, :] = y.astype(BF16)
    state_ref[...] = s


def _retention(rq, rk, rv, sg):
    b, s, _ = rq.shape
    h, dk, dv, c = RET_HEADS, RET_QK_DIM, RET_V_DIM, RET_CHUNK
    tm = ROW_TILE
    log_g = jnp.log(1.0 - 2.0 ** (-5.0 - jnp.arange(h, dtype=F32)))
    idx = jnp.arange(c, dtype=F32)
    diff = idx[:, None] - idx[None, :]
    d_intra = jnp.where(diff >= 0, jnp.exp(log_g[:, None, None] * jnp.maximum(diff, 0.0)), 0.0)
    k_decay = jnp.exp(log_g[:, None] * (c - 1.0 - idx)[None, :])
    q_decay = jnp.exp(log_g[:, None] * (idx + 1.0)[None, :])
    chunk_decay = jnp.exp(log_g * c)
    kdec = jnp.broadcast_to(k_decay[:, :, None], (h, c, dk))
    qdec = jnp.broadcast_to(q_decay[:, :, None], (h, c, dk))
    head_rows = lambda bi, hi, t: (bi, t, hi)
    head_tab = lambda bi, hi, t: (hi, 0, 0)
    return pl.pallas_call(
        _ret_kernel,
        out_shape=jax.ShapeDtypeStruct((b, s, h * dv), BF16),
        grid=(b, h, s // tm),
        in_specs=[pl.BlockSpec(memory_space=pltpu.SMEM),
                  pl.BlockSpec((None, tm, dk), head_rows),
                  pl.BlockSpec((None, tm, dk), head_rows),
                  pl.BlockSpec((None, tm, dv), head_rows),
                  pl.BlockSpec((None, tm, dv), head_rows),
                  pl.BlockSpec((None, c, c), head_tab),
                  pl.BlockSpec((None, c, dk), head_tab),
                  pl.BlockSpec((None, c, dk), head_tab)],
        out_specs=pl.BlockSpec((None, tm, dv), head_rows),
        scratch_shapes=[pltpu.VMEM((dk, dv), F32)],
        compiler_params=pltpu.CompilerParams(dimension_semantics=("arbitrary", "arbitrary", "arbitrary"),
                                             vmem_limit_bytes=V7X_VMEM_LIMIT_BYTES),
        name="ret",
    )(chunk_decay, rq, rk, rv, sg, d_intra, kdec, qdec)


def _merge_kernel(ya_ref, yr_ref, ga_ref, gr_ref, x_ref, mod_ref, pa_ref, pr_ref, wo_ref, lng_ref, lnb_ref,
                  x1_ref, h2_ref, *, d, alpha):
    a = jnp.dot(ya_ref[...], pa_ref[...], preferred_element_type=F32)
    r = jnp.dot(yr_ref[...], pr_ref[...], preferred_element_type=F32)
    merged = ga_ref[...].astype(F32) * a + gr_ref[...].astype(F32) * r
    y = jnp.dot(merged.astype(BF16), wo_ref[...], preferred_element_type=F32)
    g1 = mod_ref[:, 2 * d:3 * d]
    x1 = _layer_norm_rows(alpha * x_ref[...] + g1 * y, lng_ref[...], lnb_ref[...])
    x1_ref[...] = x1
    sh2 = mod_ref[:, 3 * d:4 * d]
    sc2 = mod_ref[:, 4 * d:5 * d]
    h2_ref[...] = (x1 * (1.0 + sc2) + sh2).astype(BF16)


def _merge(ya, yr, ga, gr, x, mod, w_pa, w_pr, w_o, ln_g, ln_b, alpha):
    b, s, d = x.shape
    tm = ROW_TILE
    row = lambda bi, t: (bi, t, 0)
    pa, pr, wo = w_pa.astype(BF16), w_pr.astype(BF16), w_o.astype(BF16)
    return pl.pallas_call(
        functools.partial(_merge_kernel, d=d, alpha=alpha),
        out_shape=(jax.ShapeDtypeStruct((b, s, d), F32), jax.ShapeDtypeStruct((b, s, d), BF16)),
        grid=(b, s // tm),
        in_specs=[pl.BlockSpec((None, tm, ya.shape[-1]), row),
                  pl.BlockSpec((None, tm, yr.shape[-1]), row),
                  pl.BlockSpec((None, tm, d), row),
                  pl.BlockSpec((None, tm, d), row),
                  pl.BlockSpec((None, tm, d), row),
                  pl.BlockSpec((None, 1, N_MOD * d), lambda bi, t: (bi, 0, 0)),
                  _resident(pa.shape), _resident(pr.shape), _resident(wo.shape),
                  _resident((1, d)), _resident((1, d))],
        out_specs=(pl.BlockSpec((None, tm, d), row), pl.BlockSpec((None, tm, d), row)),
        compiler_params=pltpu.CompilerParams(dimension_semantics=("arbitrary", "arbitrary"),
                                             vmem_limit_bytes=V7X_VMEM_LIMIT_BYTES),
        name="merge",
    )(ya, yr, ga, gr, x, mod, pa, pr, wo, ln_g.reshape(1, d), ln_b.reshape(1, d))


def _shift_rows(g, prev, k):
    rolled = pltpu.roll(g, k, 0)
    sub = lax.broadcasted_iota(jnp.int32, prev.shape, 0)
    top = jnp.where(sub < k, pltpu.roll(prev, k, 0), rolled[:prev.shape[0], :])
    return jnp.concatenate([top, rolled[prev.shape[0]:, :]], axis=0)


def _ffn_kernel(h_ref, x1_ref, mod_ref, wg_ref, wu_ref, wd_ref, cw_ref, cb_ref, lng_ref, lnb_ref,
                o_ref, tail_ref, acc_ref, *, d, alpha):
    tm = h_ref.shape[0]

    @pl.when(pl.program_id(1) == 0)
    def _():
        tail_ref[...] = jnp.zeros_like(tail_ref)

    h = h_ref[...]
    fc = FF_CHUNK
    for c in range(wg_ref.shape[1] // fc):
        cs = slice(c * fc, (c + 1) * fc)
        g = jnp.dot(h, wg_ref[:, cs], preferred_element_type=F32)
        u = jnp.dot(h, wu_ref[:, cs], preferred_element_type=F32)
        prev = tail_ref[c]
        tail_ref[c] = g[tm - 8:, :]
        gc = (_shift_rows(g, prev, 2) * cw_ref[0:1, cs] + _shift_rows(g, prev, 1) * cw_ref[1:2, cs]
              + g * cw_ref[2:3, cs] + cb_ref[:, cs])
        act = 0.5 * gc * (1.0 + lax.erf(gc * (2.0 ** -0.5))) * u
        part = jnp.dot(act.astype(BF16), wd_ref[cs, :], preferred_element_type=F32)
        if c == 0:
            acc_ref[...] = part
        else:
            acc_ref[...] += part
    g2 = mod_ref[:, 5 * d:6 * d]
    o_ref[...] = _layer_norm_rows(alpha * x1_ref[...] + g2 * acc_ref[...], lng_ref[...], lnb_ref[...])


def _ffn(h2, x1, mod, w_gate, w_up, conv_w, conv_b, w_down, ln_g, ln_b, alpha):
    b, s, d = x1.shape
    dff = w_gate.shape[1]
    tm = ROW_TILE
    row = lambda bi, t: (bi, t, 0)
    wg, wu, wd = w_gate.astype(BF16), w_up.astype(BF16), w_down.astype(BF16)
    return pl.pallas_call(
        functools.partial(_ffn_kernel, d=d, alpha=alpha),
        out_shape=jax.ShapeDtypeStruct((b, s, d), F32),
        grid=(b, s // tm),
        in_specs=[pl.BlockSpec((None, tm, d), row),
                  pl.BlockSpec((None, tm, d), row),
                  pl.BlockSpec((None, 1, N_MOD * d), lambda bi, t: (bi, 0, 0)),
                  _resident(wg.shape), _resident(wu.shape), _resident(wd.shape),
                  _resident(conv_w.shape), _resident((1, dff)),
                  _resident((1, d)), _resident((1, d))],
        out_specs=pl.BlockSpec((None, tm, d), row),
        scratch_shapes=[pltpu.VMEM((dff // FF_CHUNK, 8, FF_CHUNK), F32), pltpu.VMEM((tm, d), F32)],
        compiler_params=pltpu.CompilerParams(dimension_semantics=("arbitrary", "arbitrary"),
                                             vmem_limit_bytes=V7X_VMEM_LIMIT_BYTES),
        name="ffn",
    )(h2, x1, mod, wg, wu, wd, conv_w, conv_b.reshape(1, dff), ln_g.reshape(1, d), ln_b.reshape(1, d))


def kernel(x, c, w_ada, b_ada, w_in, w_proj_moba, w_proj_ret, w_out, ln1_g, ln1_b, w_ff_gate, w_ff_up,
           ff_conv_w, ff_conv_b, w_ff_down, ln2_g, ln2_b):
    depth = w_ada.shape[0]
    alpha = (2.0 * depth) ** 0.25
    for l in range(depth):
        mod = _ada(c, w_ada[l], b_ada[l])
        qt, k, vt, rq, rk, rv, sg, ga, gr = _inproj(x, mod, w_in[l])
        ya = _moba(qt, k, vt)
        yr = _retention(rq, rk, rv, sg)
        x1, h2 = _merge(ya, yr, ga, gr, x, mod, w_proj_moba[l], w_proj_ret[l], w_out[l],
                        ln1_g[l], ln1_b[l], alpha)
        x = _ffn(h2, x1, mod, w_ff_gate[l], w_ff_up[l], ff_conv_w[l], ff_conv_b[l], w_ff_down[l],
                 ln2_g[l], ln2_b[l], alpha)
    return x
```

```python
import functools

import jax
import jax.numpy as jnp
import numpy as np
from jax import lax
from jax.experimental import pallas as pl
from jax.experimental.pallas import tpu as pltpu

F32 = jnp.float32
BF16 = jnp.bfloat16

MOBA_HEADS = 8
MOBA_HEAD_DIM = 64
MOBA_BLOCK = 256
MOBA_TOPK = 3
ROPE_THETA = 10000.0
RET_HEADS = 4
RET_QK_DIM = 128
RET_V_DIM = 256
RET_CHUNK = 128
RET_ANGLE_BASE = 10000.0
N_MOD = 6
LN_EPS = 1e-5
NEG_BIG = -1e30
BELOW_NEG_BIG = -3e38
CONV_WIDTH = 3

V7X_VMEM_LIMIT_BYTES = 56 * 1024 * 1024
LANES = 128
BF16_SUBLANES = 16
LOG2_E = 1.4426950408889634
ROW_TILE = 512
FF_CHUNK = 256
MOBA_HEADS_PER_STEP = 4
MOBA_GROUP_LOG2 = 1


def _resident(shape):
    nd = len(shape)
    return pl.BlockSpec(shape, lambda *_: (0,) * nd, pipeline_mode=pl.Buffered(1))


def _layer_norm_rows(z, g, b):
    mu = jnp.mean(z, axis=-1, keepdims=True)
    zc = z - mu
    var = jnp.mean(zc * zc, axis=-1, keepdims=True)
    return zc * lax.rsqrt(var + LN_EPS) * g + b


def _ada_kernel(c_ref, w_ref, b_ref, o_ref):
    c = c_ref[...]
    a = c * jax.nn.sigmoid(c)
    o_ref[...] = jnp.dot(a, w_ref[...], preferred_element_type=F32,
                         precision=lax.Precision.HIGHEST) + b_ref[...]


def _ada(c, w_ada, b_ada):
    b, d = c.shape
    n = w_ada.shape[1]
    rows = 8
    cp = jnp.pad(c, ((0, rows - b), (0, 0)))
    nblk = 1536
    out = pl.pallas_call(
        _ada_kernel,
        out_shape=jax.ShapeDtypeStruct((rows, n), F32),
        grid=(n // nblk,),
        in_specs=[pl.BlockSpec((rows, d), lambda j: (0, 0)),
                  pl.BlockSpec((d, nblk), lambda j: (0, j)),
                  pl.BlockSpec((1, nblk), lambda j: (0, j))],
        out_specs=pl.BlockSpec((rows, nblk), lambda j: (0, j)),
        compiler_params=pltpu.CompilerParams(dimension_semantics=("arbitrary",),
                                             vmem_limit_bytes=V7X_VMEM_LIMIT_BYTES),
        name="ada",
    )(cp, w_ada, b_ada.reshape(1, n))
    return out[:b].reshape(b, 1, n)


def _inproj_kernel(x_ref, mod_ref, wn_ref, wt_ref, ck_ref, sk_ref, cr_ref, sr_ref, cqt_ref, sqt_ref,
                   qt_ref, k_ref, vt_ref, rq_ref, rk_ref, rv_ref, sg_ref, ga_ref, gr_ref, *, d):
    tm = x_ref.shape[0]
    mw = MOBA_HEADS * MOBA_HEAD_DIM
    rw = RET_HEADS * RET_QK_DIM
    vw = RET_HEADS * RET_V_DIM
    sh1 = mod_ref[:, 0:d]
    sc1 = mod_ref[:, d:2 * d]
    h = (x_ref[...] * (1.0 + sc1) + sh1).astype(BF16)

    def nat(col0, width):
        return jnp.dot(h, wn_ref[:, col0:col0 + width], preferred_element_type=F32)

    def trans(row0, width):
        return lax.dot_general(wt_ref[row0:row0 + width, :], h, (((1,), (1,)), ((), ())),
                               preferred_element_type=F32)

    lane = lax.broadcasted_iota(jnp.int32, (tm, LANES), 1)

    qt = trans(0, mw)
    half = MOBA_HEAD_DIM // 2
    cq = cqt_ref[...]
    sq = sqt_ref[...]
    for hd in range(MOBA_HEADS):
        r0 = hd * MOBA_HEAD_DIM
        x1 = qt[r0:r0 + half, :]
        x2 = qt[r0 + half:r0 + 2 * half, :]
        o1 = (x1 * cq - x2 * sq).astype(BF16)
        o2 = (x2 * cq + x1 * sq).astype(BF16)
        for blk in range(tm // MOBA_BLOCK):
            cs = slice(blk * MOBA_BLOCK, (blk + 1) * MOBA_BLOCK)
            qt_ref[blk, r0:r0 + half, :] = o1[:, cs]
            qt_ref[blk, r0 + half:r0 + 2 * half, :] = o2[:, cs]

    kn = nat(0, mw)
    ck = ck_ref[...]
    sk = sk_ref[...]
    low = (lane & half) == 0
    for g in range(mw // LANES):
        xs = kn[:, g * LANES:(g + 1) * LANES]
        partner = jnp.where(low, pltpu.roll(xs, LANES - half, 1), pltpu.roll(xs, half, 1))
        res = (xs * ck + partner * sk).astype(BF16)
        for blk in range(tm // MOBA_BLOCK):
            k_ref[blk, :, g * LANES:(g + 1) * LANES] = res[blk * MOBA_BLOCK:(blk + 1) * MOBA_BLOCK, :]

    vt = trans(mw, mw).astype(BF16)
    for blk in range(tm // MOBA_BLOCK):
        vt_ref[blk] = vt[:, blk * MOBA_BLOCK:(blk + 1) * MOBA_BLOCK]

    cr = cr_ref[...]
    sr = sr_ref[...]
    even = (lane & 1) == 0
    for col0, dst, scale in ((mw, rq_ref, None), (mw + rw, rk_ref, RET_QK_DIM ** -0.5)):
        rn = nat(col0, rw)
        for g in range(rw // LANES):
            xs = rn[:, g * LANES:(g + 1) * LANES]
            partner = jnp.where(even, pltpu.roll(xs, LANES - 1, 1), pltpu.roll(xs, 1, 1))
            res = xs * cr + partner * sr
            if scale is not None:
                res = res * scale
            dst[:, g * LANES:(g + 1) * LANES] = res.astype(BF16)

    col = mw + 2 * rw
    for c0 in range(0, vw, 512):
        rv_ref[:, c0:c0 + 512] = nat(col + c0, 512).astype(BF16)
    col += vw
    for c0 in range(0, vw, 512):
        z = nat(col + c0, 512)
        sg_ref[:, c0:c0 + 512] = (z * jax.nn.sigmoid(z)).astype(BF16)
    col += vw
    for dst in (ga_ref, gr_ref):
        for c0 in range(0, d, 512):
            dst[:, c0:c0 + 512] = jax.nn.sigmoid(nat(col + c0, 512)).astype(BF16)
        col += d


def _rope_tables(s):
    pos = jnp.arange(s, dtype=F32)
    hd = MOBA_HEAD_DIM
    inv_freq = ROPE_THETA ** (-jnp.arange(0, hd, 2, dtype=F32) / hd)
    ang = pos[:, None] * inv_freq[None, :]
    cos, sin = jnp.cos(ang), jnp.sin(ang)
    ck = jnp.tile(cos, (1, 2 * LANES // hd))
    sk = jnp.tile(jnp.concatenate([-sin, sin], axis=1), (1, LANES // hd))
    dk = RET_QK_DIM
    freq = 1.0 / (RET_ANGLE_BASE ** jnp.linspace(0.0, 1.0, dk // 2, dtype=F32))
    angr = pos[:, None] * freq[None, :]
    cosr, sinr = jnp.cos(angr), jnp.sin(angr)
    cr = jnp.repeat(cosr, 2, axis=1)
    sr = jnp.stack([-sinr, sinr], axis=-1).reshape(s, dk)
    q_scale = LOG2_E * hd ** -0.5
    return ck, sk, cr, sr, cos.T * q_scale, sin.T * q_scale


def _inproj(x, mod, w_in):
    b, s, d = x.shape
    mw = MOBA_HEADS * MOBA_HEAD_DIM
    rw = RET_HEADS * RET_QK_DIM
    vw = RET_HEADS * RET_V_DIM
    tm = ROW_TILE
    nb = s // MOBA_BLOCK
    bpt = tm // MOBA_BLOCK
    w_t = jnp.concatenate([w_in[:, 0:mw], w_in[:, 2 * mw:3 * mw]], axis=1).T.astype(BF16)
    w_n = jnp.concatenate([w_in[:, mw:2 * mw], w_in[:, 3 * mw:]], axis=1).astype(BF16)
    ck, sk, cr, sr, cqt, sqt = _rope_tables(s)
    row = lambda bi, t: (bi, t, 0)
    blk4 = lambda bi, t: (bi, t, 0, 0)
    tab = lambda bi, t: (t, 0)
    out_shapes = (
        jax.ShapeDtypeStruct((b, nb, mw, MOBA_BLOCK), BF16),
        jax.ShapeDtypeStruct((b, nb, MOBA_BLOCK, mw), BF16),
        jax.ShapeDtypeStruct((b, nb, mw, MOBA_BLOCK), BF16),
        jax.ShapeDtypeStruct((b, s, rw), BF16),
        jax.ShapeDtypeStruct((b, s, rw), BF16),
        jax.ShapeDtypeStruct((b, s, vw), BF16),
        jax.ShapeDtypeStruct((b, s, vw), BF16),
        jax.ShapeDtypeStruct((b, s, d), BF16),
        jax.ShapeDtypeStruct((b, s, d), BF16),
    )
    out_specs = (
        pl.BlockSpec((None, bpt, mw, MOBA_BLOCK), blk4),
        pl.BlockSpec((None, bpt, MOBA_BLOCK, mw), blk4),
        pl.BlockSpec((None, bpt, mw, MOBA_BLOCK), blk4),
        pl.BlockSpec((None, tm, rw), row),
        pl.BlockSpec((None, tm, rw), row),
        pl.BlockSpec((None, tm, vw), row),
        pl.BlockSpec((None, tm, vw), row),
        pl.BlockSpec((None, tm, d), row),
        pl.BlockSpec((None, tm, d), row),
    )
    in_specs = [
        pl.BlockSpec((None, tm, d), row),
        pl.BlockSpec((None, 1, N_MOD * d), lambda bi, t: (bi, 0, 0)),
        _resident(w_n.shape),
        _resident(w_t.shape),
        pl.BlockSpec((tm, LANES), tab), pl.BlockSpec((tm, LANES), tab),
        pl.BlockSpec((tm, LANES), tab), pl.BlockSpec((tm, LANES), tab),
        pl.BlockSpec((MOBA_HEAD_DIM // 2, tm), lambda bi, t: (0, t)),
        pl.BlockSpec((MOBA_HEAD_DIM // 2, tm), lambda bi, t: (0, t)),
    ]
    return pl.pallas_call(
        functools.partial(_inproj_kernel, d=d),
        out_shape=out_shapes,
        grid=(b, s // tm),
        in_specs=in_specs,
        out_specs=out_specs,
        compiler_params=pltpu.CompilerParams(dimension_semantics=("arbitrary", "arbitrary"),
                                             vmem_limit_bytes=V7X_VMEM_LIMIT_BYTES),
        name="inproj",
    )(x, mod, w_n, w_t, ck, sk, cr, sr, cqt, sqt)


def _moba_kernel(qt_ref, k_ref, vt_ref, hot_ref, o_ref, kmean_ref, qa_ref, sown_ref, s_ref):
    n = pl.program_id(2)
    nb = k_ref.shape[0]
    hd = MOBA_HEAD_DIM
    blk = MOBA_BLOCK
    hs = MOBA_HEADS_PER_STEP
    grp = 1 << MOBA_GROUP_LOG2
    sub = 8

    @pl.when(n == 0)
    def _():
        for j in range(nb):
            kmean_ref[j:j + 1, :] = jnp.mean(k_ref[j].astype(F32), axis=0, keepdims=True)

    qt = qt_ref[...]
    feat = lax.broadcasted_iota(jnp.int32, (LANES, blk), 0)
    km = kmean_ref[...]
    km_lane = lax.broadcasted_iota(jnp.int32, km.shape, 1)
    blk_id = lax.broadcasted_iota(jnp.int32, (nb, blk), 0)
    valid = blk_id < n
    blk_f = blk_id.astype(F32)

    pad_rows = jnp.zeros((LANES - nb, blk), BF16)
    for hh in range(hs):
        pair0 = (hh // 2) * LANES
        in_head = (feat >= hh * hd - pair0) & (feat < (hh + 1) * hd - pair0)
        kmh = jnp.where((km_lane >= hh * hd) & (km_lane < (hh + 1) * hd), km, 0.0)
        km_hi = kmh.astype(BF16)
        km_lo = (kmh - km_hi.astype(F32)).astype(BF16)
        gate = (jnp.dot(km_hi, qt, preferred_element_type=F32)
                + jnp.dot(km_lo, qt, preferred_element_type=F32))
        gate = jnp.where(valid, gate, NEG_BIG)
        taken = jnp.zeros((nb, blk), F32)
        for _ in range(MOBA_TOPK):
            best = jnp.max(gate, axis=0, keepdims=True)
            first = jnp.min(jnp.where(gate == best, blk_f, float(nb)), axis=0, keepdims=True)
            pick = blk_f == first
            taken = jnp.where(pick, 1.0, taken)
            gate = jnp.where(pick, BELOW_NEG_BIG, gate)
        qa_ref[hh, 0:LANES] = jnp.where(in_head, qt[pair0:pair0 + LANES], 0.0).astype(BF16)
        qa_ref[hh, LANES:LANES + nb] = jnp.where(valid & (taken > 0.5), 0.0, NEG_BIG).astype(BF16)
        qa_ref[hh, LANES + nb:2 * LANES] = pad_rows

    def col_max(st):
        return jnp.max(st.reshape(st.shape[0] // sub, sub, blk), axis=0)

    key_i = lax.broadcasted_iota(jnp.int32, (blk, blk), 0)
    qry_i = lax.broadcasted_iota(jnp.int32, (blk, blk), 1)
    causal = key_i <= qry_i

    def row_max(st):
        return jnp.max(col_max(st), axis=0, keepdims=True)

    for hh in range(hs):
        pair0 = (hh // 2) * LANES
        st = jnp.dot(k_ref[n, :, pair0:pair0 + LANES], qa_ref[hh, 0:LANES], preferred_element_type=F32)
        sown_ref[hh] = jnp.where(causal, st, NEG_BIG)

    n_groups = lax.shift_right_logical(n + (grp - 1), MOBA_GROUP_LOG2)

    def score_group(gi, slot, m_seen):
        j0 = pl.multiple_of(grp * gi, grp)
        hot = hot_ref[pl.ds(j0, grp)].reshape(grp * blk, LANES)
        keys = [jnp.concatenate([k_ref[pl.ds(j0, grp), :, p0:p0 + LANES].reshape(grp * blk, LANES), hot], axis=1)
                for p0 in range(0, k_ref.shape[2], LANES)]
        out = []
        for hh in range(hs):
            st = jnp.dot(keys[hh // 2], qa_ref[hh], preferred_element_type=F32)
            s_ref[slot, hh] = st.reshape(grp, blk, blk)
            out.append(jnp.maximum(m_seen[hh], row_max(st)))
        return out

    ones = jnp.ones((BF16_SUBLANES, blk), BF16)

    def absorb(acc, m_old, m_new, scores, values):
        p = jnp.exp2(scores - m_new).astype(BF16)
        return acc * jnp.exp2(m_old - m_new) + jnp.dot(values, p, preferred_element_type=F32)

    neg = jnp.full((1, blk), NEG_BIG, F32)
    last_group = nb // grp - 1
    m_first = score_group(0, 0, [neg] * hs)

    def stage(cur, nxt, g_cur, g_next, m_used, m_seen, acc):
        m_next = score_group(jnp.minimum(g_next, last_group), nxt, m_seen)
        j0 = pl.multiple_of(grp * g_cur, grp)
        new_acc = []
        for hh in range(hs):
            values = jnp.concatenate(
                [jnp.concatenate([vt_ref[j0 + g, hh * hd:(hh + 1) * hd, :], ones], axis=0) for g in range(grp)],
                axis=1)
            new_acc.append(absorb(acc[hh], m_used[hh], m_seen[hh],
                                  s_ref[cur, hh].reshape(grp * blk, blk), values))
        return m_seen, m_next, new_acc

    def body(t, carry):
        m_used, m_seen, acc = carry[:hs], carry[hs:2 * hs], carry[2 * hs:]
        m_used, m_seen, acc = stage(0, 1, 2 * t, 2 * t + 1, m_used, m_seen, acc)
        m_used, m_seen, acc = stage(1, 0, 2 * t + 1, 2 * t + 2, m_used, m_seen, acc)
        return tuple(m_used) + tuple(m_seen) + tuple(acc)

    zero_acc = jnp.zeros((hd + BF16_SUBLANES, blk), F32)
    carry = lax.fori_loop(0, lax.shift_right_logical(n_groups + 1, 1), body,
                          tuple([neg] * hs) + tuple(m_first) + tuple([zero_acc] * hs))
    m_used, acc = carry[:hs], carry[2 * hs:]

    v_own = vt_ref[n]
    outs = []
    for hh in range(hs):
        s_own = sown_ref[hh]
        m_fin = jnp.maximum(m_used[hh], row_max(s_own))
        values = jnp.concatenate([v_own[hh * hd:(hh + 1) * hd, :], ones], axis=0)
        a = absorb(acc[hh], m_used[hh], m_fin, s_own, values)
        outs.append(a[:hd, :] * (1.0 / a[hd:hd + 1, :]))
    o_ref[...] = jnp.concatenate(outs, axis=0).T.astype(BF16)


def _moba(qt, k, vt):
    b, nb, mw, blk = qt.shape
    hs = MOBA_HEADS_PER_STEP
    pair = hs * MOBA_HEAD_DIM
    assert nb <= LANES
    hot = (lax.broadcasted_iota(jnp.int32, (nb, blk, LANES), 2)
           == lax.broadcasted_iota(jnp.int32, (nb, blk, LANES), 0)).astype(BF16)
    return pl.pallas_call(
        _moba_kernel,
        out_shape=jax.ShapeDtypeStruct((b, nb * blk, mw), BF16),
        grid=(b, mw // pair, nb),
        in_specs=[pl.BlockSpec((None, None, pair, blk), lambda bi, hp, n: (bi, n, hp, 0)),
                  pl.BlockSpec((None, nb, blk, pair), lambda bi, hp, n: (bi, 0, 0, hp)),
                  pl.BlockSpec((None, nb, pair, blk), lambda bi, hp, n: (bi, 0, hp, 0)),
                  _resident(hot.shape)],
        out_specs=pl.BlockSpec((None, blk, pair), lambda bi, hp, n: (bi, n, hp)),
        scratch_shapes=[pltpu.VMEM((nb, pair), F32),
                        pltpu.VMEM((hs, 2 * LANES, blk), BF16),
                        pltpu.VMEM((hs, blk, blk), F32),
                        pltpu.VMEM((2, hs, 1 << MOBA_GROUP_LOG2, blk, blk), F32)],
        compiler_params=pltpu.CompilerParams(dimension_semantics=("arbitrary", "arbitrary", "arbitrary"),
                                             vmem_limit_bytes=V7X_VMEM_LIMIT_BYTES),
        name="moba",
    )(qt, k, vt, hot)


def _ret_kernel(cdec_ref, q_ref, k_ref, v_ref, sg_ref, dint_ref, kdec_ref, qdec_ref, o_ref, state_ref):
    hi = pl.program_id(1)

    @pl.when(pl.program_id(2) == 0)
    def _():
        state_ref[...] = jnp.zeros_like(state_ref)

    c = RET_CHUNK
    dint = dint_ref[...]
    kdec = kdec_ref[...]
    qdec = qdec_ref[...]
    cd = cdec_ref[hi]
    s = state_ref[...]
    for ci in range(q_ref.shape[0] // c):
        rows = slice(ci * c, (ci + 1) * c)
        q = q_ref[rows, :]
        k = k_ref[rows, :]
        v = v_ref[rows, :]
        scores = lax.dot_general(q, k, (((1,), (1,)), ((), ())), preferred_element_type=F32) * dint
        inner = jnp.dot(scores.astype(BF16), v, preferred_element_type=F32)
        qd = (q.astype(F32) * qdec).astype(BF16)
        inner = inner + jnp.dot(qd, s.astype(BF16), preferred_element_type=F32)
        kd = (k.astype(F32) * kdec).astype(BF16)
        kv = lax.dot_general(kd, v, (((0,), (0,)), ((), ())), preferred_element_type=F32)
        s = cd * s + kv
        mu = jnp.mean(inner, axis=-1, keepdims=True)
        xc = inner - mu
        var = jnp.mean(xc * xc, axis=-1, keepdims=True)
        y = xc * lax.rsqrt(var + LN_EPS) * sg_ref[rows, :].astype(F32)
        o_ref[rows, :] = y.astype(BF16)
    state_ref[...] = s


def _retention(rq, rk, rv, sg):
    b, s, _ = rq.shape
    h, dk, dv, c = RET_HEADS, RET_QK_DIM, RET_V_DIM, RET_CHUNK
    tm = ROW_TILE
    log_g = jnp.log(1.0 - 2.0 ** (-5.0 - jnp.arange(h, dtype=F32)))
    idx = jnp.arange(c, dtype=F32)
    diff = idx[:, None] - idx[None, :]
    d_intra = jnp.where(diff >= 0, jnp.exp(log_g[:, None, None] * jnp.maximum(diff, 0.0)), 0.0)
    k_decay = jnp.exp(log_g[:, None] * (c - 1.0 - idx)[None, :])
    q_decay = jnp.exp(log_g[:, None] * (idx + 1.0)[None, :])
    chunk_decay = jnp.exp(log_g * c)
    kdec = jnp.broadcast_to(k_decay[:, :, None], (h, c, dk))
    qdec = jnp.broadcast_to(q_decay[:, :, None], (h, c, dk))
    head_rows = lambda bi, hi, t: (bi, t, hi)
    head_tab = lambda bi, hi, t: (hi, 0, 0)
    return pl.pallas_call(
        _ret_kernel,
        out_shape=jax.ShapeDtypeStruct((b, s, h * dv), BF16),
        grid=(b, h, s // tm),
        in_specs=[pl.BlockSpec(memory_space=pltpu.SMEM),
                  pl.BlockSpec((None, tm, dk), head_rows),
                  pl.BlockSpec((None, tm, dk), head_rows),
                  pl.BlockSpec((None, tm, dv), head_rows),
                  pl.BlockSpec((None, tm, dv), head_rows),
                  pl.BlockSpec((None, c, c), head_tab),
                  pl.BlockSpec((None, c, dk), head_tab),
                  pl.BlockSpec((None, c, dk), head_tab)],
        out_specs=pl.BlockSpec((None, tm, dv), head_rows),
        scratch_shapes=[pltpu.VMEM((dk, dv), F32)],
        compiler_params=pltpu.CompilerParams(dimension_semantics=("arbitrary", "arbitrary", "arbitrary"),
                                             vmem_limit_bytes=V7X_VMEM_LIMIT_BYTES),
        name="ret",
    )(chunk_decay, rq, rk, rv, sg, d_intra, kdec, qdec)


def _merge_kernel(ya_ref, yr_ref, ga_ref, gr_ref, x_ref, mod_ref, pa_ref, pr_ref, wo_ref, lng_ref, lnb_ref,
                  x1_ref, h2_ref, *, d, alpha):
    a = jnp.dot(ya_ref[...], pa_ref[...], preferred_element_type=F32)
    r = jnp.dot(yr_ref[...], pr_ref[...], preferred_element_type=F32)
    merged = ga_ref[...].astype(F32) * a + gr_ref[...].astype(F32) * r
    y = jnp.dot(merged.astype(BF16), wo_ref[...], preferred_element_type=F32)
    g1 = mod_ref[:, 2 * d:3 * d]
    x1 = _layer_norm_rows(alpha * x_ref[...] + g1 * y, lng_ref[...], lnb_ref[...])
    x1_ref[...] = x1
    sh2 = mod_ref[:, 3 * d:4 * d]
    sc2 = mod_ref[:, 4 * d:5 * d]
    h2_ref[...] = (x1 * (1.0 + sc2) + sh2).astype(BF16)


def _merge(ya, yr, ga, gr, x, mod, w_pa, w_pr, w_o, ln_g, ln_b, alpha):
    b, s, d = x.shape
    tm = ROW_TILE
    row = lambda bi, t: (bi, t, 0)
    pa, pr, wo = w_pa.astype(BF16), w_pr.astype(BF16), w_o.astype(BF16)
    return pl.pallas_call(
        functools.partial(_merge_kernel, d=d, alpha=alpha),
        out_shape=(jax.ShapeDtypeStruct((b, s, d), F32), jax.ShapeDtypeStruct((b, s, d), BF16)),
        grid=(b, s // tm),
        in_specs=[pl.BlockSpec((None, tm, ya.shape[-1]), row),
                  pl.BlockSpec((None, tm, yr.shape[-1]), row),
                  pl.BlockSpec((None, tm, d), row),
                  pl.BlockSpec((None, tm, d), row),
                  pl.BlockSpec((None, tm, d), row),
                  pl.BlockSpec((None, 1, N_MOD * d), lambda bi, t: (bi, 0, 0)),
                  _resident(pa.shape), _resident(pr.shape), _resident(wo.shape),
                  _resident((1, d)), _resident((1, d))],
        out_specs=(pl.BlockSpec((None, tm, d), row), pl.BlockSpec((None, tm, d), row)),
        compiler_params=pltpu.CompilerParams(dimension_semantics=("arbitrary", "arbitrary"),
                                             vmem_limit_bytes=V7X_VMEM_LIMIT_BYTES),
        name="merge",
    )(ya, yr, ga, gr, x, mod, pa, pr, wo, ln_g.reshape(1, d), ln_b.reshape(1, d))


def _shift_rows(g, prev, k):
    rolled = pltpu.roll(g, k, 0)
    sub = lax.broadcasted_iota(jnp.int32, prev.shape, 0)
    top = jnp.where(sub < k, pltpu.roll(prev, k, 0), rolled[:prev.shape[0], :])
    return jnp.concatenate([top, rolled[prev.shape[0]:, :]], axis=0)


def _ffn_kernel(h_ref, x1_ref, mod_ref, wg_ref, wu_ref, wd_ref, cw_ref, cb_ref, lng_ref, lnb_ref,
                o_ref, tail_ref, acc_ref, *, d, alpha):
    tm = h_ref.shape[0]

    @pl.when(pl.program_id(1) == 0)
    def _():
        tail_ref[...] = jnp.zeros_like(tail_ref)

    h = h_ref[...]
    fc = FF_CHUNK
    for c in range(wg_ref.shape[1] // fc):
        cs = slice(c * fc, (c + 1) * fc)
        g = jnp.dot(h, wg_ref[:, cs], preferred_element_type=F32)
        u = jnp.dot(h, wu_ref[:, cs], preferred_element_type=F32)
        prev = tail_ref[c]
        tail_ref[c] = g[tm - 8:, :]
        gc = (_shift_rows(g, prev, 2) * cw_ref[0:1, cs] + _shift_rows(g, prev, 1) * cw_ref[1:2, cs]
              + g * cw_ref[2:3, cs] + cb_ref[:, cs])
        act = 0.5 * gc * (1.0 + lax.erf(gc * (2.0 ** -0.5))) * u
        part = jnp.dot(act.astype(BF16), wd_ref[cs, :], preferred_element_type=F32)
        if c == 0:
            acc_ref[...] = part
        else:
            acc_ref[...] += part
    g2 = mod_ref[:, 5 * d:6 * d]
    o_ref[...] = _layer_norm_rows(alpha * x1_ref[...] + g2 * acc_ref[...], lng_ref[...], lnb_ref[...])


def _ffn(h2, x1, mod, w_gate, w_up, conv_w, conv_b, w_down, ln_g, ln_b, alpha):
    b, s, d = x1.shape
    dff = w_gate.shape[1]
    tm = ROW_TILE
    row = lambda bi, t: (bi, t, 0)
    wg, wu, wd = w_gate.astype(BF16), w_up.astype(BF16), w_down.astype(BF16)
    return pl.pallas_call(
        functools.partial(_ffn_kernel, d=d, alpha=alpha),
        out_shape=jax.ShapeDtypeStruct((b, s, d), F32),
        grid=(b, s // tm),
        in_specs=[pl.BlockSpec((None, tm, d), row),
                  pl.BlockSpec((None, tm, d), row),
                  pl.BlockSpec((None, 1, N_MOD * d), lambda bi, t: (bi, 0, 0)),
                  _resident(wg.shape), _resident(wu.shape), _resident(wd.shape),
                  _resident(conv_w.shape), _resident((1, dff)),
                  _resident((1, d)), _resident((1, d))],
        out_specs=pl.BlockSpec((None, tm, d), row),
        scratch_shapes=[pltpu.VMEM((dff // FF_CHUNK, 8, FF_CHUNK), F32), pltpu.VMEM((tm, d), F32)],
        compiler_params=pltpu.CompilerParams(dimension_semantics=("arbitrary", "arbitrary"),
                                             vmem_limit_bytes=V7X_VMEM_LIMIT_BYTES),
        name="ffn",
    )(h2, x1, mod, wg, wu, wd, conv_w, conv_b.reshape(1, dff), ln_g.reshape(1, d), ln_b.reshape(1, d))


def kernel(x, c, w_ada, b_ada, w_in, w_proj_moba, w_proj_ret, w_out, ln1_g, ln1_b, w_ff_gate, w_ff_up,
           ff_conv_w, ff_conv_b, w_ff_down, ln2_g, ln2_b):
    depth = w_ada.shape[0]
    alpha = (2.0 * depth) ** 0.25
    for l in range(depth):
        mod = _ada(c, w_ada[l], b_ada[l])
        qt, k, vt, rq, rk, rv, sg, ga, gr = _inproj(x, mod, w_in[l])
        ya = _moba(qt, k, vt)
        yr = _retention(rq, rk, rv, sg)
        x1, h2 = _merge(ya, yr, ga, gr, x, mod, w_proj_moba[l], w_proj_ret[l], w_out[l],
                        ln1_g[l], ln1_b[l], alpha)
        x = _ffn(h2, x1, mod, w_ff_gate[l], w_ff_up[l], ff_conv_w[l], ff_conv_b[l], w_ff_down[l],
                 ln2_g[l], ln2_b[l], alpha)
    return x
```

```python
import functools

import jax
import jax.numpy as jnp
import numpy as np
from jax import lax
from jax.experimental import pallas as pl
from jax.experimental.pallas import tpu as pltpu

F32 = jnp.float32
BF16 = jnp.bfloat16

MOBA_HEADS = 8
MOBA_HEAD_DIM = 64
MOBA_BLOCK = 256
MOBA_TOPK = 3
ROPE_THETA = 10000.0
RET_HEADS = 4
RET_QK_DIM = 128
RET_V_DIM = 256
RET_CHUNK = 128
RET_ANGLE_BASE = 10000.0
N_MOD = 6
LN_EPS = 1e-5
NEG_BIG = -1e30
BELOW_NEG_BIG = -3e38
CONV_WIDTH = 3

V7X_VMEM_LIMIT_BYTES = 56 * 1024 * 1024
LANES = 128
BF16_SUBLANES = 16
LOG2_E = 1.4426950408889634
ROW_TILE = 512
FF_CHUNK = 256
RET_ROW_TILE = 4096
MERGE_SUBTILES = 2
MOBA_HEADS_PER_STEP = 4
MOBA_GROUP_LOG2 = 1


def _resident(shape):
    nd = len(shape)
    return pl.BlockSpec(shape, lambda *_: (0,) * nd, pipeline_mode=pl.Buffered(1))


def _layer_norm_rows(z, g, b):
    mu = jnp.mean(z, axis=-1, keepdims=True)
    zc = z - mu
    var = jnp.mean(zc * zc, axis=-1, keepdims=True)
    return zc * lax.rsqrt(var + LN_EPS) * g + b


def _ada_kernel(c_ref, w_ref, b_ref, o_ref):
    c = c_ref[...]
    a = c * jax.nn.sigmoid(c)
    o_ref[...] = jnp.dot(a, w_ref[...], preferred_element_type=F32,
                         precision=lax.Precision.HIGHEST) + b_ref[...]


def _ada(c, w_ada, b_ada):
    b, d = c.shape
    n = w_ada.shape[1]
    rows = 8
    cp = jnp.pad(c, ((0, rows - b), (0, 0)))
    nblk = 1536
    out = pl.pallas_call(
        _ada_kernel,
        out_shape=jax.ShapeDtypeStruct((rows, n), F32),
        grid=(n // nblk,),
        in_specs=[pl.BlockSpec((rows, d), lambda j: (0, 0)),
                  pl.BlockSpec((d, nblk), lambda j: (0, j)),
                  pl.BlockSpec((1, nblk), lambda j: (0, j))],
        out_specs=pl.BlockSpec((rows, nblk), lambda j: (0, j)),
        compiler_params=pltpu.CompilerParams(dimension_semantics=("arbitrary",),
                                             vmem_limit_bytes=V7X_VMEM_LIMIT_BYTES),
        name="ada",
    )(cp, w_ada, b_ada.reshape(1, n))
    return out[:b].reshape(b, 1, n)


def _inproj_kernel(x_ref, mod_ref, wn_ref, wt_ref, ck_ref, sk_ref, cr_ref, sr_ref, cqt_ref, sqt_ref,
                   qt_ref, k_ref, vt_ref, rq_ref, rk_ref, rv_ref, sg_ref, ga_ref, gr_ref, *, d):
    tm = x_ref.shape[0]
    mw = MOBA_HEADS * MOBA_HEAD_DIM
    rw = RET_HEADS * RET_QK_DIM
    vw = RET_HEADS * RET_V_DIM
    sh1 = mod_ref[:, 0:d]
    sc1 = mod_ref[:, d:2 * d]
    h = (x_ref[...] * (1.0 + sc1) + sh1).astype(BF16)

    def nat(col0, width):
        return jnp.dot(h, wn_ref[:, col0:col0 + width], preferred_element_type=F32)

    def trans(row0, width):
        return lax.dot_general(wt_ref[row0:row0 + width, :], h, (((1,), (1,)), ((), ())),
                               preferred_element_type=F32)

    lane = lax.broadcasted_iota(jnp.int32, (tm, LANES), 1)

    qt = trans(0, mw)
    half = MOBA_HEAD_DIM // 2
    cq = cqt_ref[...]
    sq = sqt_ref[...]
    for hd in range(MOBA_HEADS):
        r0 = hd * MOBA_HEAD_DIM
        x1 = qt[r0:r0 + half, :]
        x2 = qt[r0 + half:r0 + 2 * half, :]
        o1 = (x1 * cq - x2 * sq).astype(BF16)
        o2 = (x2 * cq + x1 * sq).astype(BF16)
        for blk in range(tm // MOBA_BLOCK):
            cs = slice(blk * MOBA_BLOCK, (blk + 1) * MOBA_BLOCK)
            qt_ref[blk, r0:r0 + half, :] = o1[:, cs]
            qt_ref[blk, r0 + half:r0 + 2 * half, :] = o2[:, cs]

    kn = nat(0, mw)
    ck = ck_ref[...]
    sk = sk_ref[...]
    low = (lane & half) == 0
    for g in range(mw // LANES):
        xs = kn[:, g * LANES:(g + 1) * LANES]
        partner = jnp.where(low, pltpu.roll(xs, LANES - half, 1), pltpu.roll(xs, half, 1))
        res = (xs * ck + partner * sk).astype(BF16)
        for blk in range(tm // MOBA_BLOCK):
            k_ref[blk, :, g * LANES:(g + 1) * LANES] = res[blk * MOBA_BLOCK:(blk + 1) * MOBA_BLOCK, :]

    vt = trans(mw, mw).astype(BF16)
    for blk in range(tm // MOBA_BLOCK):
        vt_ref[blk] = vt[:, blk * MOBA_BLOCK:(blk + 1) * MOBA_BLOCK]

    cr = cr_ref[...]
    sr = sr_ref[...]
    even = (lane & 1) == 0
    for col0, dst, scale in ((mw, rq_ref, None), (mw + rw, rk_ref, RET_QK_DIM ** -0.5)):
        rn = nat(col0, rw)
        for g in range(rw // LANES):
            xs = rn[:, g * LANES:(g + 1) * LANES]
            partner = jnp.where(even, pltpu.roll(xs, LANES - 1, 1), pltpu.roll(xs, 1, 1))
            res = xs * cr + partner * sr
            if scale is not None:
                res = res * scale
            dst[:, g * LANES:(g + 1) * LANES] = res.astype(BF16)

    col = mw + 2 * rw
    for c0 in range(0, vw, 512):
        rv_ref[:, c0:c0 + 512] = nat(col + c0, 512).astype(BF16)
    col += vw
    for c0 in range(0, vw, 512):
        z = nat(col + c0, 512)
        sg_ref[:, c0:c0 + 512] = (z * jax.nn.sigmoid(z)).astype(BF16)
    col += vw
    for dst in (ga_ref, gr_ref):
        for c0 in range(0, d, 512):
            dst[:, c0:c0 + 512] = jax.nn.sigmoid(nat(col + c0, 512)).astype(BF16)
        col += d


def _rope_tables(s):
    pos = jnp.arange(s, dtype=F32)
    hd = MOBA_HEAD_DIM
    inv_freq = ROPE_THETA ** (-jnp.arange(0, hd, 2, dtype=F32) / hd)
    ang = pos[:, None] * inv_freq[None, :]
    cos, sin = jnp.cos(ang), jnp.sin(ang)
    ck = jnp.tile(cos, (1, 2 * LANES // hd))
    sk = jnp.tile(jnp.concatenate([-sin, sin], axis=1), (1, LANES // hd))
    dk = RET_QK_DIM
    freq = 1.0 / (RET_ANGLE_BASE ** jnp.linspace(0.0, 1.0, dk // 2, dtype=F32))
    angr = pos[:, None] * freq[None, :]
    cosr, sinr = jnp.cos(angr), jnp.sin(angr)
    cr = jnp.repeat(cosr, 2, axis=1)
    sr = jnp.stack([-sinr, sinr], axis=-1).reshape(s, dk)
    q_scale = LOG2_E * hd ** -0.5
    return ck, sk, cr, sr, cos.T * q_scale, sin.T * q_scale


def _inproj(x, mod, w_in):
    b, s, d = x.shape
    mw = MOBA_HEADS * MOBA_HEAD_DIM
    rw = RET_HEADS * RET_QK_DIM
    vw = RET_HEADS * RET_V_DIM
    tm = ROW_TILE
    nb = s // MOBA_BLOCK
    bpt = tm // MOBA_BLOCK
    w_t = jnp.concatenate([w_in[:, 0:mw], w_in[:, 2 * mw:3 * mw]], axis=1).T.astype(BF16)
    w_n = jnp.concatenate([w_in[:, mw:2 * mw], w_in[:, 3 * mw:]], axis=1).astype(BF16)
    ck, sk, cr, sr, cqt, sqt = _rope_tables(s)
    row = lambda bi, t: (bi, t, 0)
    blk4 = lambda bi, t: (bi, t, 0, 0)
    tab = lambda bi, t: (t, 0)
    out_shapes = (
        jax.ShapeDtypeStruct((b, nb, mw, MOBA_BLOCK), BF16),
        jax.ShapeDtypeStruct((b, nb, MOBA_BLOCK, mw), BF16),
        jax.ShapeDtypeStruct((b, nb, mw, MOBA_BLOCK), BF16),
        jax.ShapeDtypeStruct((b, s, rw), BF16),
        jax.ShapeDtypeStruct((b, s, rw), BF16),
        jax.ShapeDtypeStruct((b, s, vw), BF16),
        jax.ShapeDtypeStruct((b, s, vw), BF16),
        jax.ShapeDtypeStruct((b, s, d), BF16),
        jax.ShapeDtypeStruct((b, s, d), BF16),
    )
    out_specs = (
        pl.BlockSpec((None, bpt, mw, MOBA_BLOCK), blk4),
        pl.BlockSpec((None, bpt, MOBA_BLOCK, mw), blk4),
        pl.BlockSpec((None, bpt, mw, MOBA_BLOCK), blk4),
        pl.BlockSpec((None, tm, rw), row),
        pl.BlockSpec((None, tm, rw), row),
        pl.BlockSpec((None, tm, vw), row),
        pl.BlockSpec((None, tm, vw), row),
        pl.BlockSpec((None, tm, d), row),
        pl.BlockSpec((None, tm, d), row),
    )
    in_specs = [
        pl.BlockSpec((None, tm, d), row),
        pl.BlockSpec((None, 1, N_MOD * d), lambda bi, t: (bi, 0, 0)),
        _resident(w_n.shape),
        _resident(w_t.shape),
        pl.BlockSpec((tm, LANES), tab), pl.BlockSpec((tm, LANES), tab),
        pl.BlockSpec((tm, LANES), tab), pl.BlockSpec((tm, LANES), tab),
        pl.BlockSpec((MOBA_HEAD_DIM // 2, tm), lambda bi, t: (0, t)),
        pl.BlockSpec((MOBA_HEAD_DIM // 2, tm), lambda bi, t: (0, t)),
    ]
    return pl.pallas_call(
        functools.partial(_inproj_kernel, d=d),
        out_shape=out_shapes,
        grid=(b, s // tm),
        in_specs=in_specs,
        out_specs=out_specs,
        compiler_params=pltpu.CompilerParams(dimension_semantics=("arbitrary", "arbitrary"),
                                             vmem_limit_bytes=V7X_VMEM_LIMIT_BYTES),
        name="inproj",
    )(x, mod, w_n, w_t, ck, sk, cr, sr, cqt, sqt)


def _moba_kernel(qt_ref, k_ref, vt_ref, hot_ref, o_ref, kmean_ref, qa_ref, sown_ref, s_ref):
    n = pl.program_id(2)
    nb = k_ref.shape[0]
    hd = MOBA_HEAD_DIM
    blk = MOBA_BLOCK
    hs = MOBA_HEADS_PER_STEP
    grp = 1 << MOBA_GROUP_LOG2
    sub = 8

    @pl.when(n == 0)
    def _():
        for j in range(nb):
            kmean_ref[j:j + 1, :] = jnp.mean(k_ref[j].astype(F32), axis=0, keepdims=True)

    qt = qt_ref[...]
    feat = lax.broadcasted_iota(jnp.int32, (LANES, blk), 0)
    km = kmean_ref[...]
    km_lane = lax.broadcasted_iota(jnp.int32, km.shape, 1)
    blk_id = lax.broadcasted_iota(jnp.int32, (nb, blk), 0)
    valid = blk_id < n
    blk_f = blk_id.astype(F32)

    pad_rows = jnp.zeros((LANES - nb, blk), BF16)
    for hh in range(hs):
        pair0 = (hh // 2) * LANES
        in_head = (feat >= hh * hd - pair0) & (feat < (hh + 1) * hd - pair0)
        kmh = jnp.where((km_lane >= hh * hd) & (km_lane < (hh + 1) * hd), km, 0.0)
        km_hi = kmh.astype(BF16)
        km_lo = (kmh - km_hi.astype(F32)).astype(BF16)
        gate = (jnp.dot(km_hi, qt, preferred_element_type=F32)
                + jnp.dot(km_lo, qt, preferred_element_type=F32))
        gate = jnp.where(valid, gate, NEG_BIG)
        taken = jnp.zeros((nb, blk), F32)
        for _ in range(MOBA_TOPK):
            best = jnp.max(gate, axis=0, keepdims=True)
            first = jnp.min(jnp.where(gate == best, blk_f, float(nb)), axis=0, keepdims=True)
            pick = blk_f == first
            taken = jnp.where(pick, 1.0, taken)
            gate = jnp.where(pick, BELOW_NEG_BIG, gate)
        qa_ref[hh, 0:LANES] = jnp.where(in_head, qt[pair0:pair0 + LANES], 0.0).astype(BF16)
        qa_ref[hh, LANES:LANES + nb] = jnp.where(valid & (taken > 0.5), 0.0, NEG_BIG).astype(BF16)
        qa_ref[hh, LANES + nb:2 * LANES] = pad_rows

    def col_max(st):
        return jnp.max(st.reshape(st.shape[0] // sub, sub, blk), axis=0)

    key_i = lax.broadcasted_iota(jnp.int32, (blk, blk), 0)
    qry_i = lax.broadcasted_iota(jnp.int32, (blk, blk), 1)
    causal = key_i <= qry_i

    def row_max(st):
        return jnp.max(col_max(st), axis=0, keepdims=True)

    for hh in range(hs):
        pair0 = (hh // 2) * LANES
        st = jnp.dot(k_ref[n, :, pair0:pair0 + LANES], qa_ref[hh, 0:LANES], preferred_element_type=F32)
        sown_ref[hh] = jnp.where(causal, st, NEG_BIG)

    n_groups = lax.shift_right_logical(n + (grp - 1), MOBA_GROUP_LOG2)

    def score_group(gi, slot, m_seen):
        j0 = pl.multiple_of(grp * gi, grp)
        hot = hot_ref[pl.ds(j0, grp)].reshape(grp * blk, LANES)
        keys = [jnp.concatenate([k_ref[pl.ds(j0, grp), :, p0:p0 + LANES].reshape(grp * blk, LANES), hot], axis=1)
                for p0 in range(0, k_ref.shape[2], LANES)]
        out = []
        for hh in range(hs):
            st = jnp.dot(keys[hh // 2], qa_ref[hh], preferred_element_type=F32)
            s_ref[slot, hh] = st.reshape(grp, blk, blk)
            out.append(jnp.maximum(m_seen[hh], row_max(st)))
        return out

    ones = jnp.ones((BF16_SUBLANES, blk), BF16)

    def absorb(acc, m_old, m_new, scores, values):
        p = jnp.exp2(scores - m_new).astype(BF16)
        return acc * jnp.exp2(m_old - m_new) + jnp.dot(values, p, preferred_element_type=F32)

    neg = jnp.full((1, blk), NEG_BIG, F32)
    last_group = nb // grp - 1
    m_first = score_group(0, 0, [neg] * hs)

    def stage(cur, nxt, g_cur, g_next, m_used, m_seen, acc):
        m_next = score_group(jnp.minimum(g_next, last_group), nxt, m_seen)
        j0 = pl.multiple_of(grp * g_cur, grp)
        new_acc = []
        for hh in range(hs):
            values = jnp.concatenate(
                [jnp.concatenate([vt_ref[j0 + g, hh * hd:(hh + 1) * hd, :], ones], axis=0) for g in range(grp)],
                axis=1)
            new_acc.append(absorb(acc[hh], m_used[hh], m_seen[hh],
                                  s_ref[cur, hh].reshape(grp * blk, blk), values))
        return m_seen, m_next, new_acc

    def body(t, carry):
        m_used, m_seen, acc = carry[:hs], carry[hs:2 * hs], carry[2 * hs:]
        m_used, m_seen, acc = stage(0, 1, 2 * t, 2 * t + 1, m_used, m_seen, acc)
        m_used, m_seen, acc = stage(1, 0, 2 * t + 1, 2 * t + 2, m_used, m_seen, acc)
        return tuple(m_used) + tuple(m_seen) + tuple(acc)

    zero_acc = jnp.zeros((hd + BF16_SUBLANES, blk), F32)
    carry = lax.fori_loop(0, lax.shift_right_logical(n_groups + 1, 1), body,
                          tuple([neg] * hs) + tuple(m_first) + tuple([zero_acc] * hs))
    m_used, acc = carry[:hs], carry[2 * hs:]

    v_own = vt_ref[n]
    outs = []
    for hh in range(hs):
        s_own = sown_ref[hh]
        m_fin = jnp.maximum(m_used[hh], row_max(s_own))
        values = jnp.concatenate([v_own[hh * hd:(hh + 1) * hd, :], ones], axis=0)
        a = absorb(acc[hh], m_used[hh], m_fin, s_own, values)
        outs.append(a[:hd, :] * (1.0 / a[hd:hd + 1, :]))
    o_ref[...] = jnp.concatenate(outs, axis=0).T.astype(BF16)


def _moba(qt, k, vt):
    b, nb, mw, blk = qt.shape
    hs = MOBA_HEADS_PER_STEP
    pair = hs * MOBA_HEAD_DIM
    assert nb <= LANES
    hot = (lax.broadcasted_iota(jnp.int32, (nb, blk, LANES), 2)
           == lax.broadcasted_iota(jnp.int32, (nb, blk, LANES), 0)).astype(BF16)
    return pl.pallas_call(
        _moba_kernel,
        out_shape=jax.ShapeDtypeStruct((b, nb * blk, mw), BF16),
        grid=(b, mw // pair, nb),
        in_specs=[pl.BlockSpec((None, None, pair, blk), lambda bi, hp, n: (bi, n, hp, 0)),
                  pl.BlockSpec((None, nb, blk, pair), lambda bi, hp, n: (bi, 0, 0, hp)),
                  pl.BlockSpec((None, nb, pair, blk), lambda bi, hp, n: (bi, 0, hp, 0)),
                  _resident(hot.shape)],
        out_specs=pl.BlockSpec((None, blk, pair), lambda bi, hp, n: (bi, n, hp)),
        scratch_shapes=[pltpu.VMEM((nb, pair), F32),
                        pltpu.VMEM((hs, 2 * LANES, blk), BF16),
                        pltpu.VMEM((hs, blk, blk), F32),
                        pltpu.VMEM((2, hs, 1 << MOBA_GROUP_LOG2, blk, blk), F32)],
        compiler_params=pltpu.CompilerParams(dimension_semantics=("arbitrary", "arbitrary", "arbitrary"),
                                             vmem_limit_bytes=V7X_VMEM_LIMIT_BYTES),
        name="moba",
    )(qt, k, vt, hot)


def _ret_kernel(cdec_ref, q_ref, k_ref, v_ref, sg_ref, dint_ref, kdec_ref, qdec_ref, o_ref, state_ref):
    hi = pl.program_id(1)

    @pl.when(pl.program_id(2) == 0)
    def _():
        state_ref[...] = jnp.zeros_like(state_ref)

    c = RET_CHUNK
    dint = dint_ref[...]
    kdec = kdec_ref[...]
    qdec = qdec_ref[...]
    cd = cdec_ref[hi]
    s = state_ref[...]
    for ci in range(q_ref.shape[0] // c):
        rows = slice(ci * c, (ci + 1) * c)
        q = q_ref[rows, :]
        k = k_ref[rows, :]
        v = v_ref[rows, :]
        scores = lax.dot_general(q, k, (((1,), (1,)), ((), ())), preferred_element_type=F32) * dint
        inner = jnp.dot(scores.astype(BF16), v, preferred_element_type=F32)
        qd = (q.astype(F32) * qdec).astype(BF16)
        inner = inner + jnp.dot(qd, s.astype(BF16), preferred_element_type=F32)
        kd = (k.astype(F32) * kdec).astype(BF16)
        kv = lax.dot_general(kd, v, (((0,), (0,)), ((), ())), preferred_element_type=F32)
        s = cd * s + kv
        mu = jnp.mean(inner, axis=-1, keepdims=True)
        xc = inner - mu
        var = jnp.mean(xc * xc, axis=-1, keepdims=True)
        y = xc * lax.rsqrt(var + LN_EPS) * sg_ref[rows, :].astype(F32)
        o_ref[rows, :] = y.astype(BF16)
    state_ref[...] = s


def _retention(rq, rk, rv, sg):
    b, s, _ = rq.shape
    h, dk, dv, c = RET_HEADS, RET_QK_DIM, RET_V_DIM, RET_CHUNK
    tm = min(RET_ROW_TILE, s)
    log_g = jnp.log(1.0 - 2.0 ** (-5.0 - jnp.arange(h, dtype=F32)))
    idx = jnp.arange(c, dtype=F32)
    diff = idx[:, None] - idx[None, :]
    d_intra = jnp.where(diff >= 0, jnp.exp(log_g[:, None, None] * jnp.maximum(diff, 0.0)), 0.0)
    k_decay = jnp.exp(log_g[:, None] * (c - 1.0 - idx)[None, :])
    q_decay = jnp.exp(log_g[:, None] * (idx + 1.0)[None, :])
    chunk_decay = jnp.exp(log_g * c)
    kdec = jnp.broadcast_to(k_decay[:, :, None], (h, c, dk))
    qdec = jnp.broadcast_to(q_decay[:, :, None], (h, c, dk))
    head_rows = lambda bi, hi, t: (bi, t, hi)
    head_tab = lambda bi, hi, t: (hi, 0, 0)
    return pl.pallas_call(
        _ret_kernel,
        out_shape=jax.ShapeDtypeStruct((b, s, h * dv), BF16),
        grid=(b, h, s // tm),
        in_specs=[pl.BlockSpec(memory_space=pltpu.SMEM),
                  pl.BlockSpec((None, tm, dk), head_rows),
                  pl.BlockSpec((None, tm, dk), head_rows),
                  pl.BlockSpec((None, tm, dv), head_rows),
                  pl.BlockSpec((None, tm, dv), head_rows),
                  pl.BlockSpec((None, c, c), head_tab),
                  pl.BlockSpec((None, c, dk), head_tab),
                  pl.BlockSpec((None, c, dk), head_tab)],
        out_specs=pl.BlockSpec((None, tm, dv), head_rows),
        scratch_shapes=[pltpu.VMEM((dk, dv), F32)],
        compiler_params=pltpu.CompilerParams(dimension_semantics=("arbitrary", "arbitrary", "arbitrary"),
                                             vmem_limit_bytes=V7X_VMEM_LIMIT_BYTES),
        name="ret",
    )(chunk_decay, rq, rk, rv, sg, d_intra, kdec, qdec)


def _merge_kernel(ya_ref, yr_ref, ga_ref, gr_ref, x_ref, mod_ref, pa_ref, pr_ref, wo_ref, lng_ref, lnb_ref,
                  x1_ref, h2_ref, *, d, alpha):
    g1 = mod_ref[:, 2 * d:3 * d]
    sh2 = mod_ref[:, 3 * d:4 * d]
    sc2 = mod_ref[:, 4 * d:5 * d]
    tm = x_ref.shape[0]
    sub = tm // MERGE_SUBTILES
    for i in range(MERGE_SUBTILES):
        rows = slice(i * sub, (i + 1) * sub)
        a = jnp.dot(ya_ref[rows, :], pa_ref[...], preferred_element_type=F32)
        r = jnp.dot(yr_ref[rows, :], pr_ref[...], preferred_element_type=F32)
        merged = ga_ref[rows, :].astype(F32) * a + gr_ref[rows, :].astype(F32) * r
        y = jnp.dot(merged.astype(BF16), wo_ref[...], preferred_element_type=F32)
        x1 = _layer_norm_rows(alpha * x_ref[rows, :] + g1 * y, lng_ref[...], lnb_ref[...])
        x1_ref[rows, :] = x1
        h2_ref[rows, :] = (x1 * (1.0 + sc2) + sh2).astype(BF16)


def _merge(ya, yr, ga, gr, x, mod, w_pa, w_pr, w_o, ln_g, ln_b, alpha):
    b, s, d = x.shape
    tm = ROW_TILE
    row = lambda bi, t: (bi, t, 0)
    pa, pr, wo = w_pa.astype(BF16), w_pr.astype(BF16), w_o.astype(BF16)
    return pl.pallas_call(
        functools.partial(_merge_kernel, d=d, alpha=alpha),
        out_shape=(jax.ShapeDtypeStruct((b, s, d), F32), jax.ShapeDtypeStruct((b, s, d), BF16)),
        grid=(b, s // tm),
        in_specs=[pl.BlockSpec((None, tm, ya.shape[-1]), row),
                  pl.BlockSpec((None, tm, yr.shape[-1]), row),
                  pl.BlockSpec((None, tm, d), row),
                  pl.BlockSpec((None, tm, d), row),
                  pl.BlockSpec((None, tm, d), row),
                  pl.BlockSpec((None, 1, N_MOD * d), lambda bi, t: (bi, 0, 0)),
                  _resident(pa.shape), _resident(pr.shape), _resident(wo.shape),
                  _resident((1, d)), _resident((1, d))],
        out_specs=(pl.BlockSpec((None, tm, d), row), pl.BlockSpec((None, tm, d), row)),
        compiler_params=pltpu.CompilerParams(dimension_semantics=("arbitrary", "arbitrary"),
                                             vmem_limit_bytes=V7X_VMEM_LIMIT_BYTES),
        name="merge",
    )(ya, yr, ga, gr, x, mod, pa, pr, wo, ln_g.reshape(1, d), ln_b.reshape(1, d))


def _shift_rows(g, prev, k):
    rolled = pltpu.roll(g, k, 0)
    sub = lax.broadcasted_iota(jnp.int32, prev.shape, 0)
    top = jnp.where(sub < k, pltpu.roll(prev, k, 0), rolled[:prev.shape[0], :])
    return jnp.concatenate([top, rolled[prev.shape[0]:, :]], axis=0)


def _ffn_kernel(h_ref, x1_ref, mod_ref, wg_ref, wu_ref, wd_ref, cw_ref, cb_ref, lng_ref, lnb_ref,
                o_ref, tail_ref, act_ref, *, d, alpha):
    tm = h_ref.shape[0]

    @pl.when(pl.program_id(1) == 0)
    def _():
        tail_ref[...] = jnp.zeros_like(tail_ref)

    h = h_ref[...]
    fc = FF_CHUNK
    for c in range(wg_ref.shape[1] // fc):
        cs = slice(c * fc, (c + 1) * fc)
        g = jnp.dot(h, wg_ref[:, cs], preferred_element_type=F32)
        u = jnp.dot(h, wu_ref[:, cs], preferred_element_type=F32)
        prev = tail_ref[c]
        tail_ref[c] = g[tm - 8:, :]
        gc = (_shift_rows(g, prev, 2) * cw_ref[0:1, cs] + _shift_rows(g, prev, 1) * cw_ref[1:2, cs]
              + g * cw_ref[2:3, cs] + cb_ref[:, cs])
        act_ref[:, cs] = (gc * (1.0 + lax.erf(gc * (2.0 ** -0.5))) * u).astype(BF16)
    y = jnp.dot(act_ref[...], wd_ref[...], preferred_element_type=F32)
    half_g2 = 0.5 * mod_ref[:, 5 * d:6 * d]
    o_ref[...] = _layer_norm_rows(alpha * x1_ref[...] + half_g2 * y, lng_ref[...], lnb_ref[...])


def _ffn(h2, x1, mod, w_gate, w_up, conv_w, conv_b, w_down, ln_g, ln_b, alpha):
    b, s, d = x1.shape
    dff = w_gate.shape[1]
    tm = ROW_TILE
    row = lambda bi, t: (bi, t, 0)
    wg, wu, wd = w_gate.astype(BF16), w_up.astype(BF16), w_down.astype(BF16)
    return pl.pallas_call(
        functools.partial(_ffn_kernel, d=d, alpha=alpha),
        out_shape=jax.ShapeDtypeStruct((b, s, d), F32),
        grid=(b, s // tm),
        in_specs=[pl.BlockSpec((None, tm, d), row),
                  pl.BlockSpec((None, tm, d), row),
                  pl.BlockSpec((None, 1, N_MOD * d), lambda bi, t: (bi, 0, 0)),
                  _resident(wg.shape), _resident(wu.shape), _resident(wd.shape),
                  _resident(conv_w.shape), _resident((1, dff)),
                  _resident((1, d)), _resident((1, d))],
        out_specs=pl.BlockSpec((None, tm, d), row),
        scratch_shapes=[pltpu.VMEM((dff // FF_CHUNK, 8, FF_CHUNK), F32),
                        pltpu.VMEM((tm, dff), BF16)],
        compiler_params=pltpu.CompilerParams(dimension_semantics=("arbitrary", "arbitrary"),
                                             vmem_limit_bytes=V7X_VMEM_LIMIT_BYTES),
        name="ffn",
    )(h2, x1, mod, wg, wu, wd, conv_w, conv_b.reshape(1, dff), ln_g.reshape(1, d), ln_b.reshape(1, d))


def kernel(x, c, w_ada, b_ada, w_in, w_proj_moba, w_proj_ret, w_out, ln1_g, ln1_b, w_ff_gate, w_ff_up,
           ff_conv_w, ff_conv_b, w_ff_down, ln2_g, ln2_b):
    depth = w_ada.shape[0]
    alpha = (2.0 * depth) ** 0.25
    for l in range(depth):
        mod = _ada(c, w_ada[l], b_ada[l])
        qt, k, vt, rq, rk, rv, sg, ga, gr = _inproj(x, mod, w_in[l])
        ya = _moba(qt, k, vt)
        yr = _retention(rq, rk, rv, sg)
        x1, h2 = _merge(ya, yr, ga, gr, x, mod, w_proj_moba[l], w_proj_ret[l], w_out[l],
                        ln1_g[l], ln1_b[l], alpha)
        x = _ffn(h2, x1, mod, w_ff_gate[l], w_ff_up[l], ff_conv_w[l], ff_conv_b[l], w_ff_down[l],
                 ln2_g[l], ln2_b[l], alpha)
    return x
```

```python
import functools

import jax
import jax.numpy as jnp
import numpy as np
from jax import lax
from jax.experimental import pallas as pl
from jax.experimental.pallas import tpu as pltpu

F32 = jnp.float32
BF16 = jnp.bfloat16

MOBA_HEADS = 8
MOBA_HEAD_DIM = 64
MOBA_BLOCK = 256
MOBA_TOPK = 3
ROPE_THETA = 10000.0
RET_HEADS = 4
RET_QK_DIM = 128
RET_V_DIM = 256
RET_CHUNK = 128
RET_ANGLE_BASE = 10000.0
N_MOD = 6
LN_EPS = 1e-5
NEG_BIG = -1e30
BELOW_NEG_BIG = -3e38
CONV_WIDTH = 3

V7X_VMEM_LIMIT_BYTES = 56 * 1024 * 1024
LANES = 128
BF16_SUBLANES = 16
LOG2_E = 1.4426950408889634
ROW_TILE = 512
FF_CHUNK = 256
RET_ROW_TILE = 4096
MERGE_SUBTILES = 2
MOBA_HEADS_PER_STEP = 4
MOBA_GROUP_LOG2 = 1


def _resident(shape):
    nd = len(shape)
    return pl.BlockSpec(shape, lambda *_: (0,) * nd, pipeline_mode=pl.Buffered(1))


def _layer_norm_rows(z, g, b):
    mu = jnp.mean(z, axis=-1, keepdims=True)
    zc = z - mu
    var = jnp.mean(zc * zc, axis=-1, keepdims=True)
    return zc * lax.rsqrt(var + LN_EPS) * g + b


def _ada_kernel(c_ref, w_ref, b_ref, o_ref):
    c = c_ref[...]
    a = c * jax.nn.sigmoid(c)
    o_ref[...] = jnp.dot(a, w_ref[...], preferred_element_type=F32,
                         precision=lax.Precision.HIGHEST) + b_ref[...]


def _ada(c, w_ada, b_ada):
    b, d = c.shape
    n = w_ada.shape[1]
    rows = 8
    cp = jnp.pad(c, ((0, rows - b), (0, 0)))
    nblk = 1536
    out = pl.pallas_call(
        _ada_kernel,
        out_shape=jax.ShapeDtypeStruct((rows, n), F32),
        grid=(n // nblk,),
        in_specs=[pl.BlockSpec((rows, d), lambda j: (0, 0)),
                  pl.BlockSpec((d, nblk), lambda j: (0, j)),
                  pl.BlockSpec((1, nblk), lambda j: (0, j))],
        out_specs=pl.BlockSpec((rows, nblk), lambda j: (0, j)),
        compiler_params=pltpu.CompilerParams(dimension_semantics=("arbitrary",),
                                             vmem_limit_bytes=V7X_VMEM_LIMIT_BYTES),
        name="ada",
    )(cp, w_ada, b_ada.reshape(1, n))
    return out[:b].reshape(b, 1, n)


def _inproj_kernel(x_ref, mod_ref, wn_ref, wvt_ref, ck_ref, sk_ref, cr_ref, sr_ref,
                   qt_ref, k_ref, vt_ref, rq_ref, rk_ref, rv_ref, sg_ref, ga_ref, gr_ref, *, d):
    tm = x_ref.shape[0]
    mw = MOBA_HEADS * MOBA_HEAD_DIM
    rw = RET_HEADS * RET_QK_DIM
    vw = RET_HEADS * RET_V_DIM
    sh1 = mod_ref[:, 0:d]
    sc1 = mod_ref[:, d:2 * d]
    h = (x_ref[...] * (1.0 + sc1) + sh1).astype(BF16)

    def nat(col0, width):
        return jnp.dot(h, wn_ref[:, col0:col0 + width], preferred_element_type=F32)

    lane = lax.broadcasted_iota(jnp.int32, (tm, LANES), 1)
    half = MOBA_HEAD_DIM // 2
    ck = ck_ref[...]
    sk = sk_ref[...]
    low = (lane & half) == 0

    def rope(xs):
        partner = jnp.where(low, pltpu.roll(xs, LANES - half, 1), pltpu.roll(xs, half, 1))
        return xs * ck + partner * sk

    q_scale = LOG2_E * MOBA_HEAD_DIM ** -0.5
    qn = nat(0, mw)
    qr = jnp.concatenate([rope(qn[:, g * LANES:(g + 1) * LANES]) for g in range(mw // LANES)], axis=1)
    qt = (qr * q_scale).T.astype(BF16)
    for blk in range(tm // MOBA_BLOCK):
        qt_ref[blk] = qt[:, blk * MOBA_BLOCK:(blk + 1) * MOBA_BLOCK]

    kn = nat(mw, mw)
    for g in range(mw // LANES):
        res = rope(kn[:, g * LANES:(g + 1) * LANES]).astype(BF16)
        for blk in range(tm // MOBA_BLOCK):
            k_ref[blk, :, g * LANES:(g + 1) * LANES] = res[blk * MOBA_BLOCK:(blk + 1) * MOBA_BLOCK, :]

    vt = lax.dot_general(wvt_ref[...], h, (((1,), (1,)), ((), ())), preferred_element_type=F32).astype(BF16)
    for blk in range(tm // MOBA_BLOCK):
        vt_ref[blk] = vt[:, blk * MOBA_BLOCK:(blk + 1) * MOBA_BLOCK]

    cr = cr_ref[...]
    sr = sr_ref[...]
    even = (lane & 1) == 0
    for col0, dst, scale in ((3 * mw, rq_ref, None), (3 * mw + rw, rk_ref, RET_QK_DIM ** -0.5)):
        rn = nat(col0, rw)
        for g in range(rw // LANES):
            xs = rn[:, g * LANES:(g + 1) * LANES]
            partner = jnp.where(even, pltpu.roll(xs, LANES - 1, 1), pltpu.roll(xs, 1, 1))
            res = xs * cr + partner * sr
            if scale is not None:
                res = res * scale
            dst[:, g * LANES:(g + 1) * LANES] = res.astype(BF16)

    col = 3 * mw + 2 * rw
    for c0 in range(0, vw, 512):
        rv_ref[:, c0:c0 + 512] = nat(col + c0, 512).astype(BF16)
    col += vw
    for c0 in range(0, vw, 512):
        z = nat(col + c0, 512)
        sg_ref[:, c0:c0 + 512] = (z * jax.nn.sigmoid(z)).astype(BF16)
    col += vw
    for dst in (ga_ref, gr_ref):
        for c0 in range(0, d, 512):
            dst[:, c0:c0 + 512] = jax.nn.sigmoid(nat(col + c0, 512)).astype(BF16)
        col += d


def _rope_tables(s):
    f = np.float32
    pos = np.arange(s, dtype=f)
    hd = MOBA_HEAD_DIM
    inv_freq = (f(ROPE_THETA) ** (-np.arange(0, hd, 2, dtype=f) / f(hd))).astype(f)
    ang = pos[:, None] * inv_freq[None, :]
    cos, sin = np.cos(ang).astype(f), np.sin(ang).astype(f)
    ck = np.tile(cos, (1, 2 * LANES // hd))
    sk = np.tile(np.concatenate([-sin, sin], axis=1), (1, LANES // hd))
    dk = RET_QK_DIM
    freq = (f(1.0) / (f(RET_ANGLE_BASE) ** np.linspace(0.0, 1.0, dk // 2, dtype=f))).astype(f)
    angr = pos[:, None] * freq[None, :]
    cosr, sinr = np.cos(angr).astype(f), np.sin(angr).astype(f)
    cr = np.repeat(cosr, 2, axis=1)
    sr = np.stack([-sinr, sinr], axis=-1).reshape(s, dk)
    return ck, sk, cr, sr


def _inproj(x, mod, w_in):
    b, s, d = x.shape
    mw = MOBA_HEADS * MOBA_HEAD_DIM
    rw = RET_HEADS * RET_QK_DIM
    vw = RET_HEADS * RET_V_DIM
    tm = ROW_TILE
    nb = s // MOBA_BLOCK
    bpt = tm // MOBA_BLOCK
    w_n = w_in.astype(BF16)
    w_vt = w_in[:, 2 * mw:3 * mw].T.astype(BF16)
    ck, sk, cr, sr = _rope_tables(s)
    row = lambda bi, t: (bi, t, 0)
    blk4 = lambda bi, t: (bi, t, 0, 0)
    tab = lambda bi, t: (t, 0)
    out_shapes = (
        jax.ShapeDtypeStruct((b, nb, mw, MOBA_BLOCK), BF16),
        jax.ShapeDtypeStruct((b, nb, MOBA_BLOCK, mw), BF16),
        jax.ShapeDtypeStruct((b, nb, mw, MOBA_BLOCK), BF16),
        jax.ShapeDtypeStruct((b, s, rw), BF16),
        jax.ShapeDtypeStruct((b, s, rw), BF16),
        jax.ShapeDtypeStruct((b, s, vw), BF16),
        jax.ShapeDtypeStruct((b, s, vw), BF16),
        jax.ShapeDtypeStruct((b, s, d), BF16),
        jax.ShapeDtypeStruct((b, s, d), BF16),
    )
    out_specs = (
        pl.BlockSpec((None, bpt, mw, MOBA_BLOCK), blk4),
        pl.BlockSpec((None, bpt, MOBA_BLOCK, mw), blk4),
        pl.BlockSpec((None, bpt, mw, MOBA_BLOCK), blk4),
        pl.BlockSpec((None, tm, rw), row),
        pl.BlockSpec((None, tm, rw), row),
        pl.BlockSpec((None, tm, vw), row),
        pl.BlockSpec((None, tm, vw), row),
        pl.BlockSpec((None, tm, d), row),
        pl.BlockSpec((None, tm, d), row),
    )
    in_specs = [
        pl.BlockSpec((None, tm, d), row),
        pl.BlockSpec((None, 1, N_MOD * d), lambda bi, t: (bi, 0, 0)),
        _resident(w_n.shape),
        _resident(w_vt.shape),
        pl.BlockSpec((tm, LANES), tab), pl.BlockSpec((tm, LANES), tab),
        pl.BlockSpec((tm, LANES), tab), pl.BlockSpec((tm, LANES), tab),
    ]
    return pl.pallas_call(
        functools.partial(_inproj_kernel, d=d),
        out_shape=out_shapes,
        grid=(b, s // tm),
        in_specs=in_specs,
        out_specs=out_specs,
        compiler_params=pltpu.CompilerParams(dimension_semantics=("arbitrary", "arbitrary"),
                                             vmem_limit_bytes=V7X_VMEM_LIMIT_BYTES),
        name="inproj",
    )(x, mod, w_n, w_vt, ck, sk, cr, sr)


def _moba_kernel(qt_ref, k_ref, vt_ref, hot_ref, o_ref, kmean_ref, qa_ref, sown_ref, s_ref):
    n = pl.program_id(2)
    nb = k_ref.shape[0]
    hd = MOBA_HEAD_DIM
    blk = MOBA_BLOCK
    hs = MOBA_HEADS_PER_STEP
    grp = 1 << MOBA_GROUP_LOG2
    sub = 8

    @pl.when(n == 0)
    def _():
        for j in range(nb):
            kmean_ref[j:j + 1, :] = jnp.mean(k_ref[j].astype(F32), axis=0, keepdims=True)

    qt = qt_ref[...]
    feat = lax.broadcasted_iota(jnp.int32, (LANES, blk), 0)
    km = kmean_ref[...]
    km_lane = lax.broadcasted_iota(jnp.int32, km.shape, 1)
    blk_id = lax.broadcasted_iota(jnp.int32, (nb, blk), 0)
    valid = blk_id < n
    blk_f = blk_id.astype(F32)

    pad_rows = jnp.zeros((LANES - nb, blk), BF16)
    for hh in range(hs):
        pair0 = (hh // 2) * LANES
        in_head = (feat >= hh * hd - pair0) & (feat < (hh + 1) * hd - pair0)
        kmh = jnp.where((km_lane >= hh * hd) & (km_lane < (hh + 1) * hd), km, 0.0)
        km_hi = kmh.astype(BF16)
        km_lo = (kmh - km_hi.astype(F32)).astype(BF16)
        gate = (jnp.dot(km_hi, qt, preferred_element_type=F32)
                + jnp.dot(km_lo, qt, preferred_element_type=F32))
        gate = jnp.where(valid, gate, NEG_BIG)
        taken = jnp.zeros((nb, blk), F32)
        for _ in range(MOBA_TOPK):
            best = jnp.max(gate, axis=0, keepdims=True)
            first = jnp.min(jnp.where(gate == best, blk_f, float(nb)), axis=0, keepdims=True)
            pick = blk_f == first
            taken = jnp.where(pick, 1.0, taken)
            gate = jnp.where(pick, BELOW_NEG_BIG, gate)
        qa_ref[hh, 0:LANES] = jnp.where(in_head, qt[pair0:pair0 + LANES], 0.0).astype(BF16)
        qa_ref[hh, LANES:LANES + nb] = jnp.where(valid & (taken > 0.5), 0.0, NEG_BIG).astype(BF16)
        qa_ref[hh, LANES + nb:2 * LANES] = pad_rows

    def col_max(st):
        return jnp.max(st.reshape(st.shape[0] // sub, sub, blk), axis=0)

    key_i = lax.broadcasted_iota(jnp.int32, (blk, blk), 0)
    qry_i = lax.broadcasted_iota(jnp.int32, (blk, blk), 1)
    causal = key_i <= qry_i

    def row_max(st):
        return jnp.max(col_max(st), axis=0, keepdims=True)

    for hh in range(hs):
        pair0 = (hh // 2) * LANES
        st = jnp.dot(k_ref[n, :, pair0:pair0 + LANES], qa_ref[hh, 0:LANES], preferred_element_type=F32)
        sown_ref[hh] = jnp.where(causal, st, NEG_BIG)

    n_groups = lax.shift_right_logical(n + (grp - 1), MOBA_GROUP_LOG2)

    def score_group(gi, slot, m_seen):
        j0 = pl.multiple_of(grp * gi, grp)
        hot = hot_ref[pl.ds(j0, grp)].reshape(grp * blk, LANES)
        keys = [jnp.concatenate([k_ref[pl.ds(j0, grp), :, p0:p0 + LANES].reshape(grp * blk, LANES), hot], axis=1)
                for p0 in range(0, k_ref.shape[2], LANES)]
        out = []
        for hh in range(hs):
            st = jnp.dot(keys[hh // 2], qa_ref[hh], preferred_element_type=F32)
            s_ref[slot, hh] = st.reshape(grp, blk, blk)
            out.append(jnp.maximum(m_seen[hh], row_max(st)))
        return out

    ones = jnp.ones((BF16_SUBLANES, blk), BF16)

    def absorb(acc, m_old, m_new, scores, values):
        p = jnp.exp2(scores - m_new).astype(BF16)
        return acc * jnp.exp2(m_old - m_new) + jnp.dot(values, p, preferred_element_type=F32)

    neg = jnp.full((1, blk), NEG_BIG, F32)
    last_group = nb // grp - 1
    m_first = score_group(0, 0, [neg] * hs)

    def stage(cur, nxt, g_cur, g_next, m_used, m_seen, acc):
        m_next = score_group(jnp.minimum(g_next, last_group), nxt, m_seen)
        j0 = pl.multiple_of(grp * g_cur, grp)
        new_acc = []
        for hh in range(hs):
            values = jnp.concatenate(
                [jnp.concatenate([vt_ref[j0 + g, hh * hd:(hh + 1) * hd, :], ones], axis=0) for g in range(grp)],
                axis=1)
            new_acc.append(absorb(acc[hh], m_used[hh], m_seen[hh],
                                  s_ref[cur, hh].reshape(grp * blk, blk), values))
        return m_seen, m_next, new_acc

    def body(t, carry):
        m_used, m_seen, acc = carry[:hs], carry[hs:2 * hs], carry[2 * hs:]
        m_used, m_seen, acc = stage(0, 1, 2 * t, 2 * t + 1, m_used, m_seen, acc)
        m_used, m_seen, acc = stage(1, 0, 2 * t + 1, 2 * t + 2, m_used, m_seen, acc)
        return tuple(m_used) + tuple(m_seen) + tuple(acc)

    zero_acc = jnp.zeros((hd + BF16_SUBLANES, blk), F32)
    carry = lax.fori_loop(0, lax.shift_right_logical(n_groups + 1, 1), body,
                          tuple([neg] * hs) + tuple(m_first) + tuple([zero_acc] * hs))
    m_used, acc = carry[:hs], carry[2 * hs:]

    v_own = vt_ref[n]
    outs = []
    for hh in range(hs):
        s_own = sown_ref[hh]
        m_fin = jnp.maximum(m_used[hh], row_max(s_own))
        values = jnp.concatenate([v_own[hh * hd:(hh + 1) * hd, :], ones], axis=0)
        a = absorb(acc[hh], m_used[hh], m_fin, s_own, values)
        outs.append(a[:hd, :] * (1.0 / a[hd:hd + 1, :]))
    o_ref[...] = jnp.concatenate(outs, axis=0).T.astype(BF16)


def _moba(qt, k, vt):
    b, nb, mw, blk = qt.shape
    hs = MOBA_HEADS_PER_STEP
    pair = hs * MOBA_HEAD_DIM
    assert nb <= LANES
    hot = np.broadcast_to(np.arange(LANES)[None, None, :] == np.arange(nb)[:, None, None], (nb, blk, LANES))
    hot = jnp.asarray(hot.astype(np.float32), BF16)
    return pl.pallas_call(
        _moba_kernel,
        out_shape=jax.ShapeDtypeStruct((b, nb * blk, mw), BF16),
        grid=(b, mw // pair, nb),
        in_specs=[pl.BlockSpec((None, None, pair, blk), lambda bi, hp, n: (bi, n, hp, 0)),
                  pl.BlockSpec((None, nb, blk, pair), lambda bi, hp, n: (bi, 0, 0, hp)),
                  pl.BlockSpec((None, nb, pair, blk), lambda bi, hp, n: (bi, 0, hp, 0)),
                  _resident(hot.shape)],
        out_specs=pl.BlockSpec((None, blk, pair), lambda bi, hp, n: (bi, n, hp)),
        scratch_shapes=[pltpu.VMEM((nb, pair), F32),
                        pltpu.VMEM((hs, 2 * LANES, blk), BF16),
                        pltpu.VMEM((hs, blk, blk), F32),
                        pltpu.VMEM((2, hs, 1 << MOBA_GROUP_LOG2, blk, blk), F32)],
        compiler_params=pltpu.CompilerParams(dimension_semantics=("arbitrary", "arbitrary", "arbitrary"),
                                             vmem_limit_bytes=V7X_VMEM_LIMIT_BYTES),
        name="moba",
    )(qt, k, vt, hot)


def _ret_kernel(cdec_ref, q_ref, k_ref, v_ref, sg_ref, dint_ref, kdec_ref, qdec_ref, o_ref, state_ref):
    hi = pl.program_id(1)

    @pl.when(pl.program_id(2) == 0)
    def _():
        state_ref[...] = jnp.zeros_like(state_ref)

    c = RET_CHUNK
    dint = dint_ref[...]
    kdec = kdec_ref[...]
    qdec = qdec_ref[...]
    cd = cdec_ref[hi]
    s = state_ref[...]
    for ci in range(q_ref.shape[0] // c):
        rows = slice(ci * c, (ci + 1) * c)
        q = q_ref[rows, :]
        k = k_ref[rows, :]
        v = v_ref[rows, :]
        scores = lax.dot_general(q, k, (((1,), (1,)), ((), ())), preferred_element_type=F32) * dint
        inner = jnp.dot(scores.astype(BF16), v, preferred_element_type=F32)
        qd = (q.astype(F32) * qdec).astype(BF16)
        inner = inner + jnp.dot(qd, s.astype(BF16), preferred_element_type=F32)
        kd = (k.astype(F32) * kdec).astype(BF16)
        kv = lax.dot_general(kd, v, (((0,), (0,)), ((), ())), preferred_element_type=F32)
        s = cd * s + kv
        mu = jnp.mean(inner, axis=-1, keepdims=True)
        xc = inner - mu
        var = jnp.mean(xc * xc, axis=-1, keepdims=True)
        y = xc * lax.rsqrt(var + LN_EPS) * sg_ref[rows, :].astype(F32)
        o_ref[rows, :] = y.astype(BF16)
    state_ref[...] = s


def _retention(rq, rk, rv, sg):
    b, s, _ = rq.shape
    h, dk, dv, c = RET_HEADS, RET_QK_DIM, RET_V_DIM, RET_CHUNK
    tm = min(RET_ROW_TILE, s)
    f = np.float32
    log_g = np.log(f(1.0) - f(2.0) ** (f(-5.0) - np.arange(h, dtype=f))).astype(f)
    idx = np.arange(c, dtype=f)
    diff = idx[:, None] - idx[None, :]
    d_intra = np.where(diff >= 0, np.exp(log_g[:, None, None] * np.maximum(diff, f(0.0))), f(0.0)).astype(f)
    k_decay = np.exp(log_g[:, None] * (f(c - 1.0) - idx)[None, :]).astype(f)
    q_decay = np.exp(log_g[:, None] * (idx + f(1.0))[None, :]).astype(f)
    chunk_decay = np.exp(log_g * f(c)).astype(f)
    kdec = np.ascontiguousarray(np.broadcast_to(k_decay[:, :, None], (h, c, dk)))
    qdec = np.ascontiguousarray(np.broadcast_to(q_decay[:, :, None], (h, c, dk)))
    head_rows = lambda bi, hi, t: (bi, t, hi)
    head_tab = lambda bi, hi, t: (hi, 0, 0)
    return pl.pallas_call(
        _ret_kernel,
        out_shape=jax.ShapeDtypeStruct((b, s, h * dv), BF16),
        grid=(b, h, s // tm),
        in_specs=[pl.BlockSpec(memory_space=pltpu.SMEM),
                  pl.BlockSpec((None, tm, dk), head_rows),
                  pl.BlockSpec((None, tm, dk), head_rows),
                  pl.BlockSpec((None, tm, dv), head_rows),
                  pl.BlockSpec((None, tm, dv), head_rows),
                  pl.BlockSpec((None, c, c), head_tab),
                  pl.BlockSpec((None, c, dk), head_tab),
                  pl.BlockSpec((None, c, dk), head_tab)],
        out_specs=pl.BlockSpec((None, tm, dv), head_rows),
        scratch_shapes=[pltpu.VMEM((dk, dv), F32)],
        compiler_params=pltpu.CompilerParams(dimension_semantics=("arbitrary", "arbitrary", "arbitrary"),
                                             vmem_limit_bytes=V7X_VMEM_LIMIT_BYTES),
        name="ret",
    )(chunk_decay, rq, rk, rv, sg, d_intra, kdec, qdec)


def _merge_kernel(ya_ref, yr_ref, ga_ref, gr_ref, x_ref, mod_ref, pa_ref, pr_ref, wo_ref, lng_ref, lnb_ref,
                  x1_ref, h2_ref, *, d, alpha):
    g1 = mod_ref[:, 2 * d:3 * d]
    sh2 = mod_ref[:, 3 * d:4 * d]
    sc2 = mod_ref[:, 4 * d:5 * d]
    tm = x_ref.shape[0]
    sub = tm // MERGE_SUBTILES
    for i in range(MERGE_SUBTILES):
        rows = slice(i * sub, (i + 1) * sub)
        a = jnp.dot(ya_ref[rows, :], pa_ref[...], preferred_element_type=F32)
        r = jnp.dot(yr_ref[rows, :], pr_ref[...], preferred_element_type=F32)
        merged = ga_ref[rows, :].astype(F32) * a + gr_ref[rows, :].astype(F32) * r
        y = jnp.dot(merged.astype(BF16), wo_ref[...], preferred_element_type=F32)
        x1 = _layer_norm_rows(alpha * x_ref[rows, :] + g1 * y, lng_ref[...], lnb_ref[...])
        x1_ref[rows, :] = x1
        h2_ref[rows, :] = (x1 * (1.0 + sc2) + sh2).astype(BF16)


def _merge(ya, yr, ga, gr, x, mod, w_pa, w_pr, w_o, ln_g, ln_b, alpha):
    b, s, d = x.shape
    tm = ROW_TILE
    row = lambda bi, t: (bi, t, 0)
    pa, pr, wo = w_pa.astype(BF16), w_pr.astype(BF16), w_o.astype(BF16)
    return pl.pallas_call(
        functools.partial(_merge_kernel, d=d, alpha=alpha),
        out_shape=(jax.ShapeDtypeStruct((b, s, d), F32), jax.ShapeDtypeStruct((b, s, d), BF16)),
        grid=(b, s // tm),
        in_specs=[pl.BlockSpec((None, tm, ya.shape[-1]), row),
                  pl.BlockSpec((None, tm, yr.shape[-1]), row),
                  pl.BlockSpec((None, tm, d), row),
                  pl.BlockSpec((None, tm, d), row),
                  pl.BlockSpec((None, tm, d), row),
                  pl.BlockSpec((None, 1, N_MOD * d), lambda bi, t: (bi, 0, 0)),
                  _resident(pa.shape), _resident(pr.shape), _resident(wo.shape),
                  _resident((1, d)), _resident((1, d))],
        out_specs=(pl.BlockSpec((None, tm, d), row), pl.BlockSpec((None, tm, d), row)),
        compiler_params=pltpu.CompilerParams(dimension_semantics=("arbitrary", "arbitrary"),
                                             vmem_limit_bytes=V7X_VMEM_LIMIT_BYTES),
        name="merge",
    )(ya, yr, ga, gr, x, mod, pa, pr, wo, ln_g.reshape(1, d), ln_b.reshape(1, d))


def _shift_rows(g, prev, k):
    rolled = pltpu.roll(g, k, 0)
    sub = lax.broadcasted_iota(jnp.int32, prev.shape, 0)
    top = jnp.where(sub < k, pltpu.roll(prev, k, 0), rolled[:prev.shape[0], :])
    return jnp.concatenate([top, rolled[prev.shape[0]:, :]], axis=0)


def _ffn_kernel(h_ref, x1_ref, mod_ref, wg_ref, wu_ref, wd_ref, cw_ref, cb_ref, lng_ref, lnb_ref,
                o_ref, tail_ref, act_ref, *, d, alpha):
    tm = h_ref.shape[0]

    @pl.when(pl.program_id(1) == 0)
    def _():
        tail_ref[...] = jnp.zeros_like(tail_ref)

    h = h_ref[...]
    fc = FF_CHUNK
    for c in range(wg_ref.shape[1] // fc):
        cs = slice(c * fc, (c + 1) * fc)
        g = jnp.dot(h, wg_ref[:, cs], preferred_element_type=F32)
        u = jnp.dot(h, wu_ref[:, cs], preferred_element_type=F32)
        prev = tail_ref[c]
        tail_ref[c] = g[tm - 8:, :]
        gc = (_shift_rows(g, prev, 2) * cw_ref[0:1, cs] + _shift_rows(g, prev, 1) * cw_ref[1:2, cs]
              + g * cw_ref[2:3, cs] + cb_ref[:, cs])
        act_ref[:, cs] = (gc * (1.0 + lax.erf(gc * (2.0 ** -0.5))) * u).astype(BF16)
    y = jnp.dot(act_ref[...], wd_ref[...], preferred_element_type=F32)
    half_g2 = 0.5 * mod_ref[:, 5 * d:6 * d]
    o_ref[...] = _layer_norm_rows(alpha * x1_ref[...] + half_g2 * y, lng_ref[...], lnb_ref[...])


def _ffn(h2, x1, mod, w_gate, w_up, conv_w, conv_b, w_down, ln_g, ln_b, alpha):
    b, s, d = x1.shape
    dff = w_gate.shape[1]
    tm = ROW_TILE
    row = lambda bi, t: (bi, t, 0)
    wg, wu, wd = w_gate.astype(BF16), w_up.astype(BF16), w_down.astype(BF16)
    return pl.pallas_call(
        functools.partial(_ffn_kernel, d=d, alpha=alpha),
        out_shape=jax.ShapeDtypeStruct((b, s, d), F32),
        grid=(b, s // tm),
        in_specs=[pl.BlockSpec((None, tm, d), row),
                  pl.BlockSpec((None, tm, d), row),
                  pl.BlockSpec((None, 1, N_MOD * d), lambda bi, t: (bi, 0, 0)),
                  _resident(wg.shape), _resident(wu.shape), _resident(wd.shape),
                  _resident(conv_w.shape), _resident((1, dff)),
                  _resident((1, d)), _resident((1, d))],
        out_specs=pl.BlockSpec((None, tm, d), row),
        scratch_shapes=[pltpu.VMEM((dff // FF_CHUNK, 8, FF_CHUNK), F32),
                        pltpu.VMEM((tm, dff), BF16)],
        compiler_params=pltpu.CompilerParams(dimension_semantics=("arbitrary", "arbitrary"),
                                             vmem_limit_bytes=V7X_VMEM_LIMIT_BYTES),
        name="ffn",
    )(h2, x1, mod, wg, wu, wd, conv_w, conv_b.reshape(1, dff), ln_g.reshape(1, d), ln_b.reshape(1, d))


def kernel(x, c, w_ada, b_ada, w_in, w_proj_moba, w_proj_ret, w_out, ln1_g, ln1_b, w_ff_gate, w_ff_up,
           ff_conv_w, ff_conv_b, w_ff_down, ln2_g, ln2_b):
    depth = w_ada.shape[0]
    alpha = (2.0 * depth) ** 0.25
    for l in range(depth):
        mod = _ada(c, w_ada[l], b_ada[l])
        qt, k, vt, rq, rk, rv, sg, ga, gr = _inproj(x, mod, w_in[l])
        ya = _moba(qt, k, vt)
        yr = _retention(rq, rk, rv, sg)
        x1, h2 = _merge(ya, yr, ga, gr, x, mod, w_proj_moba[l], w_proj_ret[l], w_out[l],
                        ln1_g[l], ln1_b[l], alpha)
        x = _ffn(h2, x1, mod, w_ff_gate[l], w_ff_up[l], ff_conv_w[l], ff_conv_b[l], w_ff_down[l],
                 ln2_g[l], ln2_b[l], alpha)
    return x
```

```python
import functools

import jax
import jax.numpy as jnp
import numpy as np
from jax import lax
from jax.experimental import pallas as pl
from jax.experimental.pallas import tpu as pltpu

F32 = jnp.float32
BF16 = jnp.bfloat16

MOBA_HEADS = 8
MOBA_HEAD_DIM = 64
MOBA_BLOCK = 256
MOBA_TOPK = 3
ROPE_THETA = 10000.0
RET_HEADS = 4
RET_QK_DIM = 128
RET_V_DIM = 256
RET_CHUNK = 128
RET_ANGLE_BASE = 10000.0
N_MOD = 6
LN_EPS = 1e-5
NEG_BIG = -1e30
BELOW_NEG_BIG = -3e38
CONV_WIDTH = 3

V7X_VMEM_LIMIT_BYTES = 56 * 1024 * 1024
LANES = 128
BF16_SUBLANES = 16
LOG2_E = 1.4426950408889634
ROW_TILE = 512
FF_CHUNK = 256
RET_ROW_TILE = 4096
MERGE_SUBTILES = 2
MOBA_HEADS_PER_STEP = 4
MOBA_GROUP_LOG2 = 1


def _resident(shape):
    nd = len(shape)
    return pl.BlockSpec(shape, lambda *_: (0,) * nd, pipeline_mode=pl.Buffered(1))


def _layer_norm_rows(z, g, b):
    mu = jnp.mean(z, axis=-1, keepdims=True)
    zc = z - mu
    var = jnp.mean(zc * zc, axis=-1, keepdims=True)
    return zc * lax.rsqrt(var + LN_EPS) * g + b


def _ada_kernel(c_ref, w_ref, b_ref, o_ref):
    c = c_ref[...]
    a = c * jax.nn.sigmoid(c)
    o_ref[...] = jnp.dot(a, w_ref[...], preferred_element_type=F32,
                         precision=lax.Precision.HIGHEST) + b_ref[...]


def _ada(c, w_ada, b_ada):
    b, d = c.shape
    n = w_ada.shape[1]
    rows = 8
    cp = jnp.pad(c, ((0, rows - b), (0, 0)))
    nblk = 1536
    out = pl.pallas_call(
        _ada_kernel,
        out_shape=jax.ShapeDtypeStruct((rows, n), F32),
        grid=(n // nblk,),
        in_specs=[pl.BlockSpec((rows, d), lambda j: (0, 0)),
                  pl.BlockSpec((d, nblk), lambda j: (0, j)),
                  pl.BlockSpec((1, nblk), lambda j: (0, j))],
        out_specs=pl.BlockSpec((rows, nblk), lambda j: (0, j)),
        compiler_params=pltpu.CompilerParams(dimension_semantics=("arbitrary",),
                                             vmem_limit_bytes=V7X_VMEM_LIMIT_BYTES),
        name="ada",
    )(cp, w_ada, b_ada.reshape(1, n))
    return out[:b].reshape(b, 1, n)


def _inproj_kernel(x_ref, mod_ref, wn_ref, ck_ref, sk_ref, cr_ref, sr_ref,
                   qt_ref, k_ref, v_ref, rq_ref, rk_ref, rv_ref, sg_ref, ga_ref, gr_ref, *, d):
    tm = x_ref.shape[0]
    mw = MOBA_HEADS * MOBA_HEAD_DIM
    rw = RET_HEADS * RET_QK_DIM
    vw = RET_HEADS * RET_V_DIM
    sh1 = mod_ref[:, 0:d]
    sc1 = mod_ref[:, d:2 * d]
    h = (x_ref[...] * (1.0 + sc1) + sh1).astype(BF16)

    def nat(col0, width):
        return jnp.dot(h, wn_ref[:, col0:col0 + width], preferred_element_type=F32)

    lane = lax.broadcasted_iota(jnp.int32, (tm, LANES), 1)
    half = MOBA_HEAD_DIM // 2
    ck = ck_ref[...]
    sk = sk_ref[...]
    low = (lane & half) == 0

    def rope(xs):
        partner = jnp.where(low, pltpu.roll(xs, LANES - half, 1), pltpu.roll(xs, half, 1))
        return xs * ck + partner * sk

    q_scale = LOG2_E * MOBA_HEAD_DIM ** -0.5
    qn = nat(0, mw)
    qr = jnp.concatenate([rope(qn[:, g * LANES:(g + 1) * LANES]) for g in range(mw // LANES)], axis=1)
    qt = (qr * q_scale).T.astype(BF16)
    for blk in range(tm // MOBA_BLOCK):
        qt_ref[blk] = qt[:, blk * MOBA_BLOCK:(blk + 1) * MOBA_BLOCK]

    kn = nat(mw, mw)
    for g in range(mw // LANES):
        res = rope(kn[:, g * LANES:(g + 1) * LANES]).astype(BF16)
        for blk in range(tm // MOBA_BLOCK):
            k_ref[blk, :, g * LANES:(g + 1) * LANES] = res[blk * MOBA_BLOCK:(blk + 1) * MOBA_BLOCK, :]

    for c0 in range(0, mw, 256):
        vn = nat(2 * mw + c0, 256).astype(BF16)
        for blk in range(tm // MOBA_BLOCK):
            v_ref[blk, :, c0:c0 + 256] = vn[blk * MOBA_BLOCK:(blk + 1) * MOBA_BLOCK, :]

    cr = cr_ref[...]
    sr = sr_ref[...]
    even = (lane & 1) == 0
    for col0, dst, scale in ((3 * mw, rq_ref, None), (3 * mw + rw, rk_ref, RET_QK_DIM ** -0.5)):
        rn = nat(col0, rw)
        for g in range(rw // LANES):
            xs = rn[:, g * LANES:(g + 1) * LANES]
            partner = jnp.where(even, pltpu.roll(xs, LANES - 1, 1), pltpu.roll(xs, 1, 1))
            res = xs * cr + partner * sr
            if scale is not None:
                res = res * scale
            dst[:, g * LANES:(g + 1) * LANES] = res.astype(BF16)

    col = 3 * mw + 2 * rw
    for c0 in range(0, vw, 512):
        rv_ref[:, c0:c0 + 512] = nat(col + c0, 512).astype(BF16)
    col += vw
    for c0 in range(0, vw, 512):
        z = nat(col + c0, 512)
        sg_ref[:, c0:c0 + 512] = (z * jax.nn.sigmoid(z)).astype(BF16)
    col += vw
    for dst in (ga_ref, gr_ref):
        for c0 in range(0, d, 512):
            dst[:, c0:c0 + 512] = jax.nn.sigmoid(nat(col + c0, 512)).astype(BF16)
        col += d


def _rope_tables(s):
    f = np.float32
    pos = np.arange(s, dtype=f)
    hd = MOBA_HEAD_DIM
    inv_freq = (f(ROPE_THETA) ** (-np.arange(0, hd, 2, dtype=f) / f(hd))).astype(f)
    ang = pos[:, None] * inv_freq[None, :]
    cos, sin = np.cos(ang).astype(f), np.sin(ang).astype(f)
    ck = np.tile(cos, (1, 2 * LANES // hd))
    sk = np.tile(np.concatenate([-sin, sin], axis=1), (1, LANES // hd))
    dk = RET_QK_DIM
    freq = (f(1.0) / (f(RET_ANGLE_BASE) ** np.linspace(0.0, 1.0, dk // 2, dtype=f))).astype(f)
    angr = pos[:, None] * freq[None, :]
    cosr, sinr = np.cos(angr).astype(f), np.sin(angr).astype(f)
    cr = np.repeat(cosr, 2, axis=1)
    sr = np.stack([-sinr, sinr], axis=-1).reshape(s, dk)
    return ck, sk, cr, sr


def _inproj(x, mod, w_in):
    b, s, d = x.shape
    mw = MOBA_HEADS * MOBA_HEAD_DIM
    rw = RET_HEADS * RET_QK_DIM
    vw = RET_HEADS * RET_V_DIM
    tm = ROW_TILE
    nb = s // MOBA_BLOCK
    bpt = tm // MOBA_BLOCK
    w_n = w_in.astype(BF16)
    ck, sk, cr, sr = _rope_tables(s)
    row = lambda bi, t: (bi, t, 0)
    blk4 = lambda bi, t: (bi, t, 0, 0)
    tab = lambda bi, t: (t, 0)
    out_shapes = (
        jax.ShapeDtypeStruct((b, nb, mw, MOBA_BLOCK), BF16),
        jax.ShapeDtypeStruct((b, nb, MOBA_BLOCK, mw), BF16),
        jax.ShapeDtypeStruct((b, nb, MOBA_BLOCK, mw), BF16),
        jax.ShapeDtypeStruct((b, s, rw), BF16),
        jax.ShapeDtypeStruct((b, s, rw), BF16),
        jax.ShapeDtypeStruct((b, s, vw), BF16),
        jax.ShapeDtypeStruct((b, s, vw), BF16),
        jax.ShapeDtypeStruct((b, s, d), BF16),
        jax.ShapeDtypeStruct((b, s, d), BF16),
    )
    out_specs = (
        pl.BlockSpec((None, bpt, mw, MOBA_BLOCK), blk4),
        pl.BlockSpec((None, bpt, MOBA_BLOCK, mw), blk4),
        pl.BlockSpec((None, bpt, MOBA_BLOCK, mw), blk4),
        pl.BlockSpec((None, tm, rw), row),
        pl.BlockSpec((None, tm, rw), row),
        pl.BlockSpec((None, tm, vw), row),
        pl.BlockSpec((None, tm, vw), row),
        pl.BlockSpec((None, tm, d), row),
        pl.BlockSpec((None, tm, d), row),
    )
    in_specs = [
        pl.BlockSpec((None, tm, d), row),
        pl.BlockSpec((None, 1, N_MOD * d), lambda bi, t: (bi, 0, 0)),
        _resident(w_n.shape),
        pl.BlockSpec((tm, LANES), tab), pl.BlockSpec((tm, LANES), tab),
        pl.BlockSpec((tm, LANES), tab), pl.BlockSpec((tm, LANES), tab),
    ]
    return pl.pallas_call(
        functools.partial(_inproj_kernel, d=d),
        out_shape=out_shapes,
        grid=(b, s // tm),
        in_specs=in_specs,
        out_specs=out_specs,
        compiler_params=pltpu.CompilerParams(dimension_semantics=("arbitrary", "arbitrary"),
                                             vmem_limit_bytes=V7X_VMEM_LIMIT_BYTES),
        name="inproj",
    )(x, mod, w_n, ck, sk, cr, sr)


def _moba_kernel(qt_ref, k_ref, v_ref, hot_ref, o_ref, kmean_ref, vt_ref, qa_ref, sown_ref, s_ref):
    n = pl.program_id(2)
    nb = k_ref.shape[0]
    hd = MOBA_HEAD_DIM
    blk = MOBA_BLOCK
    hs = MOBA_HEADS_PER_STEP
    grp = 1 << MOBA_GROUP_LOG2
    sub = 8

    @pl.when(n == 0)
    def _():
        for j in range(nb):
            kmean_ref[j:j + 1, :] = jnp.mean(k_ref[j].astype(F32), axis=0, keepdims=True)
            vt_ref[j] = v_ref[j].astype(F32).T.astype(BF16)

    qt = qt_ref[...]
    feat = lax.broadcasted_iota(jnp.int32, (LANES, blk), 0)
    km = kmean_ref[...]
    km_lane = lax.broadcasted_iota(jnp.int32, km.shape, 1)
    blk_id = lax.broadcasted_iota(jnp.int32, (nb, blk), 0)
    valid = blk_id < n
    blk_f = blk_id.astype(F32)

    pad_rows = jnp.zeros((LANES - nb, blk), BF16)
    for hh in range(hs):
        pair0 = (hh // 2) * LANES
        in_head = (feat >= hh * hd - pair0) & (feat < (hh + 1) * hd - pair0)
        kmh = jnp.where((km_lane >= hh * hd) & (km_lane < (hh + 1) * hd), km, 0.0)
        km_hi = kmh.astype(BF16)
        km_lo = (kmh - km_hi.astype(F32)).astype(BF16)
        gate = (jnp.dot(km_hi, qt, preferred_element_type=F32)
                + jnp.dot(km_lo, qt, preferred_element_type=F32))
        gate = jnp.where(valid, gate, NEG_BIG)
        taken = jnp.zeros((nb, blk), F32)
        for _ in range(MOBA_TOPK):
            best = jnp.max(gate, axis=0, keepdims=True)
            first = jnp.min(jnp.where(gate == best, blk_f, float(nb)), axis=0, keepdims=True)
            pick = blk_f == first
            taken = jnp.where(pick, 1.0, taken)
            gate = jnp.where(pick, BELOW_NEG_BIG, gate)
        qa_ref[hh, 0:LANES] = jnp.where(in_head, qt[pair0:pair0 + LANES], 0.0).astype(BF16)
        qa_ref[hh, LANES:LANES + nb] = jnp.where(valid & (taken > 0.5), 0.0, NEG_BIG).astype(BF16)
        qa_ref[hh, LANES + nb:2 * LANES] = pad_rows

    def col_max(st):
        return jnp.max(st.reshape(st.shape[0] // sub, sub, blk), axis=0)

    key_i = lax.broadcasted_iota(jnp.int32, (blk, blk), 0)
    qry_i = lax.broadcasted_iota(jnp.int32, (blk, blk), 1)
    causal = key_i <= qry_i

    def row_max(st):
        return jnp.max(col_max(st), axis=0, keepdims=True)

    for hh in range(hs):
        pair0 = (hh // 2) * LANES
        st = jnp.dot(k_ref[n, :, pair0:pair0 + LANES], qa_ref[hh, 0:LANES], preferred_element_type=F32)
        sown_ref[hh] = jnp.where(causal, st, NEG_BIG)

    n_groups = lax.shift_right_logical(n + (grp - 1), MOBA_GROUP_LOG2)

    def score_group(gi, slot, m_seen):
        j0 = pl.multiple_of(grp * gi, grp)
        hot = hot_ref[pl.ds(j0, grp)].reshape(grp * blk, LANES)
        keys = [jnp.concatenate([k_ref[pl.ds(j0, grp), :, p0:p0 + LANES].reshape(grp * blk, LANES), hot], axis=1)
                for p0 in range(0, k_ref.shape[2], LANES)]
        out = []
        for hh in range(hs):
            st = jnp.dot(keys[hh // 2], qa_ref[hh], preferred_element_type=F32)
            s_ref[slot, hh] = st.reshape(grp, blk, blk)
            out.append(jnp.maximum(m_seen[hh], row_max(st)))
        return out

    ones = jnp.ones((BF16_SUBLANES, blk), BF16)

    def absorb(acc, m_old, m_new, scores, values):
        p = jnp.exp2(scores - m_new).astype(BF16)
        return acc * jnp.exp2(m_old - m_new) + jnp.dot(values, p, preferred_element_type=F32)

    neg = jnp.full((1, blk), NEG_BIG, F32)
    last_group = nb // grp - 1
    m_first = score_group(0, 0, [neg] * hs)

    def stage(cur, nxt, g_cur, g_next, m_used, m_seen, acc):
        m_next = score_group(jnp.minimum(g_next, last_group), nxt, m_seen)
        j0 = pl.multiple_of(grp * g_cur, grp)
        new_acc = []
        for hh in range(hs):
            values = jnp.concatenate(
                [jnp.concatenate([vt_ref[j0 + g, hh * hd:(hh + 1) * hd, :], ones], axis=0) for g in range(grp)],
                axis=1)
            new_acc.append(absorb(acc[hh], m_used[hh], m_seen[hh],
                                  s_ref[cur, hh].reshape(grp * blk, blk), values))
        return m_seen, m_next, new_acc

    def body(t, carry):
        m_used, m_seen, acc = carry[:hs], carry[hs:2 * hs], carry[2 * hs:]
        m_used, m_seen, acc = stage(0, 1, 2 * t, 2 * t + 1, m_used, m_seen, acc)
        m_used, m_seen, acc = stage(1, 0, 2 * t + 1, 2 * t + 2, m_used, m_seen, acc)
        return tuple(m_used) + tuple(m_seen) + tuple(acc)

    zero_acc = jnp.zeros((hd + BF16_SUBLANES, blk), F32)
    carry = lax.fori_loop(0, lax.shift_right_logical(n_groups + 1, 1), body,
                          tuple([neg] * hs) + tuple(m_first) + tuple([zero_acc] * hs))
    m_used, acc = carry[:hs], carry[2 * hs:]

    v_own = vt_ref[n]
    outs = []
    for hh in range(hs):
        s_own = sown_ref[hh]
        m_fin = jnp.maximum(m_used[hh], row_max(s_own))
        values = jnp.concatenate([v_own[hh * hd:(hh + 1) * hd, :], ones], axis=0)
        a = absorb(acc[hh], m_used[hh], m_fin, s_own, values)
        outs.append(a[:hd, :] * (1.0 / a[hd:hd + 1, :]))
    o_ref[...] = jnp.concatenate(outs, axis=0).T.astype(BF16)


def _moba(qt, k, vt):
    b, nb, mw, blk = qt.shape
    hs = MOBA_HEADS_PER_STEP
    pair = hs * MOBA_HEAD_DIM
    assert nb <= LANES
    hot = np.broadcast_to(np.arange(LANES)[None, None, :] == np.arange(nb)[:, None, None], (nb, blk, LANES))
    hot = jnp.asarray(hot.astype(np.float32), BF16)
    return pl.pallas_call(
        _moba_kernel,
        out_shape=jax.ShapeDtypeStruct((b, nb * blk, mw), BF16),
        grid=(b, mw // pair, nb),
        in_specs=[pl.BlockSpec((None, None, pair, blk), lambda bi, hp, n: (bi, n, hp, 0)),
                  pl.BlockSpec((None, nb, blk, pair), lambda bi, hp, n: (bi, 0, 0, hp)),
                  pl.BlockSpec((None, nb, blk, pair), lambda bi, hp, n: (bi, 0, 0, hp)),
                  _resident(hot.shape)],
        out_specs=pl.BlockSpec((None, blk, pair), lambda bi, hp, n: (bi, n, hp)),
        scratch_shapes=[pltpu.VMEM((nb, pair), F32),
                        pltpu.VMEM((nb, pair, blk), BF16),
                        pltpu.VMEM((hs, 2 * LANES, blk), BF16),
                        pltpu.VMEM((hs, blk, blk), F32),
                        pltpu.VMEM((2, hs, 1 << MOBA_GROUP_LOG2, blk, blk), F32)],
        compiler_params=pltpu.CompilerParams(dimension_semantics=("arbitrary", "arbitrary", "arbitrary"),
                                             vmem_limit_bytes=V7X_VMEM_LIMIT_BYTES),
        name="moba",
    )(qt, k, vt, hot)


def _ret_kernel(cdec_ref, q_ref, k_ref, v_ref, sg_ref, dint_ref, kdec_ref, qdec_ref, o_ref, state_ref):
    hi = pl.program_id(1)

    @pl.when(pl.program_id(2) == 0)
    def _():
        state_ref[...] = jnp.zeros_like(state_ref)

    c = RET_CHUNK
    dint = dint_ref[...]
    kdec = kdec_ref[...]
    qdec = qdec_ref[...]
    cd = cdec_ref[hi]
    s = state_ref[...]
    for ci in range(q_ref.shape[0] // c):
        rows = slice(ci * c, (ci + 1) * c)
        q = q_ref[rows, :]
        k = k_ref[rows, :]
        v = v_ref[rows, :]
        scores = lax.dot_general(q, k, (((1,), (1,)), ((), ())), preferred_element_type=F32) * dint
        inner = jnp.dot(scores.astype(BF16), v, preferred_element_type=F32)
        qd = (q.astype(F32) * qdec).astype(BF16)
        inner = inner + jnp.dot(qd, s.astype(BF16), preferred_element_type=F32)
        kd = (k.astype(F32) * kdec).astype(BF16)
        kv = lax.dot_general(kd, v, (((0,), (0,)), ((), ())), preferred_element_type=F32)
        s = cd * s + kv
        mu = jnp.mean(inner, axis=-1, keepdims=True)
        xc = inner - mu
        var = jnp.mean(xc * xc, axis=-1, keepdims=True)
        y = xc * lax.rsqrt(var + LN_EPS) * sg_ref[rows, :].astype(F32)
        o_ref[rows, :] = y.astype(BF16)
    state_ref[...] = s


def _retention(rq, rk, rv, sg):
    b, s, _ = rq.shape
    h, dk, dv, c = RET_HEADS, RET_QK_DIM, RET_V_DIM, RET_CHUNK
    tm = min(RET_ROW_TILE, s)
    f = np.float32
    log_g = np.log(f(1.0) - f(2.0) ** (f(-5.0) - np.arange(h, dtype=f))).astype(f)
    idx = np.arange(c, dtype=f)
    diff = idx[:, None] - idx[None, :]
    d_intra = np.where(diff >= 0, np.exp(log_g[:, None, None] * np.maximum(diff, f(0.0))), f(0.0)).astype(f)
    k_decay = np.exp(log_g[:, None] * (f(c - 1.0) - idx)[None, :]).astype(f)
    q_decay = np.exp(log_g[:, None] * (idx + f(1.0))[None, :]).astype(f)
    chunk_decay = np.exp(log_g * f(c)).astype(f)
    kdec = np.ascontiguousarray(np.broadcast_to(k_decay[:, :, None], (h, c, dk)))
    qdec = np.ascontiguousarray(np.broadcast_to(q_decay[:, :, None], (h, c, dk)))
    head_rows = lambda bi, hi, t: (bi, t, hi)
    head_tab = lambda bi, hi, t: (hi, 0, 0)
    return pl.pallas_call(
        _ret_kernel,
        out_shape=jax.ShapeDtypeStruct((b, s, h * dv), BF16),
        grid=(b, h, s // tm),
        in_specs=[pl.BlockSpec(memory_space=pltpu.SMEM),
                  pl.BlockSpec((None, tm, dk), head_rows),
                  pl.BlockSpec((None, tm, dk), head_rows),
                  pl.BlockSpec((None, tm, dv), head_rows),
                  pl.BlockSpec((None, tm, dv), head_rows),
                  pl.BlockSpec((None, c, c), head_tab),
                  pl.BlockSpec((None, c, dk), head_tab),
                  pl.BlockSpec((None, c, dk), head_tab)],
        out_specs=pl.BlockSpec((None, tm, dv), head_rows),
        scratch_shapes=[pltpu.VMEM((dk, dv), F32)],
        compiler_params=pltpu.CompilerParams(dimension_semantics=("arbitrary", "arbitrary", "arbitrary"),
                                             vmem_limit_bytes=V7X_VMEM_LIMIT_BYTES),
        name="ret",
    )(chunk_decay, rq, rk, rv, sg, d_intra, kdec, qdec)


def _merge_kernel(ya_ref, yr_ref, ga_ref, gr_ref, x_ref, mod_ref, pa_ref, pr_ref, wo_ref, lng_ref, lnb_ref,
                  x1_ref, h2_ref, *, d, alpha):
    g1 = mod_ref[:, 2 * d:3 * d]
    sh2 = mod_ref[:, 3 * d:4 * d]
    sc2 = mod_ref[:, 4 * d:5 * d]
    tm = x_ref.shape[0]
    sub = tm // MERGE_SUBTILES
    for i in range(MERGE_SUBTILES):
        rows = slice(i * sub, (i + 1) * sub)
        a = jnp.dot(ya_ref[rows, :], pa_ref[...], preferred_element_type=F32)
        r = jnp.dot(yr_ref[rows, :], pr_ref[...], preferred_element_type=F32)
        merged = ga_ref[rows, :].astype(F32) * a + gr_ref[rows, :].astype(F32) * r
        y = jnp.dot(merged.astype(BF16), wo_ref[...], preferred_element_type=F32)
        x1 = _layer_norm_rows(alpha * x_ref[rows, :] + g1 * y, lng_ref[...], lnb_ref[...])
        x1_ref[rows, :] = x1
        h2_ref[rows, :] = (x1 * (1.0 + sc2) + sh2).astype(BF16)


def _merge(ya, yr, ga, gr, x, mod, w_pa, w_pr, w_o, ln_g, ln_b, alpha):
    b, s, d = x.shape
    tm = ROW_TILE
    row = lambda bi, t: (bi, t, 0)
    pa, pr, wo = w_pa.astype(BF16), w_pr.astype(BF16), w_o.astype(BF16)
    return pl.pallas_call(
        functools.partial(_merge_kernel, d=d, alpha=alpha),
        out_shape=(jax.ShapeDtypeStruct((b, s, d), F32), jax.ShapeDtypeStruct((b, s, d), BF16)),
        grid=(b, s // tm),
        in_specs=[pl.BlockSpec((None, tm, ya.shape[-1]), row),
                  pl.BlockSpec((None, tm, yr.shape[-1]), row),
                  pl.BlockSpec((None, tm, d), row),
                  pl.BlockSpec((None, tm, d), row),
                  pl.BlockSpec((None, tm, d), row),
                  pl.BlockSpec((None, 1, N_MOD * d), lambda bi, t: (bi, 0, 0)),
                  _resident(pa.shape), _resident(pr.shape), _resident(wo.shape),
                  _resident((1, d)), _resident((1, d))],
        out_specs=(pl.BlockSpec((None, tm, d), row), pl.BlockSpec((None, tm, d), row)),
        compiler_params=pltpu.CompilerParams(dimension_semantics=("arbitrary", "arbitrary"),
                                             vmem_limit_bytes=V7X_VMEM_LIMIT_BYTES),
        name="merge",
    )(ya, yr, ga, gr, x, mod, pa, pr, wo, ln_g.reshape(1, d), ln_b.reshape(1, d))


def _shift_rows(g, prev, k):
    rolled = pltpu.roll(g, k, 0)
    sub = lax.broadcasted_iota(jnp.int32, prev.shape, 0)
    top = jnp.where(sub < k, pltpu.roll(prev, k, 0), rolled[:prev.shape[0], :])
    return jnp.concatenate([top, rolled[prev.shape[0]:, :]], axis=0)


def _ffn_kernel(h_ref, x1_ref, mod_ref, wg_ref, wu_ref, wd_ref, cw_ref, cb_ref, lng_ref, lnb_ref,
                o_ref, tail_ref, act_ref, *, d, alpha):
    tm = h_ref.shape[0]

    @pl.when(pl.program_id(1) == 0)
    def _():
        tail_ref[...] = jnp.zeros_like(tail_ref)

    h = h_ref[...]
    fc = FF_CHUNK
    for c in range(wg_ref.shape[1] // fc):
        cs = slice(c * fc, (c + 1) * fc)
        g = jnp.dot(h, wg_ref[:, cs], preferred_element_type=F32)
        u = jnp.dot(h, wu_ref[:, cs], preferred_element_type=F32)
        prev = tail_ref[c]
        tail_ref[c] = g[tm - 8:, :]
        gc = (_shift_rows(g, prev, 2) * cw_ref[0:1, cs] + _shift_rows(g, prev, 1) * cw_ref[1:2, cs]
              + g * cw_ref[2:3, cs] + cb_ref[:, cs])
        act_ref[:, cs] = (gc * (1.0 + lax.erf(gc * (2.0 ** -0.5))) * u).astype(BF16)
    y = jnp.dot(act_ref[...], wd_ref[...], preferred_element_type=F32)
    half_g2 = 0.5 * mod_ref[:, 5 * d:6 * d]
    o_ref[...] = _layer_norm_rows(alpha * x1_ref[...] + half_g2 * y, lng_ref[...], lnb_ref[...])


def _ffn(h2, x1, mod, w_gate, w_up, conv_w, conv_b, w_down, ln_g, ln_b, alpha):
    b, s, d = x1.shape
    dff = w_gate.shape[1]
    tm = ROW_TILE
    row = lambda bi, t: (bi, t, 0)
    wg, wu, wd = w_gate.astype(BF16), w_up.astype(BF16), w_down.astype(BF16)
    return pl.pallas_call(
        functools.partial(_ffn_kernel, d=d, alpha=alpha),
        out_shape=jax.ShapeDtypeStruct((b, s, d), F32),
        grid=(b, s // tm),
        in_specs=[pl.BlockSpec((None, tm, d), row),
                  pl.BlockSpec((None, tm, d), row),
                  pl.BlockSpec((None, 1, N_MOD * d), lambda bi, t: (bi, 0, 0)),
                  _resident(wg.shape), _resident(wu.shape), _resident(wd.shape),
                  _resident(conv_w.shape), _resident((1, dff)),
                  _resident((1, d)), _resident((1, d))],
        out_specs=pl.BlockSpec((None, tm, d), row),
        scratch_shapes=[pltpu.VMEM((dff // FF_CHUNK, 8, FF_CHUNK), F32),
                        pltpu.VMEM((tm, dff), BF16)],
        compiler_params=pltpu.CompilerParams(dimension_semantics=("arbitrary", "arbitrary"),
                                             vmem_limit_bytes=V7X_VMEM_LIMIT_BYTES),
        name="ffn",
    )(h2, x1, mod, wg, wu, wd, conv_w, conv_b.reshape(1, dff), ln_g.reshape(1, d), ln_b.reshape(1, d))


def kernel(x, c, w_ada, b_ada, w_in, w_proj_moba, w_proj_ret, w_out, ln1_g, ln1_b, w_ff_gate, w_ff_up,
           ff_conv_w, ff_conv_b, w_ff_down, ln2_g, ln2_b):
    depth = w_ada.shape[0]
    alpha = (2.0 * depth) ** 0.25
    for l in range(depth):
        mod = _ada(c, w_ada[l], b_ada[l])
        qt, k, vt, rq, rk, rv, sg, ga, gr = _inproj(x, mod, w_in[l])
        ya = _moba(qt, k, vt)
        yr = _retention(rq, rk, rv, sg)
        x1, h2 = _merge(ya, yr, ga, gr, x, mod, w_proj_moba[l], w_proj_ret[l], w_out[l],
                        ln1_g[l], ln1_b[l], alpha)
        x = _ffn(h2, x1, mod, w_ff_gate[l], w_ff_up[l], ff_conv_w[l], ff_conv_b[l], w_ff_down[l],
                 ln2_g[l], ln2_b[l], alpha)
    return x
```

```python
import functools

import jax
import jax.numpy as jnp
import numpy as np
from jax import lax
from jax.experimental import pallas as pl
from jax.experimental.pallas import tpu as pltpu

F32 = jnp.float32
BF16 = jnp.bfloat16

MOBA_HEADS = 8
MOBA_HEAD_DIM = 64
MOBA_BLOCK = 256
MOBA_TOPK = 3
ROPE_THETA = 10000.0
RET_HEADS = 4
RET_QK_DIM = 128
RET_V_DIM = 256
RET_CHUNK = 128
RET_ANGLE_BASE = 10000.0
N_MOD = 6
LN_EPS = 1e-5
NEG_BIG = -1e30
BELOW_NEG_BIG = -3e38
CONV_WIDTH = 3

V7X_VMEM_LIMIT_BYTES = 56 * 1024 * 1024
LANES = 128
BF16_SUBLANES = 16
LOG2_E = 1.4426950408889634
ROW_TILE = 512
FF_CHUNK = 256
RET_ROW_TILE = 4096
MERGE_SUBTILES = 2
MOBA_HEADS_PER_STEP = 4
MOBA_GROUP_LOG2 = 1


def _resident(shape):
    nd = len(shape)
    return pl.BlockSpec(shape, lambda *_: (0,) * nd, pipeline_mode=pl.Buffered(1))


def _layer_norm_rows(z, g, b):
    mu = jnp.mean(z, axis=-1, keepdims=True)
    zc = z - mu
    var = jnp.mean(zc * zc, axis=-1, keepdims=True)
    return zc * lax.rsqrt(var + LN_EPS) * g + b


def _ada_kernel(c_ref, w_ref, b_ref, o_ref):
    c = c_ref[...]
    a = c * jax.nn.sigmoid(c)
    o_ref[...] = jnp.dot(a, w_ref[...], preferred_element_type=F32,
                         precision=lax.Precision.HIGHEST) + b_ref[...]


def _ada(c, w_ada, b_ada):
    b, d = c.shape
    n = w_ada.shape[1]
    rows = 8
    cp = jnp.pad(c, ((0, rows - b), (0, 0)))
    nblk = 1536
    out = pl.pallas_call(
        _ada_kernel,
        out_shape=jax.ShapeDtypeStruct((rows, n), F32),
        grid=(n // nblk,),
        in_specs=[pl.BlockSpec((rows, d), lambda j: (0, 0)),
                  pl.BlockSpec((d, nblk), lambda j: (0, j)),
                  pl.BlockSpec((1, nblk), lambda j: (0, j))],
        out_specs=pl.BlockSpec((rows, nblk), lambda j: (0, j)),
        compiler_params=pltpu.CompilerParams(dimension_semantics=("arbitrary",),
                                             vmem_limit_bytes=V7X_VMEM_LIMIT_BYTES),
        name="ada",
    )(cp, w_ada, b_ada.reshape(1, n))
    return out[:b].reshape(b, 1, n)


def _inproj_kernel(x_ref, mod_ref, wn_ref, ck_ref, sk_ref, cr_ref, sr_ref,
                   qt_ref, k_ref, v_ref, rq_ref, rk_ref, rv_ref, sg_ref, ga_ref, gr_ref, *, d):
    tm = x_ref.shape[0]
    mw = MOBA_HEADS * MOBA_HEAD_DIM
    rw = RET_HEADS * RET_QK_DIM
    vw = RET_HEADS * RET_V_DIM
    sh1 = mod_ref[:, 0:d]
    sc1 = mod_ref[:, d:2 * d]
    h = (x_ref[...] * (1.0 + sc1) + sh1).astype(BF16)

    def nat(col0, width):
        return jnp.dot(h, wn_ref[:, col0:col0 + width], preferred_element_type=F32)

    lane = lax.broadcasted_iota(jnp.int32, (tm, LANES), 1)
    half = MOBA_HEAD_DIM // 2
    ck = ck_ref[...]
    sk = sk_ref[...]
    low = (lane & half) == 0

    def rope(xs):
        partner = jnp.where(low, pltpu.roll(xs, LANES - half, 1), pltpu.roll(xs, half, 1))
        return xs * ck + partner * sk

    q_scale = LOG2_E * MOBA_HEAD_DIM ** -0.5
    qn = nat(0, mw)
    qr = jnp.concatenate([rope(qn[:, g * LANES:(g + 1) * LANES]) for g in range(mw // LANES)], axis=1)
    qt = (qr * q_scale).T.astype(BF16)
    for blk in range(tm // MOBA_BLOCK):
        qt_ref[blk] = qt[:, blk * MOBA_BLOCK:(blk + 1) * MOBA_BLOCK]

    kn = nat(mw, mw)
    for g in range(mw // LANES):
        res = rope(kn[:, g * LANES:(g + 1) * LANES]).astype(BF16)
        for blk in range(tm // MOBA_BLOCK):
            k_ref[blk, :, g * LANES:(g + 1) * LANES] = res[blk * MOBA_BLOCK:(blk + 1) * MOBA_BLOCK, :]

    for c0 in range(0, mw, 256):
        vn = nat(2 * mw + c0, 256).astype(BF16)
        for blk in range(tm // MOBA_BLOCK):
            v_ref[blk, :, c0:c0 + 256] = vn[blk * MOBA_BLOCK:(blk + 1) * MOBA_BLOCK, :]

    cr = cr_ref[...]
    sr = sr_ref[...]
    even = (lane & 1) == 0
    for col0, dst, scale in ((3 * mw, rq_ref, None), (3 * mw + rw, rk_ref, RET_QK_DIM ** -0.5)):
        rn = nat(col0, rw)
        for g in range(rw // LANES):
            xs = rn[:, g * LANES:(g + 1) * LANES]
            partner = jnp.where(even, pltpu.roll(xs, LANES - 1, 1), pltpu.roll(xs, 1, 1))
            res = xs * cr + partner * sr
            if scale is not None:
                res = res * scale
            dst[:, g * LANES:(g + 1) * LANES] = res.astype(BF16)

    col = 3 * mw + 2 * rw
    for c0 in range(0, vw, 512):
        rv_ref[:, c0:c0 + 512] = nat(col + c0, 512).astype(BF16)
    col += vw
    for c0 in range(0, vw, 512):
        z = nat(col + c0, 512)
        sg_ref[:, c0:c0 + 512] = (z * jax.nn.sigmoid(z)).astype(BF16)
    col += vw
    for dst in (ga_ref, gr_ref):
        for c0 in range(0, d, 512):
            dst[:, c0:c0 + 512] = jax.nn.sigmoid(nat(col + c0, 512)).astype(BF16)
        col += d


def _rope_tables(s):
    f = np.float32
    pos = np.arange(s, dtype=f)
    hd = MOBA_HEAD_DIM
    inv_freq = (f(ROPE_THETA) ** (-np.arange(0, hd, 2, dtype=f) / f(hd))).astype(f)
    ang = pos[:, None] * inv_freq[None, :]
    cos, sin = np.cos(ang).astype(f), np.sin(ang).astype(f)
    ck = np.tile(cos, (1, 2 * LANES // hd))
    sk = np.tile(np.concatenate([-sin, sin], axis=1), (1, LANES // hd))
    dk = RET_QK_DIM
    freq = (f(1.0) / (f(RET_ANGLE_BASE) ** np.linspace(0.0, 1.0, dk // 2, dtype=f))).astype(f)
    angr = pos[:, None] * freq[None, :]
    cosr, sinr = np.cos(angr).astype(f), np.sin(angr).astype(f)
    cr = np.repeat(cosr, 2, axis=1)
    sr = np.stack([-sinr, sinr], axis=-1).reshape(s, dk)
    return ck, sk, cr, sr


def _inproj(x, mod, w_in):
    b, s, d = x.shape
    mw = MOBA_HEADS * MOBA_HEAD_DIM
    rw = RET_HEADS * RET_QK_DIM
    vw = RET_HEADS * RET_V_DIM
    tm = ROW_TILE
    nb = s // MOBA_BLOCK
    bpt = tm // MOBA_BLOCK
    w_n = w_in.astype(BF16)
    ck, sk, cr, sr = _rope_tables(s)
    row = lambda bi, t: (bi, t, 0)
    blk4 = lambda bi, t: (bi, t, 0, 0)
    tab = lambda bi, t: (t, 0)
    out_shapes = (
        jax.ShapeDtypeStruct((b, nb, mw, MOBA_BLOCK), BF16),
        jax.ShapeDtypeStruct((b, nb, MOBA_BLOCK, mw), BF16),
        jax.ShapeDtypeStruct((b, nb, MOBA_BLOCK, mw), BF16),
        jax.ShapeDtypeStruct((b, s, rw), BF16),
        jax.ShapeDtypeStruct((b, s, rw), BF16),
        jax.ShapeDtypeStruct((b, s, vw), BF16),
        jax.ShapeDtypeStruct((b, s, vw), BF16),
        jax.ShapeDtypeStruct((b, s, d), BF16),
        jax.ShapeDtypeStruct((b, s, d), BF16),
    )
    out_specs = (
        pl.BlockSpec((None, bpt, mw, MOBA_BLOCK), blk4),
        pl.BlockSpec((None, bpt, MOBA_BLOCK, mw), blk4),
        pl.BlockSpec((None, bpt, MOBA_BLOCK, mw), blk4),
        pl.BlockSpec((None, tm, rw), row),
        pl.BlockSpec((None, tm, rw), row),
        pl.BlockSpec((None, tm, vw), row),
        pl.BlockSpec((None, tm, vw), row),
        pl.BlockSpec((None, tm, d), row),
        pl.BlockSpec((None, tm, d), row),
    )
    in_specs = [
        pl.BlockSpec((None, tm, d), row),
        pl.BlockSpec((None, 1, N_MOD * d), lambda bi, t: (bi, 0, 0)),
        _resident(w_n.shape),
        pl.BlockSpec((tm, LANES), tab), pl.BlockSpec((tm, LANES), tab),
        pl.BlockSpec((tm, LANES), tab), pl.BlockSpec((tm, LANES), tab),
    ]
    return pl.pallas_call(
        functools.partial(_inproj_kernel, d=d),
        out_shape=out_shapes,
        grid=(b, s // tm),
        in_specs=in_specs,
        out_specs=out_specs,
        compiler_params=pltpu.CompilerParams(dimension_semantics=("arbitrary", "arbitrary"),
                                             vmem_limit_bytes=V7X_VMEM_LIMIT_BYTES),
        name="inproj",
    )(x, mod, w_n, ck, sk, cr, sr)


def _moba_kernel(qt_ref, k_ref, v_ref, hot_ref, o_ref, kmean_ref, vt_ref, qa_ref, s_ref):
    n = pl.program_id(2)
    nb = k_ref.shape[0]
    hd = MOBA_HEAD_DIM
    blk = MOBA_BLOCK
    hs = MOBA_HEADS_PER_STEP
    grp = 1 << MOBA_GROUP_LOG2
    sub = 8

    @pl.when(n == 0)
    def _():
        for j in range(nb):
            kmean_ref[j:j + 1, :] = jnp.mean(k_ref[j].astype(F32), axis=0, keepdims=True)
            vt_ref[j] = v_ref[j].astype(F32).T.astype(BF16)

    qt = qt_ref[...]
    feat = lax.broadcasted_iota(jnp.int32, (LANES, blk), 0)
    km = kmean_ref[...]
    km_lane = lax.broadcasted_iota(jnp.int32, km.shape, 1)
    blk_id = lax.broadcasted_iota(jnp.int32, (nb, blk), 0)
    valid = blk_id < n
    blk_f = blk_id.astype(F32)

    pad_rows = jnp.zeros((LANES - nb, blk), BF16)
    km_rows = []
    for hh in range(hs):
        pair0 = (hh // 2) * LANES
        in_head = (feat >= hh * hd - pair0) & (feat < (hh + 1) * hd - pair0)
        qa_ref[hh, 0:LANES] = jnp.where(in_head, qt[pair0:pair0 + LANES], 0.0).astype(BF16)
        qa_ref[hh, LANES + nb:2 * LANES] = pad_rows
        kmh = jnp.where((km_lane >= hh * hd) & (km_lane < (hh + 1) * hd), km, 0.0)
        km_hi = kmh.astype(BF16)
        km_rows += [km_hi, (kmh - km_hi.astype(F32)).astype(BF16)]
    gate_all = jnp.dot(jnp.concatenate(km_rows, axis=0), qt, preferred_element_type=F32)
    gates = [gate_all[2 * hh * nb:(2 * hh + 1) * nb] + gate_all[(2 * hh + 1) * nb:(2 * hh + 2) * nb]
             for hh in range(hs)]

    def col_max(st):
        return jnp.max(st.reshape(st.shape[0] // sub, sub, blk), axis=0)

    key_i = lax.broadcasted_iota(jnp.int32, (blk, blk), 0)
    qry_i = lax.broadcasted_iota(jnp.int32, (blk, blk), 1)
    causal = key_i <= qry_i

    def row_max(st):
        return jnp.max(col_max(st), axis=0, keepdims=True)

    ones = jnp.ones((BF16_SUBLANES, blk), BF16)

    m_own, p_own = [], []
    for hh in range(hs):
        pair0 = (hh // 2) * LANES
        st = jnp.dot(k_ref[n, :, pair0:pair0 + LANES], qa_ref[hh, 0:LANES], preferred_element_type=F32)
        st = jnp.where(causal, st, NEG_BIG)
        m_own.append(row_max(st))
        p_own.append(jnp.exp2(st - m_own[hh]).astype(BF16))

    for hh in range(hs):
        gate = jnp.where(valid, gates[hh], NEG_BIG)
        taken = jnp.zeros((nb, blk), F32)
        for _ in range(MOBA_TOPK):
            best = jnp.max(gate, axis=0, keepdims=True)
            first = jnp.min(jnp.where(gate == best, blk_f, float(nb)), axis=0, keepdims=True)
            pick = blk_f == first
            taken = jnp.where(pick, 1.0, taken)
            gate = jnp.where(pick, BELOW_NEG_BIG, gate)
        qa_ref[hh, LANES:LANES + nb] = jnp.where(valid & (taken > 0.5), 0.0, NEG_BIG).astype(BF16)

    n_groups = lax.shift_right_logical(n + (grp - 1), MOBA_GROUP_LOG2)

    def score_group(gi, slot, m_seen):
        j0 = pl.multiple_of(grp * gi, grp)
        hot = hot_ref[pl.ds(j0, grp)].reshape(grp * blk, LANES)
        keys = [jnp.concatenate([k_ref[pl.ds(j0, grp), :, p0:p0 + LANES].reshape(grp * blk, LANES), hot], axis=1)
                for p0 in range(0, k_ref.shape[2], LANES)]
        out = []
        for hh in range(hs):
            st = jnp.dot(keys[hh // 2], qa_ref[hh], preferred_element_type=F32)
            s_ref[slot, hh] = st.reshape(grp, blk, blk)
            out.append(jnp.maximum(m_seen[hh], row_max(st)))
        return out

    def absorb(acc, m_old, m_new, scores, values):
        p = jnp.exp2(scores - m_new).astype(BF16)
        return acc * jnp.exp2(m_old - m_new) + jnp.dot(values, p, preferred_element_type=F32)

    last_group = nb // grp - 1
    m_first = score_group(0, 0, m_own)
    v_own = vt_ref[n]
    acc_own = [jnp.dot(jnp.concatenate([v_own[hh * hd:(hh + 1) * hd, :], ones], axis=0), p_own[hh],
                       preferred_element_type=F32) for hh in range(hs)]

    def stage(cur, nxt, g_cur, g_next, m_used, m_seen, acc):
        m_next = score_group(jnp.minimum(g_next, last_group), nxt, m_seen)
        j0 = pl.multiple_of(grp * g_cur, grp)
        new_acc = []
        for hh in range(hs):
            values = jnp.concatenate(
                [jnp.concatenate([vt_ref[j0 + g, hh * hd:(hh + 1) * hd, :], ones], axis=0) for g in range(grp)],
                axis=1)
            new_acc.append(absorb(acc[hh], m_used[hh], m_seen[hh],
                                  s_ref[cur, hh].reshape(grp * blk, blk), values))
        return m_seen, m_next, new_acc

    def body(t, carry):
        m_used, m_seen, acc = carry[:hs], carry[hs:2 * hs], carry[2 * hs:]
        m_used, m_seen, acc = stage(0, 1, 2 * t, 2 * t + 1, m_used, m_seen, acc)
        m_used, m_seen, acc = stage(1, 0, 2 * t + 1, 2 * t + 2, m_used, m_seen, acc)
        return tuple(m_used) + tuple(m_seen) + tuple(acc)

    carry = lax.fori_loop(0, lax.shift_right_logical(n_groups + 1, 1), body,
                          tuple(m_own) + tuple(m_first) + tuple(acc_own))
    outs = [a[:hd, :] * (1.0 / a[hd:hd + 1, :]) for a in carry[2 * hs:]]
    o_ref[...] = jnp.concatenate(outs, axis=0).T.astype(BF16)


def _moba(qt, k, vt):
    b, nb, mw, blk = qt.shape
    hs = MOBA_HEADS_PER_STEP
    pair = hs * MOBA_HEAD_DIM
    assert nb <= LANES
    hot = np.broadcast_to(np.arange(LANES)[None, None, :] == np.arange(nb)[:, None, None], (nb, blk, LANES))
    hot = jnp.asarray(hot.astype(np.float32), BF16)
    return pl.pallas_call(
        _moba_kernel,
        out_shape=jax.ShapeDtypeStruct((b, nb * blk, mw), BF16),
        grid=(b, mw // pair, nb),
        in_specs=[pl.BlockSpec((None, None, pair, blk), lambda bi, hp, n: (bi, n, hp, 0)),
                  pl.BlockSpec((None, nb, blk, pair), lambda bi, hp, n: (bi, 0, 0, hp)),
                  pl.BlockSpec((None, nb, blk, pair), lambda bi, hp, n: (bi, 0, 0, hp)),
                  _resident(hot.shape)],
        out_specs=pl.BlockSpec((None, blk, pair), lambda bi, hp, n: (bi, n, hp)),
        scratch_shapes=[pltpu.VMEM((nb, pair), F32),
                        pltpu.VMEM((nb, pair, blk), BF16),
                        pltpu.VMEM((hs, 2 * LANES, blk), BF16),
                        pltpu.VMEM((2, hs, 1 << MOBA_GROUP_LOG2, blk, blk), F32)],
        compiler_params=pltpu.CompilerParams(dimension_semantics=("arbitrary", "arbitrary", "arbitrary"),
                                             vmem_limit_bytes=V7X_VMEM_LIMIT_BYTES),
        name="moba",
    )(qt, k, vt, hot)


def _ret_kernel(cdec_ref, q_ref, k_ref, v_ref, sg_ref, dint_ref, kdec_ref, qdec_ref, o_ref, state_ref):
    hi = pl.program_id(1)

    @pl.when(pl.program_id(2) == 0)
    def _():
        state_ref[...] = jnp.zeros_like(state_ref)

    c = RET_CHUNK
    dint = dint_ref[...]
    kdec = kdec_ref[...]
    qdec = qdec_ref[...]
    cd = cdec_ref[hi]
    s = state_ref[...]
    for ci in range(q_ref.shape[0] // c):
        rows = slice(ci * c, (ci + 1) * c)
        q = q_ref[rows, :]
        k = k_ref[rows, :]
        v = v_ref[rows, :]
        scores = lax.dot_general(q, k, (((1,), (1,)), ((), ())), preferred_element_type=F32) * dint
        inner = jnp.dot(scores.astype(BF16), v, preferred_element_type=F32)
        qd = (q.astype(F32) * qdec).astype(BF16)
        inner = inner + jnp.dot(qd, s.astype(BF16), preferred_element_type=F32)
        kd = (k.astype(F32) * kdec).astype(BF16)
        kv = lax.dot_general(kd, v, (((0,), (0,)), ((), ())), preferred_element_type=F32)
        s = cd * s + kv
        mu = jnp.mean(inner, axis=-1, keepdims=True)
        xc = inner - mu
        var = jnp.mean(xc * xc, axis=-1, keepdims=True)
        y = xc * lax.rsqrt(var + LN_EPS) * sg_ref[rows, :].astype(F32)
        o_ref[rows, :] = y.astype(BF16)
    state_ref[...] = s


def _retention(rq, rk, rv, sg):
    b, s, _ = rq.shape
    h, dk, dv, c = RET_HEADS, RET_QK_DIM, RET_V_DIM, RET_CHUNK
    tm = min(RET_ROW_TILE, s)
    f = np.float32
    log_g = np.log(f(1.0) - f(2.0) ** (f(-5.0) - np.arange(h, dtype=f))).astype(f)
    idx = np.arange(c, dtype=f)
    diff = idx[:, None] - idx[None, :]
    d_intra = np.where(diff >= 0, np.exp(log_g[:, None, None] * np.maximum(diff, f(0.0))), f(0.0)).astype(f)
    k_decay = np.exp(log_g[:, None] * (f(c - 1.0) - idx)[None, :]).astype(f)
    q_decay = np.exp(log_g[:, None] * (idx + f(1.0))[None, :]).astype(f)
    chunk_decay = np.exp(log_g * f(c)).astype(f)
    kdec = np.ascontiguousarray(np.broadcast_to(k_decay[:, :, None], (h, c, dk)))
    qdec = np.ascontiguousarray(np.broadcast_to(q_decay[:, :, None], (h, c, dk)))
    head_rows = lambda bi, hi, t: (bi, t, hi)
    head_tab = lambda bi, hi, t: (hi, 0, 0)
    return pl.pallas_call(
        _ret_kernel,
        out_shape=jax.ShapeDtypeStruct((b, s, h * dv), BF16),
        grid=(b, h, s // tm),
        in_specs=[pl.BlockSpec(memory_space=pltpu.SMEM),
                  pl.BlockSpec((None, tm, dk), head_rows),
                  pl.BlockSpec((None, tm, dk), head_rows),
                  pl.BlockSpec((None, tm, dv), head_rows),
                  pl.BlockSpec((None, tm, dv), head_rows),
                  pl.BlockSpec((None, c, c), head_tab),
                  pl.BlockSpec((None, c, dk), head_tab),
                  pl.BlockSpec((None, c, dk), head_tab)],
        out_specs=pl.BlockSpec((None, tm, dv), head_rows),
        scratch_shapes=[pltpu.VMEM((dk, dv), F32)],
        compiler_params=pltpu.CompilerParams(dimension_semantics=("arbitrary", "arbitrary", "arbitrary"),
                                             vmem_limit_bytes=V7X_VMEM_LIMIT_BYTES),
        name="ret",
    )(chunk_decay, rq, rk, rv, sg, d_intra, kdec, qdec)


def _merge_kernel(ya_ref, yr_ref, ga_ref, gr_ref, x_ref, mod_ref, pa_ref, pr_ref, wo_ref, lng_ref, lnb_ref,
                  x1_ref, h2_ref, *, d, alpha):
    g1 = mod_ref[:, 2 * d:3 * d]
    sh2 = mod_ref[:, 3 * d:4 * d]
    sc2 = mod_ref[:, 4 * d:5 * d]
    tm = x_ref.shape[0]
    sub = tm // MERGE_SUBTILES
    for i in range(MERGE_SUBTILES):
        rows = slice(i * sub, (i + 1) * sub)
        a = jnp.dot(ya_ref[rows, :], pa_ref[...], preferred_element_type=F32)
        r = jnp.dot(yr_ref[rows, :], pr_ref[...], preferred_element_type=F32)
        merged = ga_ref[rows, :].astype(F32) * a + gr_ref[rows, :].astype(F32) * r
        y = jnp.dot(merged.astype(BF16), wo_ref[...], preferred_element_type=F32)
        x1 = _layer_norm_rows(alpha * x_ref[rows, :] + g1 * y, lng_ref[...], lnb_ref[...])
        x1_ref[rows, :] = x1
        h2_ref[rows, :] = (x1 * (1.0 + sc2) + sh2).astype(BF16)


def _merge(ya, yr, ga, gr, x, mod, w_pa, w_pr, w_o, ln_g, ln_b, alpha):
    b, s, d = x.shape
    tm = ROW_TILE
    row = lambda bi, t: (bi, t, 0)
    pa, pr, wo = w_pa.astype(BF16), w_pr.astype(BF16), w_o.astype(BF16)
    return pl.pallas_call(
        functools.partial(_merge_kernel, d=d, alpha=alpha),
        out_shape=(jax.ShapeDtypeStruct((b, s, d), F32), jax.ShapeDtypeStruct((b, s, d), BF16)),
        grid=(b, s // tm),
        in_specs=[pl.BlockSpec((None, tm, ya.shape[-1]), row),
                  pl.BlockSpec((None, tm, yr.shape[-1]), row),
                  pl.BlockSpec((None, tm, d), row),
                  pl.BlockSpec((None, tm, d), row),
                  pl.BlockSpec((None, tm, d), row),
                  pl.BlockSpec((None, 1, N_MOD * d), lambda bi, t: (bi, 0, 0)),
                  _resident(pa.shape), _resident(pr.shape), _resident(wo.shape),
                  _resident((1, d)), _resident((1, d))],
        out_specs=(pl.BlockSpec((None, tm, d), row), pl.BlockSpec((None, tm, d), row)),
        compiler_params=pltpu.CompilerParams(dimension_semantics=("arbitrary", "arbitrary"),
                                             vmem_limit_bytes=V7X_VMEM_LIMIT_BYTES),
        name="merge",
    )(ya, yr, ga, gr, x, mod, pa, pr, wo, ln_g.reshape(1, d), ln_b.reshape(1, d))


def _shift_rows(g, prev, k):
    rolled = pltpu.roll(g, k, 0)
    sub = lax.broadcasted_iota(jnp.int32, prev.shape, 0)
    top = jnp.where(sub < k, pltpu.roll(prev, k, 0), rolled[:prev.shape[0], :])
    return jnp.concatenate([top, rolled[prev.shape[0]:, :]], axis=0)


def _ffn_kernel(h_ref, x1_ref, mod_ref, wg_ref, wu_ref, wd_ref, cw_ref, cb_ref, lng_ref, lnb_ref,
                o_ref, tail_ref, act_ref, *, d, alpha):
    tm = h_ref.shape[0]

    @pl.when(pl.program_id(1) == 0)
    def _():
        tail_ref[...] = jnp.zeros_like(tail_ref)

    h = h_ref[...]
    fc = FF_CHUNK
    for c in range(wg_ref.shape[1] // fc):
        cs = slice(c * fc, (c + 1) * fc)
        g = jnp.dot(h, wg_ref[:, cs], preferred_element_type=F32)
        u = jnp.dot(h, wu_ref[:, cs], preferred_element_type=F32)
        prev = tail_ref[c]
        tail_ref[c] = g[tm - 8:, :]
        gc = (_shift_rows(g, prev, 2) * cw_ref[0:1, cs] + _shift_rows(g, prev, 1) * cw_ref[1:2, cs]
              + g * cw_ref[2:3, cs] + cb_ref[:, cs])
        act_ref[:, cs] = (gc * (1.0 + lax.erf(gc * (2.0 ** -0.5))) * u).astype(BF16)
    y = jnp.dot(act_ref[...], wd_ref[...], preferred_element_type=F32)
    half_g2 = 0.5 * mod_ref[:, 5 * d:6 * d]
    o_ref[...] = _layer_norm_rows(alpha * x1_ref[...] + half_g2 * y, lng_ref[...], lnb_ref[...])


def _ffn(h2, x1, mod, w_gate, w_up, conv_w, conv_b, w_down, ln_g, ln_b, alpha):
    b, s, d = x1.shape
    dff = w_gate.shape[1]
    tm = ROW_TILE
    row = lambda bi, t: (bi, t, 0)
    wg, wu, wd = w_gate.astype(BF16), w_up.astype(BF16), w_down.astype(BF16)
    return pl.pallas_call(
        functools.partial(_ffn_kernel, d=d, alpha=alpha),
        out_shape=jax.ShapeDtypeStruct((b, s, d), F32),
        grid=(b, s // tm),
        in_specs=[pl.BlockSpec((None, tm, d), row),
                  pl.BlockSpec((None, tm, d), row),
                  pl.BlockSpec((None, 1, N_MOD * d), lambda bi, t: (bi, 0, 0)),
                  _resident(wg.shape), _resident(wu.shape), _resident(wd.shape),
                  _resident(conv_w.shape), _resident((1, dff)),
                  _resident((1, d)), _resident((1, d))],
        out_specs=pl.BlockSpec((None, tm, d), row),
        scratch_shapes=[pltpu.VMEM((dff // FF_CHUNK, 8, FF_CHUNK), F32),
                        pltpu.VMEM((tm, dff), BF16)],
        compiler_params=pltpu.CompilerParams(dimension_semantics=("arbitrary", "arbitrary"),
                                             vmem_limit_bytes=V7X_VMEM_LIMIT_BYTES),
        name="ffn",
    )(h2, x1, mod, wg, wu, wd, conv_w, conv_b.reshape(1, dff), ln_g.reshape(1, d), ln_b.reshape(1, d))


def kernel(x, c, w_ada, b_ada, w_in, w_proj_moba, w_proj_ret, w_out, ln1_g, ln1_b, w_ff_gate, w_ff_up,
           ff_conv_w, ff_conv_b, w_ff_down, ln2_g, ln2_b):
    depth = w_ada.shape[0]
    alpha = (2.0 * depth) ** 0.25
    for l in range(depth):
        mod = _ada(c, w_ada[l], b_ada[l])
        qt, k, vt, rq, rk, rv, sg, ga, gr = _inproj(x, mod, w_in[l])
        ya = _moba(qt, k, vt)
        yr = _retention(rq, rk, rv, sg)
        x1, h2 = _merge(ya, yr, ga, gr, x, mod, w_proj_moba[l], w_proj_ret[l], w_out[l],
                        ln1_g[l], ln1_b[l], alpha)
        x = _ffn(h2, x1, mod, w_ff_gate[l], w_ff_up[l], ff_conv_w[l], ff_conv_b[l], w_ff_down[l],
                 ln2_g[l], ln2_b[l], alpha)
    return x
```

```python
import functools

import jax
import jax.numpy as jnp
import numpy as np
from jax import lax
from jax.experimental import pallas as pl
from jax.experimental.pallas import tpu as pltpu

F32 = jnp.float32
BF16 = jnp.bfloat16

MOBA_HEADS = 8
MOBA_HEAD_DIM = 64
MOBA_BLOCK = 256
MOBA_TOPK = 3
ROPE_THETA = 10000.0
RET_HEADS = 4
RET_QK_DIM = 128
RET_V_DIM = 256
RET_CHUNK = 128
RET_ANGLE_BASE = 10000.0
N_MOD = 6
LN_EPS = 1e-5
NEG_BIG = -1e30
BELOW_NEG_BIG = -3e38
CONV_WIDTH = 3

V7X_VMEM_LIMIT_BYTES = 56 * 1024 * 1024
LANES = 128
BF16_SUBLANES = 16
LOG2_E = 1.4426950408889634
ROW_TILE = 512
FF_CHUNK = 256
RET_ROW_TILE = 4096
MERGE_SUBTILES = 2
MOBA_HEADS_PER_STEP = 4
MOBA_GROUP_LOG2 = 1


def _resident(shape):
    nd = len(shape)
    return pl.BlockSpec(shape, lambda *_: (0,) * nd, pipeline_mode=pl.Buffered(1))


def _layer_norm_rows(z, g, b):
    mu = jnp.mean(z, axis=-1, keepdims=True)
    zc = z - mu
    var = jnp.mean(zc * zc, axis=-1, keepdims=True)
    return zc * lax.rsqrt(var + LN_EPS) * g + b


def _ada_kernel(c_ref, w_ref, b_ref, o_ref):
    c = c_ref[...]
    a = c * jax.nn.sigmoid(c)
    o_ref[...] = jnp.dot(a, w_ref[...], preferred_element_type=F32,
                         precision=lax.Precision.HIGHEST) + b_ref[...]


def _ada(c, w_ada, b_ada):
    b, d = c.shape
    n = w_ada.shape[1]
    rows = 8
    cp = jnp.pad(c, ((0, rows - b), (0, 0)))
    nblk = 1536
    out = pl.pallas_call(
        _ada_kernel,
        out_shape=jax.ShapeDtypeStruct((rows, n), F32),
        grid=(n // nblk,),
        in_specs=[pl.BlockSpec((rows, d), lambda j: (0, 0)),
                  pl.BlockSpec((d, nblk), lambda j: (0, j)),
                  pl.BlockSpec((1, nblk), lambda j: (0, j))],
        out_specs=pl.BlockSpec((rows, nblk), lambda j: (0, j)),
        compiler_params=pltpu.CompilerParams(dimension_semantics=("arbitrary",),
                                             vmem_limit_bytes=V7X_VMEM_LIMIT_BYTES),
        name="ada",
    )(cp, w_ada, b_ada.reshape(1, n))
    return out[:b].reshape(b, 1, n)


def _inproj_kernel(x_ref, mod_ref, wn_ref, ck_ref, sk_ref, cr_ref, sr_ref,
                   qt_ref, k_ref, v_ref, rq_ref, rk_ref, rv_ref, sg_ref, ga_ref, gr_ref, *, d):
    tm = x_ref.shape[0]
    mw = MOBA_HEADS * MOBA_HEAD_DIM
    rw = RET_HEADS * RET_QK_DIM
    vw = RET_HEADS * RET_V_DIM
    sh1 = mod_ref[:, 0:d]
    sc1 = mod_ref[:, d:2 * d]
    h = (x_ref[...] * (1.0 + sc1) + sh1).astype(BF16)

    def nat(col0, width):
        return jnp.dot(h, wn_ref[:, col0:col0 + width], preferred_element_type=F32)

    lane = lax.broadcasted_iota(jnp.int32, (tm, LANES), 1)
    half = MOBA_HEAD_DIM // 2
    ck = ck_ref[...]
    sk = sk_ref[...]
    low = (lane & half) == 0

    def rope(xs):
        partner = jnp.where(low, pltpu.roll(xs, LANES - half, 1), pltpu.roll(xs, half, 1))
        return xs * ck + partner * sk

    q_scale = LOG2_E * MOBA_HEAD_DIM ** -0.5
    qn = nat(0, mw)
    qr = jnp.concatenate([rope(qn[:, g * LANES:(g + 1) * LANES]) for g in range(mw // LANES)], axis=1)
    qt = (qr * q_scale).T.astype(BF16)
    for blk in range(tm // MOBA_BLOCK):
        qt_ref[blk] = qt[:, blk * MOBA_BLOCK:(blk + 1) * MOBA_BLOCK]

    kn = nat(mw, mw)
    for g in range(mw // LANES):
        res = rope(kn[:, g * LANES:(g + 1) * LANES]).astype(BF16)
        for blk in range(tm // MOBA_BLOCK):
            k_ref[blk, :, g * LANES:(g + 1) * LANES] = res[blk * MOBA_BLOCK:(blk + 1) * MOBA_BLOCK, :]

    for c0 in range(0, mw, 256):
        vn = nat(2 * mw + c0, 256).astype(BF16)
        for blk in range(tm // MOBA_BLOCK):
            v_ref[blk, :, c0:c0 + 256] = vn[blk * MOBA_BLOCK:(blk + 1) * MOBA_BLOCK, :]

    cr = cr_ref[...]
    sr = sr_ref[...]
    even = (lane & 1) == 0
    for col0, dst, scale in ((3 * mw, rq_ref, None), (3 * mw + rw, rk_ref, RET_QK_DIM ** -0.5)):
        rn = nat(col0, rw)
        for g in range(rw // LANES):
            xs = rn[:, g * LANES:(g + 1) * LANES]
            partner = jnp.where(even, pltpu.roll(xs, LANES - 1, 1), pltpu.roll(xs, 1, 1))
            res = xs * cr + partner * sr
            if scale is not None:
                res = res * scale
            dst[:, g * LANES:(g + 1) * LANES] = res.astype(BF16)

    col = 3 * mw + 2 * rw
    for c0 in range(0, vw, 512):
        rv_ref[:, c0:c0 + 512] = nat(col + c0, 512).astype(BF16)
    col += vw
    for c0 in range(0, vw, 512):
        z = nat(col + c0, 512)
        sg_ref[:, c0:c0 + 512] = (z * jax.nn.sigmoid(z)).astype(BF16)
    col += vw
    for dst in (ga_ref, gr_ref):
        for c0 in range(0, d, 512):
            dst[:, c0:c0 + 512] = jax.nn.sigmoid(nat(col + c0, 512)).astype(BF16)
        col += d


def _rope_tables(s):
    f = np.float32
    pos = np.arange(s, dtype=f)
    hd = MOBA_HEAD_DIM
    inv_freq = (f(ROPE_THETA) ** (-np.arange(0, hd, 2, dtype=f) / f(hd))).astype(f)
    ang = pos[:, None] * inv_freq[None, :]
    cos, sin = np.cos(ang).astype(f), np.sin(ang).astype(f)
    ck = np.tile(cos, (1, 2 * LANES // hd))
    sk = np.tile(np.concatenate([-sin, sin], axis=1), (1, LANES // hd))
    dk = RET_QK_DIM
    freq = (f(1.0) / (f(RET_ANGLE_BASE) ** np.linspace(0.0, 1.0, dk // 2, dtype=f))).astype(f)
    angr = pos[:, None] * freq[None, :]
    cosr, sinr = np.cos(angr).astype(f), np.sin(angr).astype(f)
    cr = np.repeat(cosr, 2, axis=1)
    sr = np.stack([-sinr, sinr], axis=-1).reshape(s, dk)
    return ck, sk, cr, sr


def _inproj(x, mod, w_in):
    b, s, d = x.shape
    mw = MOBA_HEADS * MOBA_HEAD_DIM
    rw = RET_HEADS * RET_QK_DIM
    vw = RET_HEADS * RET_V_DIM
    tm = ROW_TILE
    nb = s // MOBA_BLOCK
    bpt = tm // MOBA_BLOCK
    w_n = w_in.astype(BF16)
    ck, sk, cr, sr = _rope_tables(s)
    row = lambda bi, t: (bi, t, 0)
    blk4 = lambda bi, t: (bi, t, 0, 0)
    tab = lambda bi, t: (t, 0)
    out_shapes = (
        jax.ShapeDtypeStruct((b, nb, mw, MOBA_BLOCK), BF16),
        jax.ShapeDtypeStruct((b, nb, MOBA_BLOCK, mw), BF16),
        jax.ShapeDtypeStruct((b, nb, MOBA_BLOCK, mw), BF16),
        jax.ShapeDtypeStruct((b, s, rw), BF16),
        jax.ShapeDtypeStruct((b, s, rw), BF16),
        jax.ShapeDtypeStruct((b, s, vw), BF16),
        jax.ShapeDtypeStruct((b, s, vw), BF16),
        jax.ShapeDtypeStruct((b, s, d), BF16),
        jax.ShapeDtypeStruct((b, s, d), BF16),
    )
    out_specs = (
        pl.BlockSpec((None, bpt, mw, MOBA_BLOCK), blk4),
        pl.BlockSpec((None, bpt, MOBA_BLOCK, mw), blk4),
        pl.BlockSpec((None, bpt, MOBA_BLOCK, mw), blk4),
        pl.BlockSpec((None, tm, rw), row),
        pl.BlockSpec((None, tm, rw), row),
        pl.BlockSpec((None, tm, vw), row),
        pl.BlockSpec((None, tm, vw), row),
        pl.BlockSpec((None, tm, d), row),
        pl.BlockSpec((None, tm, d), row),
    )
    in_specs = [
        pl.BlockSpec((None, tm, d), row),
        pl.BlockSpec((None, 1, N_MOD * d), lambda bi, t: (bi, 0, 0)),
        _resident(w_n.shape),
        pl.BlockSpec((tm, LANES), tab), pl.BlockSpec((tm, LANES), tab),
        pl.BlockSpec((tm, LANES), tab), pl.BlockSpec((tm, LANES), tab),
    ]
    return pl.pallas_call(
        functools.partial(_inproj_kernel, d=d),
        out_shape=out_shapes,
        grid=(b, s // tm),
        in_specs=in_specs,
        out_specs=out_specs,
        compiler_params=pltpu.CompilerParams(dimension_semantics=("arbitrary", "arbitrary"),
                                             vmem_limit_bytes=V7X_VMEM_LIMIT_BYTES),
        name="inproj",
    )(x, mod, w_n, ck, sk, cr, sr)


def _switch(index, branches, *operands):
    def build(lo, hi):
        if hi - lo == 1:
            return branches[lo]
        mid = (lo + hi) // 2
        return lambda *ops: lax.cond(index < mid, build(lo, mid), build(mid, hi), *ops)
    return build(0, len(branches))(*operands)


def _moba_kernel(qt_ref, k_ref, v_ref, hot_ref, o_ref, kmean_ref, vt_ref, qa_ref, s_ref):
    n = pl.program_id(2)
    nb = k_ref.shape[0]
    hd = MOBA_HEAD_DIM
    blk = MOBA_BLOCK
    hs = MOBA_HEADS_PER_STEP
    grp = 1 << MOBA_GROUP_LOG2
    sub = 8

    @pl.when(n == 0)
    def _():
        for j in range(nb):
            kmean_ref[j:j + 1, :] = jnp.mean(k_ref[j].astype(F32), axis=0, keepdims=True)
            vt_ref[j] = v_ref[j].astype(F32).T.astype(BF16)

    qt = qt_ref[...]
    feat = lax.broadcasted_iota(jnp.int32, (LANES, blk), 0)
    km = kmean_ref[...]
    km_lane = lax.broadcasted_iota(jnp.int32, km.shape, 1)
    blk_id = lax.broadcasted_iota(jnp.int32, (nb, blk), 0)
    valid = blk_id < n
    blk_f = blk_id.astype(F32)

    pad_rows = jnp.zeros((LANES - nb, blk), BF16)
    km_rows = []
    for hh in range(hs):
        pair0 = (hh // 2) * LANES
        in_head = (feat >= hh * hd - pair0) & (feat < (hh + 1) * hd - pair0)
        qa_ref[hh, 0:LANES] = jnp.where(in_head, qt[pair0:pair0 + LANES], 0.0).astype(BF16)
        qa_ref[hh, LANES + nb:2 * LANES] = pad_rows
        kmh = jnp.where((km_lane >= hh * hd) & (km_lane < (hh + 1) * hd), km, 0.0)
        km_hi = kmh.astype(BF16)
        km_rows += [km_hi, (kmh - km_hi.astype(F32)).astype(BF16)]
    gate_all = jnp.dot(jnp.concatenate(km_rows, axis=0), qt, preferred_element_type=F32)
    gates = [gate_all[2 * hh * nb:(2 * hh + 1) * nb] + gate_all[(2 * hh + 1) * nb:(2 * hh + 2) * nb]
             for hh in range(hs)]

    def col_max(st):
        return jnp.max(st.reshape(st.shape[0] // sub, sub, blk), axis=0)

    key_i = lax.broadcasted_iota(jnp.int32, (blk, blk), 0)
    qry_i = lax.broadcasted_iota(jnp.int32, (blk, blk), 1)
    causal = key_i <= qry_i

    def row_max(st):
        return jnp.max(col_max(st), axis=0, keepdims=True)

    ones = jnp.ones((BF16_SUBLANES, blk), BF16)

    m_own, p_own = [], []
    for hh in range(hs):
        pair0 = (hh // 2) * LANES
        st = jnp.dot(k_ref[n, :, pair0:pair0 + LANES], qa_ref[hh, 0:LANES], preferred_element_type=F32)
        st = jnp.where(causal, st, NEG_BIG)
        m_own.append(row_max(st))
        p_own.append(jnp.exp2(st - m_own[hh]).astype(BF16))

    for hh in range(hs):
        gate = jnp.where(valid, gates[hh], NEG_BIG)
        taken = jnp.zeros((nb, blk), F32)
        for _ in range(MOBA_TOPK):
            best = jnp.max(gate, axis=0, keepdims=True)
            first = jnp.min(jnp.where(gate == best, blk_f, float(nb)), axis=0, keepdims=True)
            pick = blk_f == first
            taken = jnp.where(pick, 1.0, taken)
            gate = jnp.where(pick, BELOW_NEG_BIG, gate)
        qa_ref[hh, LANES:LANES + nb] = jnp.where(valid & (taken > 0.5), 0.0, NEG_BIG).astype(BF16)

    n_groups = lax.shift_right_logical(n + (grp - 1), MOBA_GROUP_LOG2)

    def score_group(gi, slot, m_seen):
        j0 = grp * gi
        hot = hot_ref[j0:j0 + grp].reshape(grp * blk, LANES)
        keys = [jnp.concatenate([k_ref[j0:j0 + grp, :, p0:p0 + LANES].reshape(grp * blk, LANES), hot], axis=1)
                for p0 in range(0, k_ref.shape[2], LANES)]
        out = []
        for hh in range(hs):
            st = jnp.dot(keys[hh // 2], qa_ref[hh], preferred_element_type=F32)
            s_ref[slot, hh] = st.reshape(grp, blk, blk)
            out.append(jnp.maximum(m_seen[hh], row_max(st)))
        return out

    def absorb(acc, m_old, m_new, scores, values):
        p = jnp.exp2(scores - m_new).astype(BF16)
        return acc * jnp.exp2(m_old - m_new) + jnp.dot(values, p, preferred_element_type=F32)

    def finish(num_groups, m_own, p_own):
        m_used, m_seen = m_own, m_own
        if num_groups:
            m_seen = score_group(0, 0, m_own)
        v_own = vt_ref[n]
        acc = [jnp.dot(jnp.concatenate([v_own[hh * hd:(hh + 1) * hd, :], ones], axis=0), p_own[hh],
                       preferred_element_type=F32) for hh in range(hs)]
        for g in range(num_groups):
            cur = g % 2
            m_next = score_group(g + 1, 1 - cur, m_seen) if g + 1 < num_groups else m_seen
            new_acc = []
            for hh in range(hs):
                values = jnp.concatenate(
                    [jnp.concatenate([vt_ref[grp * g + i, hh * hd:(hh + 1) * hd, :], ones], axis=0)
                     for i in range(grp)], axis=1)
                new_acc.append(absorb(acc[hh], m_used[hh], m_seen[hh],
                                      s_ref[cur, hh].reshape(grp * blk, blk), values))
            m_used, m_seen, acc = m_seen, m_next, new_acc
        outs = [a[:hd, :] * (1.0 / a[hd:hd + 1, :]) for a in acc]
        o_ref[...] = jnp.concatenate(outs, axis=0).T.astype(BF16)

    _switch(n_groups, [functools.partial(finish, g) for g in range(nb // grp + 1)], m_own, p_own)


def _moba(qt, k, vt):
    b, nb, mw, blk = qt.shape
    hs = MOBA_HEADS_PER_STEP
    pair = hs * MOBA_HEAD_DIM
    assert nb <= LANES
    hot = np.broadcast_to(np.arange(LANES)[None, None, :] == np.arange(nb)[:, None, None], (nb, blk, LANES))
    hot = jnp.asarray(hot.astype(np.float32), BF16)
    return pl.pallas_call(
        _moba_kernel,
        out_shape=jax.ShapeDtypeStruct((b, nb * blk, mw), BF16),
        grid=(b, mw // pair, nb),
        in_specs=[pl.BlockSpec((None, None, pair, blk), lambda bi, hp, n: (bi, n, hp, 0)),
                  pl.BlockSpec((None, nb, blk, pair), lambda bi, hp, n: (bi, 0, 0, hp)),
                  pl.BlockSpec((None, nb, blk, pair), lambda bi, hp, n: (bi, 0, 0, hp)),
                  _resident(hot.shape)],
        out_specs=pl.BlockSpec((None, blk, pair), lambda bi, hp, n: (bi, n, hp)),
        scratch_shapes=[pltpu.VMEM((nb, pair), F32),
                        pltpu.VMEM((nb, pair, blk), BF16),
                        pltpu.VMEM((hs, 2 * LANES, blk), BF16),
                        pltpu.VMEM((2, hs, 1 << MOBA_GROUP_LOG2, blk, blk), F32)],
        compiler_params=pltpu.CompilerParams(dimension_semantics=("arbitrary", "arbitrary", "arbitrary"),
                                             vmem_limit_bytes=V7X_VMEM_LIMIT_BYTES),
        name="moba",
    )(qt, k, vt, hot)


def _ret_kernel(cdec_ref, q_ref, k_ref, v_ref, sg_ref, dint_ref, kdec_ref, qdec_ref, o_ref, state_ref):
    hi = pl.program_id(1)

    @pl.when(pl.program_id(2) == 0)
    def _():
        state_ref[...] = jnp.zeros_like(state_ref)

    c = RET_CHUNK
    dint = dint_ref[...]
    kdec = kdec_ref[...]
    qdec = qdec_ref[...]
    cd = cdec_ref[hi]
    s = state_ref[...]
    for ci in range(q_ref.shape[0] // c):
        rows = slice(ci * c, (ci + 1) * c)
        q = q_ref[rows, :]
        k = k_ref[rows, :]
        v = v_ref[rows, :]
        scores = lax.dot_general(q, k, (((1,), (1,)), ((), ())), preferred_element_type=F32) * dint
        inner = jnp.dot(scores.astype(BF16), v, preferred_element_type=F32)
        qd = (q.astype(F32) * qdec).astype(BF16)
        inner = inner + jnp.dot(qd, s.astype(BF16), preferred_element_type=F32)
        kd = (k.astype(F32) * kdec).astype(BF16)
        kv = lax.dot_general(kd, v, (((0,), (0,)), ((), ())), preferred_element_type=F32)
        s = cd * s + kv
        mu = jnp.mean(inner, axis=-1, keepdims=True)
        xc = inner - mu
        var = jnp.mean(xc * xc, axis=-1, keepdims=True)
        y = xc * lax.rsqrt(var + LN_EPS) * sg_ref[rows, :].astype(F32)
        o_ref[rows, :] = y.astype(BF16)
    state_ref[...] = s


def _retention(rq, rk, rv, sg):
    b, s, _ = rq.shape
    h, dk, dv, c = RET_HEADS, RET_QK_DIM, RET_V_DIM, RET_CHUNK
    tm = min(RET_ROW_TILE, s)
    f = np.float32
    log_g = np.log(f(1.0) - f(2.0) ** (f(-5.0) - np.arange(h, dtype=f))).astype(f)
    idx = np.arange(c, dtype=f)
    diff = idx[:, None] - idx[None, :]
    d_intra = np.where(diff >= 0, np.exp(log_g[:, None, None] * np.maximum(diff, f(0.0))), f(0.0)).astype(f)
    k_decay = np.exp(log_g[:, None] * (f(c - 1.0) - idx)[None, :]).astype(f)
    q_decay = np.exp(log_g[:, None] * (idx + f(1.0))[None, :]).astype(f)
    chunk_decay = np.exp(log_g * f(c)).astype(f)
    kdec = np.ascontiguousarray(np.broadcast_to(k_decay[:, :, None], (h, c, dk)))
    qdec = np.ascontiguousarray(np.broadcast_to(q_decay[:, :, None], (h, c, dk)))
    head_rows = lambda bi, hi, t: (bi, t, hi)
    head_tab = lambda bi, hi, t: (hi, 0, 0)
    return pl.pallas_call(
        _ret_kernel,
        out_shape=jax.ShapeDtypeStruct((b, s, h * dv), BF16),
        grid=(b, h, s // tm),
        in_specs=[pl.BlockSpec(memory_space=pltpu.SMEM),
                  pl.BlockSpec((None, tm, dk), head_rows),
                  pl.BlockSpec((None, tm, dk), head_rows),
                  pl.BlockSpec((None, tm, dv), head_rows),
                  pl.BlockSpec((None, tm, dv), head_rows),
                  pl.BlockSpec((None, c, c), head_tab),
                  pl.BlockSpec((None, c, dk), head_tab),
                  pl.BlockSpec((None, c, dk), head_tab)],
        out_specs=pl.BlockSpec((None, tm, dv), head_rows),
        scratch_shapes=[pltpu.VMEM((dk, dv), F32)],
        compiler_params=pltpu.CompilerParams(dimension_semantics=("arbitrary", "arbitrary", "arbitrary"),
                                             vmem_limit_bytes=V7X_VMEM_LIMIT_BYTES),
        name="ret",
    )(chunk_decay, rq, rk, rv, sg, d_intra, kdec, qdec)


def _merge_kernel(ya_ref, yr_ref, ga_ref, gr_ref, x_ref, mod_ref, pa_ref, pr_ref, wo_ref, lng_ref, lnb_ref,
                  x1_ref, h2_ref, *, d, alpha):
    g1 = mod_ref[:, 2 * d:3 * d]
    sh2 = mod_ref[:, 3 * d:4 * d]
    sc2 = mod_ref[:, 4 * d:5 * d]
    tm = x_ref.shape[0]
    sub = tm // MERGE_SUBTILES
    for i in range(MERGE_SUBTILES):
        rows = slice(i * sub, (i + 1) * sub)
        a = jnp.dot(ya_ref[rows, :], pa_ref[...], preferred_element_type=F32)
        r = jnp.dot(yr_ref[rows, :], pr_ref[...], preferred_element_type=F32)
        merged = ga_ref[rows, :].astype(F32) * a + gr_ref[rows, :].astype(F32) * r
        y = jnp.dot(merged.astype(BF16), wo_ref[...], preferred_element_type=F32)
        x1 = _layer_norm_rows(alpha * x_ref[rows, :] + g1 * y, lng_ref[...], lnb_ref[...])
        x1_ref[rows, :] = x1
        h2_ref[rows, :] = (x1 * (1.0 + sc2) + sh2).astype(BF16)


def _merge(ya, yr, ga, gr, x, mod, w_pa, w_pr, w_o, ln_g, ln_b, alpha):
    b, s, d = x.shape
    tm = ROW_TILE
    row = lambda bi, t: (bi, t, 0)
    pa, pr, wo = w_pa.astype(BF16), w_pr.astype(BF16), w_o.astype(BF16)
    return pl.pallas_call(
        functools.partial(_merge_kernel, d=d, alpha=alpha),
        out_shape=(jax.ShapeDtypeStruct((b, s, d), F32), jax.ShapeDtypeStruct((b, s, d), BF16)),
        grid=(b, s // tm),
        in_specs=[pl.BlockSpec((None, tm, ya.shape[-1]), row),
                  pl.BlockSpec((None, tm, yr.shape[-1]), row),
                  pl.BlockSpec((None, tm, d), row),
                  pl.BlockSpec((None, tm, d), row),
                  pl.BlockSpec((None, tm, d), row),
                  pl.BlockSpec((None, 1, N_MOD * d), lambda bi, t: (bi, 0, 0)),
                  _resident(pa.shape), _resident(pr.shape), _resident(wo.shape),
                  _resident((1, d)), _resident((1, d))],
        out_specs=(pl.BlockSpec((None, tm, d), row), pl.BlockSpec((None, tm, d), row)),
        compiler_params=pltpu.CompilerParams(dimension_semantics=("arbitrary", "arbitrary"),
                                             vmem_limit_bytes=V7X_VMEM_LIMIT_BYTES),
        name="merge",
    )(ya, yr, ga, gr, x, mod, pa, pr, wo, ln_g.reshape(1, d), ln_b.reshape(1, d))


def _shift_rows(g, prev, k):
    rolled = pltpu.roll(g, k, 0)
    sub = lax.broadcasted_iota(jnp.int32, prev.shape, 0)
    top = jnp.where(sub < k, pltpu.roll(prev, k, 0), rolled[:prev.shape[0], :])
    return jnp.concatenate([top, rolled[prev.shape[0]:, :]], axis=0)


def _ffn_kernel(h_ref, x1_ref, mod_ref, wg_ref, wu_ref, wd_ref, cw_ref, cb_ref, lng_ref, lnb_ref,
                o_ref, tail_ref, act_ref, *, d, alpha):
    tm = h_ref.shape[0]

    @pl.when(pl.program_id(1) == 0)
    def _():
        tail_ref[...] = jnp.zeros_like(tail_ref)

    h = h_ref[...]
    fc = FF_CHUNK
    for c in range(wg_ref.shape[1] // fc):
        cs = slice(c * fc, (c + 1) * fc)
        g = jnp.dot(h, wg_ref[:, cs], preferred_element_type=F32)
        u = jnp.dot(h, wu_ref[:, cs], preferred_element_type=F32)
        prev = tail_ref[c]
        tail_ref[c] = g[tm - 8:, :]
        gc = (_shift_rows(g, prev, 2) * cw_ref[0:1, cs] + _shift_rows(g, prev, 1) * cw_ref[1:2, cs]
              + g * cw_ref[2:3, cs] + cb_ref[:, cs])
        act_ref[:, cs] = (gc * (1.0 + lax.erf(gc * (2.0 ** -0.5))) * u).astype(BF16)
    y = jnp.dot(act_ref[...], wd_ref[...], preferred_element_type=F32)
    half_g2 = 0.5 * mod_ref[:, 5 * d:6 * d]
    o_ref[...] = _layer_norm_rows(alpha * x1_ref[...] + half_g2 * y, lng_ref[...], lnb_ref[...])


def _ffn(h2, x1, mod, w_gate, w_up, conv_w, conv_b, w_down, ln_g, ln_b, alpha):
    b, s, d = x1.shape
    dff = w_gate.shape[1]
    tm = ROW_TILE
    row = lambda bi, t: (bi, t, 0)
    wg, wu, wd = w_gate.astype(BF16), w_up.astype(BF16), w_down.astype(BF16)
    return pl.pallas_call(
        functools.partial(_ffn_kernel, d=d, alpha=alpha),
        out_shape=jax.ShapeDtypeStruct((b, s, d), F32),
        grid=(b, s // tm),
        in_specs=[pl.BlockSpec((None, tm, d), row),
                  pl.BlockSpec((None, tm, d), row),
                  pl.BlockSpec((None, 1, N_MOD * d), lambda bi, t: (bi, 0, 0)),
                  _resident(wg.shape), _resident(wu.shape), _resident(wd.shape),
                  _resident(conv_w.shape), _resident((1, dff)),
                  _resident((1, d)), _resident((1, d))],
        out_specs=pl.BlockSpec((None, tm, d), row),
        scratch_shapes=[pltpu.VMEM((dff // FF_CHUNK, 8, FF_CHUNK), F32),
                        pltpu.VMEM((tm, dff), BF16)],
        compiler_params=pltpu.CompilerParams(dimension_semantics=("arbitrary", "arbitrary"),
                                             vmem_limit_bytes=V7X_VMEM_LIMIT_BYTES),
        name="ffn",
    )(h2, x1, mod, wg, wu, wd, conv_w, conv_b.reshape(1, dff), ln_g.reshape(1, d), ln_b.reshape(1, d))


def kernel(x, c, w_ada, b_ada, w_in, w_proj_moba, w_proj_ret, w_out, ln1_g, ln1_b, w_ff_gate, w_ff_up,
           ff_conv_w, ff_conv_b, w_ff_down, ln2_g, ln2_b):
    depth = w_ada.shape[0]
    alpha = (2.0 * depth) ** 0.25
    for l in range(depth):
        mod = _ada(c, w_ada[l], b_ada[l])
        qt, k, vt, rq, rk, rv, sg, ga, gr = _inproj(x, mod, w_in[l])
        ya = _moba(qt, k, vt)
        yr = _retention(rq, rk, rv, sg)
        x1, h2 = _merge(ya, yr, ga, gr, x, mod, w_proj_moba[l], w_proj_ret[l], w_out[l],
                        ln1_g[l], ln1_b[l], alpha)
        x = _ffn(h2, x1, mod, w_ff_gate[l], w_ff_up[l], ff_conv_w[l], ff_conv_b[l], w_ff_down[l],
                 ln2_g[l], ln2_b[l], alpha)
    return x
```

```python
import functools

import jax
import jax.numpy as jnp
import numpy as np
from jax import lax
from jax.experimental import pallas as pl
from jax.experimental.pallas import tpu as pltpu

F32 = jnp.float32
BF16 = jnp.bfloat16

MOBA_HEADS = 8
MOBA_HEAD_DIM = 64
MOBA_BLOCK = 256
MOBA_TOPK = 3
ROPE_THETA = 10000.0
RET_HEADS = 4
RET_QK_DIM = 128
RET_V_DIM = 256
RET_CHUNK = 128
RET_ANGLE_BASE = 10000.0
N_MOD = 6
LN_EPS = 1e-5
NEG_BIG = -1e30
BELOW_NEG_BIG = -3e38
CONV_WIDTH = 3

V7X_VMEM_LIMIT_BYTES = 56 * 1024 * 1024
LANES = 128
BF16_SUBLANES = 16
LOG2_E = 1.4426950408889634
ROW_TILE = 512
FF_CHUNK = 256
RET_ROW_TILE = 4096
MERGE_SUBTILES = 2
FFN_EARLY_CHUNKS = 3
MOBA_HEADS_PER_STEP = 4
MOBA_GROUP_LOG2 = 1


def _resident(shape):
    nd = len(shape)
    return pl.BlockSpec(shape, lambda *_: (0,) * nd, pipeline_mode=pl.Buffered(1))


def _layer_norm_rows(z, g, b):
    mu = jnp.mean(z, axis=-1, keepdims=True)
    zc = z - mu
    var = jnp.mean(zc * zc, axis=-1, keepdims=True)
    return zc * lax.rsqrt(var + LN_EPS) * g + b


def _ada_kernel(c_ref, w_ref, b_ref, o_ref):
    c = c_ref[...]
    a = c * jax.nn.sigmoid(c)
    o_ref[...] = jnp.dot(a, w_ref[...], preferred_element_type=F32,
                         precision=lax.Precision.HIGHEST) + b_ref[...]


def _ada(c, w_ada, b_ada):
    b, d = c.shape
    n = w_ada.shape[1]
    rows = 8
    cp = jnp.pad(c, ((0, rows - b), (0, 0)))
    nblk = 1536
    out = pl.pallas_call(
        _ada_kernel,
        out_shape=jax.ShapeDtypeStruct((rows, n), F32),
        grid=(n // nblk,),
        in_specs=[pl.BlockSpec((rows, d), lambda j: (0, 0)),
                  pl.BlockSpec((d, nblk), lambda j: (0, j)),
                  pl.BlockSpec((1, nblk), lambda j: (0, j))],
        out_specs=pl.BlockSpec((rows, nblk), lambda j: (0, j)),
        compiler_params=pltpu.CompilerParams(dimension_semantics=("arbitrary",),
                                             vmem_limit_bytes=V7X_VMEM_LIMIT_BYTES),
        name="ada",
    )(cp, w_ada, b_ada.reshape(1, n))
    return out[:b].reshape(b, 1, n)


def _inproj_kernel(x_ref, mod_ref, wn_ref, ck_ref, sk_ref, cr_ref, sr_ref,
                   qt_ref, k_ref, v_ref, rq_ref, rk_ref, rv_ref, sg_ref, ga_ref, gr_ref, *, d):
    tm = x_ref.shape[0]
    mw = MOBA_HEADS * MOBA_HEAD_DIM
    rw = RET_HEADS * RET_QK_DIM
    vw = RET_HEADS * RET_V_DIM
    sh1 = mod_ref[:, 0:d]
    sc1 = mod_ref[:, d:2 * d]
    h = (x_ref[...] * (1.0 + sc1) + sh1).astype(BF16)

    def nat(col0, width):
        return jnp.dot(h, wn_ref[:, col0:col0 + width], preferred_element_type=F32)

    lane = lax.broadcasted_iota(jnp.int32, (tm, LANES), 1)
    half = MOBA_HEAD_DIM // 2
    ck = ck_ref[...]
    sk = sk_ref[...]
    low = (lane & half) == 0

    def rope(xs):
        partner = jnp.where(low, pltpu.roll(xs, LANES - half, 1), pltpu.roll(xs, half, 1))
        return xs * ck + partner * sk

    q_scale = LOG2_E * MOBA_HEAD_DIM ** -0.5
    qn = nat(0, mw)
    qr = jnp.concatenate([rope(qn[:, g * LANES:(g + 1) * LANES]) for g in range(mw // LANES)], axis=1)
    qt = (qr * q_scale).T.astype(BF16)
    for blk in range(tm // MOBA_BLOCK):
        qt_ref[blk] = qt[:, blk * MOBA_BLOCK:(blk + 1) * MOBA_BLOCK]

    kn = nat(mw, mw)
    for g in range(mw // LANES):
        res = rope(kn[:, g * LANES:(g + 1) * LANES]).astype(BF16)
        for blk in range(tm // MOBA_BLOCK):
            k_ref[blk, :, g * LANES:(g + 1) * LANES] = res[blk * MOBA_BLOCK:(blk + 1) * MOBA_BLOCK, :]

    for c0 in range(0, mw, 256):
        vn = nat(2 * mw + c0, 256).astype(BF16)
        for blk in range(tm // MOBA_BLOCK):
            v_ref[blk, :, c0:c0 + 256] = vn[blk * MOBA_BLOCK:(blk + 1) * MOBA_BLOCK, :]

    cr = cr_ref[...]
    sr = sr_ref[...]
    even = (lane & 1) == 0
    for col0, dst, scale in ((3 * mw, rq_ref, None), (3 * mw + rw, rk_ref, RET_QK_DIM ** -0.5)):
        rn = nat(col0, rw)
        for g in range(rw // LANES):
            xs = rn[:, g * LANES:(g + 1) * LANES]
            partner = jnp.where(even, pltpu.roll(xs, LANES - 1, 1), pltpu.roll(xs, 1, 1))
            res = xs * cr + partner * sr
            if scale is not None:
                res = res * scale
            dst[:, g * LANES:(g + 1) * LANES] = res.astype(BF16)

    col = 3 * mw + 2 * rw
    for c0 in range(0, vw, 512):
        rv_ref[:, c0:c0 + 512] = nat(col + c0, 512).astype(BF16)
    col += vw
    for c0 in range(0, vw, 512):
        z = nat(col + c0, 512)
        sg_ref[:, c0:c0 + 512] = (z * jax.nn.sigmoid(z)).astype(BF16)
    col += vw
    for dst in (ga_ref, gr_ref):
        for c0 in range(0, d, 512):
            dst[:, c0:c0 + 512] = jax.nn.sigmoid(nat(col + c0, 512)).astype(BF16)
        col += d


def _rope_tables(s):
    f = np.float32
    pos = np.arange(s, dtype=f)
    hd = MOBA_HEAD_DIM
    inv_freq = (f(ROPE_THETA) ** (-np.arange(0, hd, 2, dtype=f) / f(hd))).astype(f)
    ang = pos[:, None] * inv_freq[None, :]
    cos, sin = np.cos(ang).astype(f), np.sin(ang).astype(f)
    ck = np.tile(cos, (1, 2 * LANES // hd))
    sk = np.tile(np.concatenate([-sin, sin], axis=1), (1, LANES // hd))
    dk = RET_QK_DIM
    freq = (f(1.0) / (f(RET_ANGLE_BASE) ** np.linspace(0.0, 1.0, dk // 2, dtype=f))).astype(f)
    angr = pos[:, None] * freq[None, :]
    cosr, sinr = np.cos(angr).astype(f), np.sin(angr).astype(f)
    cr = np.repeat(cosr, 2, axis=1)
    sr = np.stack([-sinr, sinr], axis=-1).reshape(s, dk)
    return ck, sk, cr, sr


def _inproj(x, mod, w_in):
    b, s, d = x.shape
    mw = MOBA_HEADS * MOBA_HEAD_DIM
    rw = RET_HEADS * RET_QK_DIM
    vw = RET_HEADS * RET_V_DIM
    tm = ROW_TILE
    nb = s // MOBA_BLOCK
    bpt = tm // MOBA_BLOCK
    w_n = w_in.astype(BF16)
    ck, sk, cr, sr = _rope_tables(s)
    row = lambda bi, t: (bi, t, 0)
    blk4 = lambda bi, t: (bi, t, 0, 0)
    tab = lambda bi, t: (t, 0)
    out_shapes = (
        jax.ShapeDtypeStruct((b, nb, mw, MOBA_BLOCK), BF16),
        jax.ShapeDtypeStruct((b, nb, MOBA_BLOCK, mw), BF16),
        jax.ShapeDtypeStruct((b, nb, MOBA_BLOCK, mw), BF16),
        jax.ShapeDtypeStruct((b, s, rw), BF16),
        jax.ShapeDtypeStruct((b, s, rw), BF16),
        jax.ShapeDtypeStruct((b, s, vw), BF16),
        jax.ShapeDtypeStruct((b, s, vw), BF16),
        jax.ShapeDtypeStruct((b, s, d), BF16),
        jax.ShapeDtypeStruct((b, s, d), BF16),
    )
    out_specs = (
        pl.BlockSpec((None, bpt, mw, MOBA_BLOCK), blk4),
        pl.BlockSpec((None, bpt, MOBA_BLOCK, mw), blk4),
        pl.BlockSpec((None, bpt, MOBA_BLOCK, mw), blk4),
        pl.BlockSpec((None, tm, rw), row),
        pl.BlockSpec((None, tm, rw), row),
        pl.BlockSpec((None, tm, vw), row),
        pl.BlockSpec((None, tm, vw), row),
        pl.BlockSpec((None, tm, d), row),
        pl.BlockSpec((None, tm, d), row),
    )
    in_specs = [
        pl.BlockSpec((None, tm, d), row),
        pl.BlockSpec((None, 1, N_MOD * d), lambda bi, t: (bi, 0, 0)),
        _resident(w_n.shape),
        pl.BlockSpec((tm, LANES), tab), pl.BlockSpec((tm, LANES), tab),
        pl.BlockSpec((tm, LANES), tab), pl.BlockSpec((tm, LANES), tab),
    ]
    return pl.pallas_call(
        functools.partial(_inproj_kernel, d=d),
        out_shape=out_shapes,
        grid=(b, s // tm),
        in_specs=in_specs,
        out_specs=out_specs,
        compiler_params=pltpu.CompilerParams(dimension_semantics=("arbitrary", "arbitrary"),
                                             vmem_limit_bytes=V7X_VMEM_LIMIT_BYTES),
        name="inproj",
    )(x, mod, w_n, ck, sk, cr, sr)


def _switch(index, branches, *operands):
    def build(lo, hi):
        if hi - lo == 1:
            return branches[lo]
        mid = (lo + hi) // 2
        return lambda *ops: lax.cond(index < mid, build(lo, mid), build(mid, hi), *ops)
    return build(0, len(branches))(*operands)


def _moba_kernel(qt_ref, k_ref, v_ref, hot_ref, o_ref, kmean_ref, vt_ref, qa_ref, s_ref):
    n = pl.program_id(2)
    nb = k_ref.shape[0]
    hd = MOBA_HEAD_DIM
    blk = MOBA_BLOCK
    hs = MOBA_HEADS_PER_STEP
    grp = 1 << MOBA_GROUP_LOG2
    sub = 8

    @pl.when(n == 0)
    def _():
        for j in range(nb):
            kmean_ref[j:j + 1, :] = jnp.mean(k_ref[j].astype(F32), axis=0, keepdims=True)
            vt_ref[j] = v_ref[j].astype(F32).T.astype(BF16)

    qt = qt_ref[...]
    feat = lax.broadcasted_iota(jnp.int32, (LANES, blk), 0)
    km = kmean_ref[...]
    km_lane = lax.broadcasted_iota(jnp.int32, km.shape, 1)
    blk_id = lax.broadcasted_iota(jnp.int32, (nb, blk), 0)
    valid = blk_id < n
    blk_f = blk_id.astype(F32)

    pad_rows = jnp.zeros((LANES - nb, blk), BF16)
    km_rows = []
    for hh in range(hs):
        pair0 = (hh // 2) * LANES
        in_head = (feat >= hh * hd - pair0) & (feat < (hh + 1) * hd - pair0)
        qa_ref[hh, 0:LANES] = jnp.where(in_head, qt[pair0:pair0 + LANES], 0.0).astype(BF16)
        qa_ref[hh, LANES + nb:2 * LANES] = pad_rows
        kmh = jnp.where((km_lane >= hh * hd) & (km_lane < (hh + 1) * hd), km, 0.0)
        km_hi = kmh.astype(BF16)
        km_rows += [km_hi, (kmh - km_hi.astype(F32)).astype(BF16)]
    gate_all = jnp.dot(jnp.concatenate(km_rows, axis=0), qt, preferred_element_type=F32)
    gates = [gate_all[2 * hh * nb:(2 * hh + 1) * nb] + gate_all[(2 * hh + 1) * nb:(2 * hh + 2) * nb]
             for hh in range(hs)]

    def col_max(st):
        return jnp.max(st.reshape(st.shape[0] // sub, sub, blk), axis=0)

    key_i = lax.broadcasted_iota(jnp.int32, (blk, blk), 0)
    qry_i = lax.broadcasted_iota(jnp.int32, (blk, blk), 1)
    causal = key_i <= qry_i

    def row_max(st):
        return jnp.max(col_max(st), axis=0, keepdims=True)

    ones = jnp.ones((BF16_SUBLANES, blk), BF16)

    m_own, p_own = [], []
    for hh in range(hs):
        pair0 = (hh // 2) * LANES
        st = jnp.dot(k_ref[n, :, pair0:pair0 + LANES], qa_ref[hh, 0:LANES], preferred_element_type=F32)
        st = jnp.where(causal, st, NEG_BIG)
        m_own.append(row_max(st))
        p_own.append(jnp.exp2(st - m_own[hh]).astype(BF16))

    for hh in range(hs):
        gate = jnp.where(valid, gates[hh], NEG_BIG)
        taken = jnp.zeros((nb, blk), F32)
        for _ in range(MOBA_TOPK):
            best = jnp.max(gate, axis=0, keepdims=True)
            first = jnp.min(jnp.where(gate == best, blk_f, float(nb)), axis=0, keepdims=True)
            pick = blk_f == first
            taken = jnp.where(pick, 1.0, taken)
            gate = jnp.where(pick, BELOW_NEG_BIG, gate)
        qa_ref[hh, LANES:LANES + nb] = jnp.where(valid & (taken > 0.5), 0.0, NEG_BIG).astype(BF16)

    n_groups = lax.shift_right_logical(n + (grp - 1), MOBA_GROUP_LOG2)

    def score_group(gi, slot, m_seen):
        j0 = grp * gi
        hot = hot_ref[j0:j0 + grp].reshape(grp * blk, LANES)
        keys = [jnp.concatenate([k_ref[j0:j0 + grp, :, p0:p0 + LANES].reshape(grp * blk, LANES), hot], axis=1)
                for p0 in range(0, k_ref.shape[2], LANES)]
        out = []
        for hh in range(hs):
            st = jnp.dot(keys[hh // 2], qa_ref[hh], preferred_element_type=F32)
            s_ref[slot, hh] = st.reshape(grp, blk, blk)
            out.append(jnp.maximum(m_seen[hh], row_max(st)))
        return out

    def absorb(acc, m_old, m_new, scores, values):
        p = jnp.exp2(scores - m_new).astype(BF16)
        return acc * jnp.exp2(m_old - m_new) + jnp.dot(values, p, preferred_element_type=F32)

    def finish(num_groups, m_own, p_own):
        m_used, m_seen = m_own, m_own
        if num_groups:
            m_seen = score_group(0, 0, m_own)
        v_own = vt_ref[n]
        acc = [jnp.dot(jnp.concatenate([v_own[hh * hd:(hh + 1) * hd, :], ones], axis=0), p_own[hh],
                       preferred_element_type=F32) for hh in range(hs)]
        for g in range(num_groups):
            cur = g % 2
            m_next = score_group(g + 1, 1 - cur, m_seen) if g + 1 < num_groups else m_seen
            new_acc = []
            for hh in range(hs):
                values = jnp.concatenate(
                    [jnp.concatenate([vt_ref[grp * g + i, hh * hd:(hh + 1) * hd, :], ones], axis=0)
                     for i in range(grp)], axis=1)
                new_acc.append(absorb(acc[hh], m_used[hh], m_seen[hh],
                                      s_ref[cur, hh].reshape(grp * blk, blk), values))
            m_used, m_seen, acc = m_seen, m_next, new_acc
        outs = [a[:hd, :] * (1.0 / a[hd:hd + 1, :]) for a in acc]
        o_ref[...] = jnp.concatenate(outs, axis=0).T.astype(BF16)

    _switch(n_groups, [functools.partial(finish, g) for g in range(nb // grp + 1)], m_own, p_own)


def _moba(qt, k, vt):
    b, nb, mw, blk = qt.shape
    hs = MOBA_HEADS_PER_STEP
    pair = hs * MOBA_HEAD_DIM
    assert nb <= LANES
    hot = np.broadcast_to(np.arange(LANES)[None, None, :] == np.arange(nb)[:, None, None], (nb, blk, LANES))
    hot = jnp.asarray(hot.astype(np.float32), BF16)
    return pl.pallas_call(
        _moba_kernel,
        out_shape=jax.ShapeDtypeStruct((b, nb * blk, mw), BF16),
        grid=(b, mw // pair, nb),
        in_specs=[pl.BlockSpec((None, None, pair, blk), lambda bi, hp, n: (bi, n, hp, 0)),
                  pl.BlockSpec((None, nb, blk, pair), lambda bi, hp, n: (bi, 0, 0, hp)),
                  pl.BlockSpec((None, nb, blk, pair), lambda bi, hp, n: (bi, 0, 0, hp)),
                  _resident(hot.shape)],
        out_specs=pl.BlockSpec((None, blk, pair), lambda bi, hp, n: (bi, n, hp)),
        scratch_shapes=[pltpu.VMEM((nb, pair), F32),
                        pltpu.VMEM((nb, pair, blk), BF16),
                        pltpu.VMEM((hs, 2 * LANES, blk), BF16),
                        pltpu.VMEM((2, hs, 1 << MOBA_GROUP_LOG2, blk, blk), F32)],
        compiler_params=pltpu.CompilerParams(dimension_semantics=("arbitrary", "arbitrary", "arbitrary"),
                                             vmem_limit_bytes=V7X_VMEM_LIMIT_BYTES),
        name="moba",
    )(qt, k, vt, hot)


def _ret_kernel(cdec_ref, q_ref, k_ref, v_ref, sg_ref, dint_ref, kdec_ref, qdec_ref, o_ref, state_ref):
    hi = pl.program_id(1)

    @pl.when(pl.program_id(2) == 0)
    def _():
        state_ref[...] = jnp.zeros_like(state_ref)

    c = RET_CHUNK
    dint = dint_ref[...]
    kdec = kdec_ref[...]
    qdec = qdec_ref[...]
    cd = cdec_ref[hi]
    s = state_ref[...]
    for ci in range(q_ref.shape[0] // c):
        rows = slice(ci * c, (ci + 1) * c)
        q = q_ref[rows, :]
        k = k_ref[rows, :]
        v = v_ref[rows, :]
        scores = lax.dot_general(q, k, (((1,), (1,)), ((), ())), preferred_element_type=F32) * dint
        inner = jnp.dot(scores.astype(BF16), v, preferred_element_type=F32)
        qd = (q.astype(F32) * qdec).astype(BF16)
        inner = inner + jnp.dot(qd, s.astype(BF16), preferred_element_type=F32)
        kd = (k.astype(F32) * kdec).astype(BF16)
        kv = lax.dot_general(kd, v, (((0,), (0,)), ((), ())), preferred_element_type=F32)
        s = cd * s + kv
        mu = jnp.mean(inner, axis=-1, keepdims=True)
        xc = inner - mu
        var = jnp.mean(xc * xc, axis=-1, keepdims=True)
        y = xc * lax.rsqrt(var + LN_EPS) * sg_ref[rows, :].astype(F32)
        o_ref[rows, :] = y.astype(BF16)
    state_ref[...] = s


def _retention(rq, rk, rv, sg):
    b, s, _ = rq.shape
    h, dk, dv, c = RET_HEADS, RET_QK_DIM, RET_V_DIM, RET_CHUNK
    tm = min(RET_ROW_TILE, s)
    f = np.float32
    log_g = np.log(f(1.0) - f(2.0) ** (f(-5.0) - np.arange(h, dtype=f))).astype(f)
    idx = np.arange(c, dtype=f)
    diff = idx[:, None] - idx[None, :]
    d_intra = np.where(diff >= 0, np.exp(log_g[:, None, None] * np.maximum(diff, f(0.0))), f(0.0)).astype(f)
    k_decay = np.exp(log_g[:, None] * (f(c - 1.0) - idx)[None, :]).astype(f)
    q_decay = np.exp(log_g[:, None] * (idx + f(1.0))[None, :]).astype(f)
    chunk_decay = np.exp(log_g * f(c)).astype(f)
    kdec = np.ascontiguousarray(np.broadcast_to(k_decay[:, :, None], (h, c, dk)))
    qdec = np.ascontiguousarray(np.broadcast_to(q_decay[:, :, None], (h, c, dk)))
    head_rows = lambda bi, hi, t: (bi, t, hi)
    head_tab = lambda bi, hi, t: (hi, 0, 0)
    return pl.pallas_call(
        _ret_kernel,
        out_shape=jax.ShapeDtypeStruct((b, s, h * dv), BF16),
        grid=(b, h, s // tm),
        in_specs=[pl.BlockSpec(memory_space=pltpu.SMEM),
                  pl.BlockSpec((None, tm, dk), head_rows),
                  pl.BlockSpec((None, tm, dk), head_rows),
                  pl.BlockSpec((None, tm, dv), head_rows),
                  pl.BlockSpec((None, tm, dv), head_rows),
                  pl.BlockSpec((None, c, c), head_tab),
                  pl.BlockSpec((None, c, dk), head_tab),
                  pl.BlockSpec((None, c, dk), head_tab)],
        out_specs=pl.BlockSpec((None, tm, dv), head_rows),
        scratch_shapes=[pltpu.VMEM((dk, dv), F32)],
        compiler_params=pltpu.CompilerParams(dimension_semantics=("arbitrary", "arbitrary", "arbitrary"),
                                             vmem_limit_bytes=V7X_VMEM_LIMIT_BYTES),
        name="ret",
    )(chunk_decay, rq, rk, rv, sg, d_intra, kdec, qdec)


def _shift_rows(g, prev, k):
    rolled = pltpu.roll(g, k, 0)
    sub = lax.broadcasted_iota(jnp.int32, prev.shape, 0)
    top = jnp.where(sub < k, pltpu.roll(prev, k, 0), rolled[:prev.shape[0], :])
    return jnp.concatenate([top, rolled[prev.shape[0]:, :]], axis=0)


def _mix_ffn_kernel(ya_ref, yr_ref, ga_ref, gr_ref, x_ref, mod_ref, pa_ref, pr_ref, wo_ref, ln1g_ref, ln1b_ref,
                    wg_ref, wu_ref, wd_ref, cw_ref, cb_ref, ln2g_ref, ln2b_ref,
                    o_ref, tail_ref, x1_ref, h2_ref, act_ref, *, d, alpha):
    tm = x_ref.shape[0]

    @pl.when(pl.program_id(1) == 0)
    def _():
        tail_ref[...] = jnp.zeros_like(tail_ref)

    g1 = mod_ref[:, 2 * d:3 * d]
    sh2 = mod_ref[:, 3 * d:4 * d]
    sc2 = mod_ref[:, 4 * d:5 * d]
    sub = tm // MERGE_SUBTILES
    for i in range(MERGE_SUBTILES):
        rows = slice(i * sub, (i + 1) * sub)
        a = jnp.dot(ya_ref[rows, :], pa_ref[...], preferred_element_type=F32)
        r = jnp.dot(yr_ref[rows, :], pr_ref[...], preferred_element_type=F32)
        merged = ga_ref[rows, :].astype(F32) * a + gr_ref[rows, :].astype(F32) * r
        y = jnp.dot(merged.astype(BF16), wo_ref[...], preferred_element_type=F32)
        x1 = _layer_norm_rows(alpha * x_ref[rows, :] + g1 * y, ln1g_ref[...], ln1b_ref[...])
        x1_ref[rows, :] = x1
        h2_ref[rows, :] = (x1 * (1.0 + sc2) + sh2).astype(BF16)

    fc = FF_CHUNK
    n_chunks = wg_ref.shape[1] // fc
    early = {}
    for i in range(MERGE_SUBTILES):
        rows = slice(i * sub, (i + 1) * sub)
        for c in range(min(FFN_EARLY_CHUNKS, n_chunks)):
            cs = slice(c * fc, (c + 1) * fc)
            early[i, c] = (jnp.dot(h2_ref[rows, :], wg_ref[:, cs], preferred_element_type=F32),
                           jnp.dot(h2_ref[rows, :], wu_ref[:, cs], preferred_element_type=F32))
    h = h2_ref[...]
    for c in range(n_chunks):
        cs = slice(c * fc, (c + 1) * fc)
        if (0, c) in early:
            g = jnp.concatenate([early[i, c][0] for i in range(MERGE_SUBTILES)], axis=0)
            u = jnp.concatenate([early[i, c][1] for i in range(MERGE_SUBTILES)], axis=0)
        else:
            g = jnp.dot(h, wg_ref[:, cs], preferred_element_type=F32)
            u = jnp.dot(h, wu_ref[:, cs], preferred_element_type=F32)
        prev = tail_ref[c]
        tail_ref[c] = g[tm - 8:, :]
        gc = (_shift_rows(g, prev, 2) * cw_ref[0:1, cs] + _shift_rows(g, prev, 1) * cw_ref[1:2, cs]
              + g * cw_ref[2:3, cs] + cb_ref[:, cs])
        act_ref[:, cs] = (gc * (1.0 + lax.erf(gc * (2.0 ** -0.5))) * u).astype(BF16)
    half_g2 = 0.5 * mod_ref[:, 5 * d:6 * d]
    for i in range(MERGE_SUBTILES):
        rows = slice(i * sub, (i + 1) * sub)
        y = jnp.dot(act_ref[rows, :], wd_ref[...], preferred_element_type=F32)
        o_ref[rows, :] = _layer_norm_rows(alpha * x1_ref[rows, :] + half_g2 * y, ln2g_ref[...], ln2b_ref[...])


def _mix_ffn(ya, yr, ga, gr, x, mod, w_pa, w_pr, w_o, ln1_g, ln1_b,
             w_gate, w_up, conv_w, conv_b, w_down, ln2_g, ln2_b, alpha):
    b, s, d = x.shape
    dff = w_gate.shape[1]
    tm = ROW_TILE
    row = lambda bi, t: (bi, t, 0)
    pa, pr, wo = w_pa.astype(BF16), w_pr.astype(BF16), w_o.astype(BF16)
    wg, wu, wd = w_gate.astype(BF16), w_up.astype(BF16), w_down.astype(BF16)
    return pl.pallas_call(
        functools.partial(_mix_ffn_kernel, d=d, alpha=alpha),
        out_shape=jax.ShapeDtypeStruct((b, s, d), F32),
        grid=(b, s // tm),
        in_specs=[pl.BlockSpec((None, tm, ya.shape[-1]), row),
                  pl.BlockSpec((None, tm, yr.shape[-1]), row),
                  pl.BlockSpec((None, tm, d), row),
                  pl.BlockSpec((None, tm, d), row),
                  pl.BlockSpec((None, tm, d), row),
                  pl.BlockSpec((None, 1, N_MOD * d), lambda bi, t: (bi, 0, 0)),
                  _resident(pa.shape), _resident(pr.shape), _resident(wo.shape),
                  _resident((1, d)), _resident((1, d)),
                  _resident(wg.shape), _resident(wu.shape), _resident(wd.shape),
                  _resident(conv_w.shape), _resident((1, dff)),
                  _resident((1, d)), _resident((1, d))],
        out_specs=pl.BlockSpec((None, tm, d), row),
        scratch_shapes=[pltpu.VMEM((dff // FF_CHUNK, 8, FF_CHUNK), F32),
                        pltpu.VMEM((tm, d), F32),
                        pltpu.VMEM((tm, d), BF16),
                        pltpu.VMEM((tm, dff), BF16)],
        compiler_params=pltpu.CompilerParams(dimension_semantics=("arbitrary", "arbitrary"),
                                             vmem_limit_bytes=V7X_VMEM_LIMIT_BYTES),
        name="mix_ffn",
    )(ya, yr, ga, gr, x, mod, pa, pr, wo, ln1_g.reshape(1, d), ln1_b.reshape(1, d),
      wg, wu, wd, conv_w, conv_b.reshape(1, dff), ln2_g.reshape(1, d), ln2_b.reshape(1, d))


def kernel(x, c, w_ada, b_ada, w_in, w_proj_moba, w_proj_ret, w_out, ln1_g, ln1_b, w_ff_gate, w_ff_up,
           ff_conv_w, ff_conv_b, w_ff_down, ln2_g, ln2_b):
    depth = w_ada.shape[0]
    alpha = (2.0 * depth) ** 0.25
    for l in range(depth):
        mod = _ada(c, w_ada[l], b_ada[l])
        qt, k, v, rq, rk, rv, sg, ga, gr = _inproj(x, mod, w_in[l])
        ya = _moba(qt, k, v)
        yr = _retention(rq, rk, rv, sg)
        x = _mix_ffn(ya, yr, ga, gr, x, mod, w_proj_moba[l], w_proj_ret[l], w_out[l], ln1_g[l], ln1_b[l],
                     w_ff_gate[l], w_ff_up[l], ff_conv_w[l], ff_conv_b[l], w_ff_down[l], ln2_g[l], ln2_b[l],
                     alpha)
    return x
```

```python
import functools

import jax
import jax.numpy as jnp
import numpy as np
from jax import lax
from jax.experimental import pallas as pl
from jax.experimental.pallas import tpu as pltpu

F32 = jnp.float32
BF16 = jnp.bfloat16

MOBA_HEADS = 8
MOBA_HEAD_DIM = 64
MOBA_BLOCK = 256
MOBA_TOPK = 3
ROPE_THETA = 10000.0
RET_HEADS = 4
RET_QK_DIM = 128
RET_V_DIM = 256
RET_CHUNK = 128
RET_ANGLE_BASE = 10000.0
N_MOD = 6
LN_EPS = 1e-5
NEG_BIG = -1e30
BELOW_NEG_BIG = -3e38
CONV_WIDTH = 3

V7X_VMEM_LIMIT_BYTES = 56 * 1024 * 1024
LANES = 128
BF16_SUBLANES = 16
LOG2_E = 1.4426950408889634
ROW_TILE = 512
FF_CHUNK = 256
RET_ROW_TILE = 4096
MERGE_SUBTILES = 2
FFN_EARLY_CHUNKS = 3
MOBA_HEADS_PER_STEP = 4
MOBA_GROUP_LOG2 = 1


def _resident(shape):
    nd = len(shape)
    return pl.BlockSpec(shape, lambda *_: (0,) * nd, pipeline_mode=pl.Buffered(1))


def _layer_norm_rows(z, g, b):
    mu = jnp.mean(z, axis=-1, keepdims=True)
    zc = z - mu
    var = jnp.mean(zc * zc, axis=-1, keepdims=True)
    return zc * lax.rsqrt(var + LN_EPS) * g + b


def _ada_kernel(c_ref, w_ref, b_ref, o_ref):
    c = c_ref[...]
    a = c * jax.nn.sigmoid(c)
    o_ref[...] = jnp.dot(a, w_ref[...], preferred_element_type=F32,
                         precision=lax.Precision.HIGHEST) + b_ref[...]


def _ada(c, w_ada, b_ada):
    b, d = c.shape
    n = w_ada.shape[1]
    rows = 8
    cp = jnp.pad(c, ((0, rows - b), (0, 0)))
    nblk = 1536
    out = pl.pallas_call(
        _ada_kernel,
        out_shape=jax.ShapeDtypeStruct((rows, n), F32),
        grid=(n // nblk,),
        in_specs=[pl.BlockSpec((rows, d), lambda j: (0, 0)),
                  pl.BlockSpec((d, nblk), lambda j: (0, j)),
                  pl.BlockSpec((1, nblk), lambda j: (0, j))],
        out_specs=pl.BlockSpec((rows, nblk), lambda j: (0, j)),
        compiler_params=pltpu.CompilerParams(dimension_semantics=("arbitrary",),
                                             vmem_limit_bytes=V7X_VMEM_LIMIT_BYTES),
        name="ada",
    )(cp, w_ada, b_ada.reshape(1, n))
    return out[:b].reshape(b, 1, n)


def _inproj_kernel(x_ref, mod_ref, wn_ref, ck_ref, sk_ref, cr_ref, sr_ref,
                   qt_ref, k_ref, v_ref, rq_ref, rk_ref, rv_ref, sg_ref, ga_ref, gr_ref, *, d):
    tm = x_ref.shape[0]
    mw = MOBA_HEADS * MOBA_HEAD_DIM
    rw = RET_HEADS * RET_QK_DIM
    vw = RET_HEADS * RET_V_DIM
    sh1 = mod_ref[:, 0:d]
    sc1 = mod_ref[:, d:2 * d]
    h = (x_ref[...] * (1.0 + sc1) + sh1).astype(BF16)

    def nat(col0, width):
        return jnp.dot(h, wn_ref[:, col0:col0 + width], preferred_element_type=F32)

    lane = lax.broadcasted_iota(jnp.int32, (tm, LANES), 1)
    half = MOBA_HEAD_DIM // 2
    ck = ck_ref[...]
    sk = sk_ref[...]
    low = (lane & half) == 0

    def rope(xs):
        partner = jnp.where(low, pltpu.roll(xs, LANES - half, 1), pltpu.roll(xs, half, 1))
        return xs * ck + partner * sk

    q_scale = LOG2_E * MOBA_HEAD_DIM ** -0.5
    qn = nat(0, mw)
    qr = jnp.concatenate([rope(qn[:, g * LANES:(g + 1) * LANES]) for g in range(mw // LANES)], axis=1)
    qt = (qr * q_scale).T.astype(BF16)
    for blk in range(tm // MOBA_BLOCK):
        qt_ref[blk] = qt[:, blk * MOBA_BLOCK:(blk + 1) * MOBA_BLOCK]

    kn = nat(mw, mw)
    for g in range(mw // LANES):
        res = rope(kn[:, g * LANES:(g + 1) * LANES]).astype(BF16)
        for blk in range(tm // MOBA_BLOCK):
            k_ref[blk, :, g * LANES:(g + 1) * LANES] = res[blk * MOBA_BLOCK:(blk + 1) * MOBA_BLOCK, :]

    for c0 in range(0, mw, 256):
        vn = nat(2 * mw + c0, 256).astype(BF16)
        for blk in range(tm // MOBA_BLOCK):
            v_ref[blk, :, c0:c0 + 256] = vn[blk * MOBA_BLOCK:(blk + 1) * MOBA_BLOCK, :]

    cr = cr_ref[...]
    sr = sr_ref[...]
    even = (lane & 1) == 0
    for col0, dst, scale in ((3 * mw, rq_ref, None), (3 * mw + rw, rk_ref, RET_QK_DIM ** -0.5)):
        rn = nat(col0, rw)
        for g in range(rw // LANES):
            xs = rn[:, g * LANES:(g + 1) * LANES]
            partner = jnp.where(even, pltpu.roll(xs, LANES - 1, 1), pltpu.roll(xs, 1, 1))
            res = xs * cr + partner * sr
            if scale is not None:
                res = res * scale
            dst[:, g * LANES:(g + 1) * LANES] = res.astype(BF16)

    col = 3 * mw + 2 * rw
    for c0 in range(0, vw, 512):
        rv_ref[:, c0:c0 + 512] = nat(col + c0, 512).astype(BF16)
    col += vw
    for c0 in range(0, vw, 512):
        z = nat(col + c0, 512)
        sg_ref[:, c0:c0 + 512] = (z * jax.nn.sigmoid(z)).astype(BF16)
    col += vw
    for dst in (ga_ref, gr_ref):
        for c0 in range(0, d, 512):
            dst[:, c0:c0 + 512] = jax.nn.sigmoid(nat(col + c0, 512)).astype(BF16)
        col += d


def _rope_tables(s):
    f = np.float32
    pos = np.arange(s, dtype=f)
    hd = MOBA_HEAD_DIM
    inv_freq = (f(ROPE_THETA) ** (-np.arange(0, hd, 2, dtype=f) / f(hd))).astype(f)
    ang = pos[:, None] * inv_freq[None, :]
    cos, sin = np.cos(ang).astype(f), np.sin(ang).astype(f)
    ck = np.tile(cos, (1, 2 * LANES // hd))
    sk = np.tile(np.concatenate([-sin, sin], axis=1), (1, LANES // hd))
    dk = RET_QK_DIM
    freq = (f(1.0) / (f(RET_ANGLE_BASE) ** np.linspace(0.0, 1.0, dk // 2, dtype=f))).astype(f)
    angr = pos[:, None] * freq[None, :]
    cosr, sinr = np.cos(angr).astype(f), np.sin(angr).astype(f)
    cr = np.repeat(cosr, 2, axis=1)
    sr = np.stack([-sinr, sinr], axis=-1).reshape(s, dk)
    return ck, sk, cr, sr


def _inproj(x, mod, w_in):
    b, s, d = x.shape
    mw = MOBA_HEADS * MOBA_HEAD_DIM
    rw = RET_HEADS * RET_QK_DIM
    vw = RET_HEADS * RET_V_DIM
    tm = ROW_TILE
    nb = s // MOBA_BLOCK
    bpt = tm // MOBA_BLOCK
    w_n = w_in.astype(BF16)
    ck, sk, cr, sr = _rope_tables(s)
    row = lambda bi, t: (bi, t, 0)
    blk4 = lambda bi, t: (bi, t, 0, 0)
    tab = lambda bi, t: (t, 0)
    out_shapes = (
        jax.ShapeDtypeStruct((b, nb, mw, MOBA_BLOCK), BF16),
        jax.ShapeDtypeStruct((b, nb, MOBA_BLOCK, mw), BF16),
        jax.ShapeDtypeStruct((b, nb, MOBA_BLOCK, mw), BF16),
        jax.ShapeDtypeStruct((b, s, rw), BF16),
        jax.ShapeDtypeStruct((b, s, rw), BF16),
        jax.ShapeDtypeStruct((b, s, vw), BF16),
        jax.ShapeDtypeStruct((b, s, vw), BF16),
        jax.ShapeDtypeStruct((b, s, d), BF16),
        jax.ShapeDtypeStruct((b, s, d), BF16),
    )
    out_specs = (
        pl.BlockSpec((None, bpt, mw, MOBA_BLOCK), blk4),
        pl.BlockSpec((None, bpt, MOBA_BLOCK, mw), blk4),
        pl.BlockSpec((None, bpt, MOBA_BLOCK, mw), blk4),
        pl.BlockSpec((None, tm, rw), row),
        pl.BlockSpec((None, tm, rw), row),
        pl.BlockSpec((None, tm, vw), row),
        pl.BlockSpec((None, tm, vw), row),
        pl.BlockSpec((None, tm, d), row),
        pl.BlockSpec((None, tm, d), row),
    )
    in_specs = [
        pl.BlockSpec((None, tm, d), row),
        pl.BlockSpec((None, 1, N_MOD * d), lambda bi, t: (bi, 0, 0)),
        _resident(w_n.shape),
        pl.BlockSpec((tm, LANES), tab), pl.BlockSpec((tm, LANES), tab),
        pl.BlockSpec((tm, LANES), tab), pl.BlockSpec((tm, LANES), tab),
    ]
    return pl.pallas_call(
        functools.partial(_inproj_kernel, d=d),
        out_shape=out_shapes,
        grid=(b, s // tm),
        in_specs=in_specs,
        out_specs=out_specs,
        compiler_params=pltpu.CompilerParams(dimension_semantics=("arbitrary", "arbitrary"),
                                             vmem_limit_bytes=V7X_VMEM_LIMIT_BYTES),
        name="inproj",
    )(x, mod, w_n, ck, sk, cr, sr)


def _switch(index, branches, *operands):
    def build(lo, hi):
        if hi - lo == 1:
            return branches[lo]
        mid = (lo + hi) // 2
        return lambda *ops: lax.cond(index < mid, build(lo, mid), build(mid, hi), *ops)
    return build(0, len(branches))(*operands)


def _moba_kernel(qt_ref, k_ref, v_ref, hot_ref, o_ref, kmean_ref, vt_ref, qa_ref, s_ref):
    n = pl.program_id(2)
    nb = k_ref.shape[0]
    hd = MOBA_HEAD_DIM
    blk = MOBA_BLOCK
    hs = MOBA_HEADS_PER_STEP
    grp = 1 << MOBA_GROUP_LOG2
    sub = 8

    @pl.when(n == 0)
    def _():
        for j in range(nb):
            kmean_ref[j:j + 1, :] = jnp.mean(k_ref[j].astype(F32), axis=0, keepdims=True)
            vt_ref[j] = v_ref[j].astype(F32).T.astype(BF16)

    qt = qt_ref[...]
    feat = lax.broadcasted_iota(jnp.int32, (LANES, blk), 0)
    km = kmean_ref[...]
    km_lane = lax.broadcasted_iota(jnp.int32, km.shape, 1)
    blk_id = lax.broadcasted_iota(jnp.int32, (nb, blk), 0)
    valid = blk_id < n
    blk_f = blk_id.astype(F32)

    def col_max(st):
        return jnp.max(st.reshape(st.shape[0] // sub, sub, blk), axis=0)

    key_i = lax.broadcasted_iota(jnp.int32, (blk, blk), 0)
    qry_i = lax.broadcasted_iota(jnp.int32, (blk, blk), 1)
    causal = key_i <= qry_i

    def row_max(st):
        return jnp.max(col_max(st), axis=0, keepdims=True)

    ones = jnp.ones((BF16_SUBLANES, blk), BF16)

    def prologue():
        pad_rows = jnp.zeros((LANES - nb, blk), BF16)
        km_rows = []
        for hh in range(hs):
            pair0 = (hh // 2) * LANES
            in_head = (feat >= hh * hd - pair0) & (feat < (hh + 1) * hd - pair0)
            qa_ref[hh, 0:LANES] = jnp.where(in_head, qt[pair0:pair0 + LANES], 0.0).astype(BF16)
            qa_ref[hh, LANES + nb:2 * LANES] = pad_rows
            kmh = jnp.where((km_lane >= hh * hd) & (km_lane < (hh + 1) * hd), km, 0.0)
            km_hi = kmh.astype(BF16)
            km_rows += [km_hi, (kmh - km_hi.astype(F32)).astype(BF16)]
        gate_all = jnp.dot(jnp.concatenate(km_rows, axis=0), qt, preferred_element_type=F32)
        gates = [gate_all[2 * hh * nb:(2 * hh + 1) * nb] + gate_all[(2 * hh + 1) * nb:(2 * hh + 2) * nb]
                 for hh in range(hs)]

        m_own, p_own = [], []
        for hh in range(hs):
            pair0 = (hh // 2) * LANES
            st = jnp.dot(k_ref[n, :, pair0:pair0 + LANES], qa_ref[hh, 0:LANES], preferred_element_type=F32)
            st = jnp.where(causal, st, NEG_BIG)
            m_own.append(row_max(st))
            p_own.append(jnp.exp2(st - m_own[hh]).astype(BF16))

        for hh in range(hs):
            gate = jnp.where(valid, gates[hh], NEG_BIG)
            taken = jnp.zeros((nb, blk), F32)
            for _ in range(MOBA_TOPK):
                best = jnp.max(gate, axis=0, keepdims=True)
                first = jnp.min(jnp.where(gate == best, blk_f, float(nb)), axis=0, keepdims=True)
                pick = blk_f == first
                taken = jnp.where(pick, 1.0, taken)
                gate = jnp.where(pick, BELOW_NEG_BIG, gate)
            qa_ref[hh, LANES:LANES + nb] = jnp.where(valid & (taken > 0.5), 0.0, NEG_BIG).astype(BF16)
        return m_own, p_own

    n_groups = lax.shift_right_logical(n + (grp - 1), MOBA_GROUP_LOG2)

    def score_group(gi, slot, m_seen):
        j0 = grp * gi
        hot = hot_ref[j0:j0 + grp].reshape(grp * blk, LANES)
        keys = [jnp.concatenate([k_ref[j0:j0 + grp, :, p0:p0 + LANES].reshape(grp * blk, LANES), hot], axis=1)
                for p0 in range(0, k_ref.shape[2], LANES)]
        out = []
        for hh in range(hs):
            st = jnp.dot(keys[hh // 2], qa_ref[hh], preferred_element_type=F32)
            s_ref[slot, hh] = st.reshape(grp, blk, blk)
            out.append(jnp.maximum(m_seen[hh], row_max(st)))
        return out

    def absorb(acc, m_old, m_new, scores, values):
        p = jnp.exp2(scores - m_new).astype(BF16)
        return acc * jnp.exp2(m_old - m_new) + jnp.dot(values, p, preferred_element_type=F32)

    def finish(num_groups):
        m_own, p_own = prologue()
        m_used, m_seen = m_own, m_own
        if num_groups:
            m_seen = score_group(0, 0, m_own)
        v_own = vt_ref[n]
        acc = [jnp.dot(jnp.concatenate([v_own[hh * hd:(hh + 1) * hd, :], ones], axis=0), p_own[hh],
                       preferred_element_type=F32) for hh in range(hs)]
        for g in range(num_groups):
            cur = g % 2
            m_next = score_group(g + 1, 1 - cur, m_seen) if g + 1 < num_groups else m_seen
            new_acc = []
            for hh in range(hs):
                values = jnp.concatenate(
                    [jnp.concatenate([vt_ref[grp * g + i, hh * hd:(hh + 1) * hd, :], ones], axis=0)
                     for i in range(grp)], axis=1)
                new_acc.append(absorb(acc[hh], m_used[hh], m_seen[hh],
                                      s_ref[cur, hh].reshape(grp * blk, blk), values))
            m_used, m_seen, acc = m_seen, m_next, new_acc
        outs = [a[:hd, :] * (1.0 / a[hd:hd + 1, :]) for a in acc]
        o_ref[...] = jnp.concatenate(outs, axis=0).T.astype(BF16)

    _switch(n_groups, [functools.partial(finish, g) for g in range(nb // grp + 1)])


def _moba(qt, k, vt):
    b, nb, mw, blk = qt.shape
    hs = MOBA_HEADS_PER_STEP
    pair = hs * MOBA_HEAD_DIM
    assert nb <= LANES
    hot = np.broadcast_to(np.arange(LANES)[None, None, :] == np.arange(nb)[:, None, None], (nb, blk, LANES))
    hot = jnp.asarray(hot.astype(np.float32), BF16)
    return pl.pallas_call(
        _moba_kernel,
        out_shape=jax.ShapeDtypeStruct((b, nb * blk, mw), BF16),
        grid=(b, mw // pair, nb),
        in_specs=[pl.BlockSpec((None, None, pair, blk), lambda bi, hp, n: (bi, n, hp, 0)),
                  pl.BlockSpec((None, nb, blk, pair), lambda bi, hp, n: (bi, 0, 0, hp)),
                  pl.BlockSpec((None, nb, blk, pair), lambda bi, hp, n: (bi, 0, 0, hp)),
                  _resident(hot.shape)],
        out_specs=pl.BlockSpec((None, blk, pair), lambda bi, hp, n: (bi, n, hp)),
        scratch_shapes=[pltpu.VMEM((nb, pair), F32),
                        pltpu.VMEM((nb, pair, blk), BF16),
                        pltpu.VMEM((hs, 2 * LANES, blk), BF16),
                        pltpu.VMEM((2, hs, 1 << MOBA_GROUP_LOG2, blk, blk), F32)],
        compiler_params=pltpu.CompilerParams(dimension_semantics=("arbitrary", "arbitrary", "arbitrary"),
                                             vmem_limit_bytes=V7X_VMEM_LIMIT_BYTES),
        name="moba",
    )(qt, k, vt, hot)


def _ret_kernel(cdec_ref, q_ref, k_ref, v_ref, sg_ref, dint_ref, kdec_ref, qdec_ref, o_ref, state_ref):
    hi = pl.program_id(1)

    @pl.when(pl.program_id(2) == 0)
    def _():
        state_ref[...] = jnp.zeros_like(state_ref)

    c = RET_CHUNK
    dint = dint_ref[...]
    kdec = kdec_ref[...]
    qdec = qdec_ref[...]
    cd = cdec_ref[hi]
    s = state_ref[...]
    for ci in range(q_ref.shape[0] // c):
        rows = slice(ci * c, (ci + 1) * c)
        q = q_ref[rows, :]
        k = k_ref[rows, :]
        v = v_ref[rows, :]
        scores = lax.dot_general(q, k, (((1,), (1,)), ((), ())), preferred_element_type=F32) * dint
        inner = jnp.dot(scores.astype(BF16), v, preferred_element_type=F32)
        qd = (q.astype(F32) * qdec).astype(BF16)
        inner = inner + jnp.dot(qd, s.astype(BF16), preferred_element_type=F32)
        kd = (k.astype(F32) * kdec).astype(BF16)
        kv = lax.dot_general(kd, v, (((0,), (0,)), ((), ())), preferred_element_type=F32)
        s = cd * s + kv
        mu = jnp.mean(inner, axis=-1, keepdims=True)
        xc = inner - mu
        var = jnp.mean(xc * xc, axis=-1, keepdims=True)
        y = xc * lax.rsqrt(var + LN_EPS) * sg_ref[rows, :].astype(F32)
        o_ref[rows, :] = y.astype(BF16)
    state_ref[...] = s


def _retention(rq, rk, rv, sg):
    b, s, _ = rq.shape
    h, dk, dv, c = RET_HEADS, RET_QK_DIM, RET_V_DIM, RET_CHUNK
    tm = min(RET_ROW_TILE, s)
    f = np.float32
    log_g = np.log(f(1.0) - f(2.0) ** (f(-5.0) - np.arange(h, dtype=f))).astype(f)
    idx = np.arange(c, dtype=f)
    diff = idx[:, None] - idx[None, :]
    d_intra = np.where(diff >= 0, np.exp(log_g[:, None, None] * np.maximum(diff, f(0.0))), f(0.0)).astype(f)
    k_decay = np.exp(log_g[:, None] * (f(c - 1.0) - idx)[None, :]).astype(f)
    q_decay = np.exp(log_g[:, None] * (idx + f(1.0))[None, :]).astype(f)
    chunk_decay = np.exp(log_g * f(c)).astype(f)
    kdec = np.ascontiguousarray(np.broadcast_to(k_decay[:, :, None], (h, c, dk)))
    qdec = np.ascontiguousarray(np.broadcast_to(q_decay[:, :, None], (h, c, dk)))
    head_rows = lambda bi, hi, t: (bi, t, hi)
    head_tab = lambda bi, hi, t: (hi, 0, 0)
    return pl.pallas_call(
        _ret_kernel,
        out_shape=jax.ShapeDtypeStruct((b, s, h * dv), BF16),
        grid=(b, h, s // tm),
        in_specs=[pl.BlockSpec(memory_space=pltpu.SMEM),
                  pl.BlockSpec((None, tm, dk), head_rows),
                  pl.BlockSpec((None, tm, dk), head_rows),
                  pl.BlockSpec((None, tm, dv), head_rows),
                  pl.BlockSpec((None, tm, dv), head_rows),
                  pl.BlockSpec((None, c, c), head_tab),
                  pl.BlockSpec((None, c, dk), head_tab),
                  pl.BlockSpec((None, c, dk), head_tab)],
        out_specs=pl.BlockSpec((None, tm, dv), head_rows),
        scratch_shapes=[pltpu.VMEM((dk, dv), F32)],
        compiler_params=pltpu.CompilerParams(dimension_semantics=("arbitrary", "arbitrary", "arbitrary"),
                                             vmem_limit_bytes=V7X_VMEM_LIMIT_BYTES),
        name="ret",
    )(chunk_decay, rq, rk, rv, sg, d_intra, kdec, qdec)


def _shift_rows(g, prev, k):
    rolled = pltpu.roll(g, k, 0)
    sub = lax.broadcasted_iota(jnp.int32, prev.shape, 0)
    top = jnp.where(sub < k, pltpu.roll(prev, k, 0), rolled[:prev.shape[0], :])
    return jnp.concatenate([top, rolled[prev.shape[0]:, :]], axis=0)


def _mix_ffn_kernel(ya_ref, yr_ref, ga_ref, gr_ref, x_ref, mod_ref, pa_ref, pr_ref, wo_ref, ln1g_ref, ln1b_ref,
                    wg_ref, wu_ref, wd_ref, cw_ref, cb_ref, ln2g_ref, ln2b_ref,
                    o_ref, tail_ref, x1_ref, h2_ref, act_ref, *, d, alpha):
    tm = x_ref.shape[0]

    @pl.when(pl.program_id(1) == 0)
    def _():
        tail_ref[...] = jnp.zeros_like(tail_ref)

    g1 = mod_ref[:, 2 * d:3 * d]
    sh2 = mod_ref[:, 3 * d:4 * d]
    sc2 = mod_ref[:, 4 * d:5 * d]
    sub = tm // MERGE_SUBTILES
    for i in range(MERGE_SUBTILES):
        rows = slice(i * sub, (i + 1) * sub)
        a = jnp.dot(ya_ref[rows, :], pa_ref[...], preferred_element_type=F32)
        r = jnp.dot(yr_ref[rows, :], pr_ref[...], preferred_element_type=F32)
        merged = ga_ref[rows, :].astype(F32) * a + gr_ref[rows, :].astype(F32) * r
        y = jnp.dot(merged.astype(BF16), wo_ref[...], preferred_element_type=F32)
        x1 = _layer_norm_rows(alpha * x_ref[rows, :] + g1 * y, ln1g_ref[...], ln1b_ref[...])
        x1_ref[rows, :] = x1
        h2_ref[rows, :] = (x1 * (1.0 + sc2) + sh2).astype(BF16)

    fc = FF_CHUNK
    n_chunks = wg_ref.shape[1] // fc
    early = {}
    for i in range(MERGE_SUBTILES):
        rows = slice(i * sub, (i + 1) * sub)
        for c in range(min(FFN_EARLY_CHUNKS, n_chunks)):
            cs = slice(c * fc, (c + 1) * fc)
            early[i, c] = (jnp.dot(h2_ref[rows, :], wg_ref[:, cs], preferred_element_type=F32),
                           jnp.dot(h2_ref[rows, :], wu_ref[:, cs], preferred_element_type=F32))
    h = h2_ref[...]
    for c in range(n_chunks):
        cs = slice(c * fc, (c + 1) * fc)
        if (0, c) in early:
            g = jnp.concatenate([early[i, c][0] for i in range(MERGE_SUBTILES)], axis=0)
            u = jnp.concatenate([early[i, c][1] for i in range(MERGE_SUBTILES)], axis=0)
        else:
            g = jnp.dot(h, wg_ref[:, cs], preferred_element_type=F32)
            u = jnp.dot(h, wu_ref[:, cs], preferred_element_type=F32)
        prev = tail_ref[c]
        tail_ref[c] = g[tm - 8:, :]
        gc = (_shift_rows(g, prev, 2) * cw_ref[0:1, cs] + _shift_rows(g, prev, 1) * cw_ref[1:2, cs]
              + g * cw_ref[2:3, cs] + cb_ref[:, cs])
        act_ref[:, cs] = (gc * (1.0 + lax.erf(gc * (2.0 ** -0.5))) * u).astype(BF16)
    half_g2 = 0.5 * mod_ref[:, 5 * d:6 * d]
    for i in range(MERGE_SUBTILES):
        rows = slice(i * sub, (i + 1) * sub)
        y = jnp.dot(act_ref[rows, :], wd_ref[...], preferred_element_type=F32)
        o_ref[rows, :] = _layer_norm_rows(alpha * x1_ref[rows, :] + half_g2 * y, ln2g_ref[...], ln2b_ref[...])


def _mix_ffn(ya, yr, ga, gr, x, mod, w_pa, w_pr, w_o, ln1_g, ln1_b,
             w_gate, w_up, conv_w, conv_b, w_down, ln2_g, ln2_b, alpha):
    b, s, d = x.shape
    dff = w_gate.shape[1]
    tm = ROW_TILE
    row = lambda bi, t: (bi, t, 0)
    pa, pr, wo = w_pa.astype(BF16), w_pr.astype(BF16), w_o.astype(BF16)
    wg, wu, wd = w_gate.astype(BF16), w_up.astype(BF16), w_down.astype(BF16)
    return pl.pallas_call(
        functools.partial(_mix_ffn_kernel, d=d, alpha=alpha),
        out_shape=jax.ShapeDtypeStruct((b, s, d), F32),
        grid=(b, s // tm),
        in_specs=[pl.BlockSpec((None, tm, ya.shape[-1]), row),
                  pl.BlockSpec((None, tm, yr.shape[-1]), row),
                  pl.BlockSpec((None, tm, d), row),
                  pl.BlockSpec((None, tm, d), row),
                  pl.BlockSpec((None, tm, d), row),
                  pl.BlockSpec((None, 1, N_MOD * d), lambda bi, t: (bi, 0, 0)),
                  _resident(pa.shape), _resident(pr.shape), _resident(wo.shape),
                  _resident((1, d)), _resident((1, d)),
                  _resident(wg.shape), _resident(wu.shape), _resident(wd.shape),
                  _resident(conv_w.shape), _resident((1, dff)),
                  _resident((1, d)), _resident((1, d))],
        out_specs=pl.BlockSpec((None, tm, d), row),
        scratch_shapes=[pltpu.VMEM((dff // FF_CHUNK, 8, FF_CHUNK), F32),
                        pltpu.VMEM((tm, d), F32),
                        pltpu.VMEM((tm, d), BF16),
                        pltpu.VMEM((tm, dff), BF16)],
        compiler_params=pltpu.CompilerParams(dimension_semantics=("arbitrary", "arbitrary"),
                                             vmem_limit_bytes=V7X_VMEM_LIMIT_BYTES),
        name="mix_ffn",
    )(ya, yr, ga, gr, x, mod, pa, pr, wo, ln1_g.reshape(1, d), ln1_b.reshape(1, d),
      wg, wu, wd, conv_w, conv_b.reshape(1, dff), ln2_g.reshape(1, d), ln2_b.reshape(1, d))


def kernel(x, c, w_ada, b_ada, w_in, w_proj_moba, w_proj_ret, w_out, ln1_g, ln1_b, w_ff_gate, w_ff_up,
           ff_conv_w, ff_conv_b, w_ff_down, ln2_g, ln2_b):
    depth = w_ada.shape[0]
    alpha = (2.0 * depth) ** 0.25
    for l in range(depth):
        mod = _ada(c, w_ada[l], b_ada[l])
        qt, k, v, rq, rk, rv, sg, ga, gr = _inproj(x, mod, w_in[l])
        ya = _moba(qt, k, v)
        yr = _retention(rq, rk, rv, sg)
        x = _mix_ffn(ya, yr, ga, gr, x, mod, w_proj_moba[l], w_proj_ret[l], w_out[l], ln1_g[l], ln1_b[l],
                     w_ff_gate[l], w_ff_up[l], ff_conv_w[l], ff_conv_b[l], w_ff_down[l], ln2_g[l], ln2_b[l],
                     alpha)
    return x
```

```python
import functools

import jax
import jax.numpy as jnp
import numpy as np
from jax import lax
from jax.experimental import pallas as pl
from jax.experimental.pallas import tpu as pltpu

F32 = jnp.float32
BF16 = jnp.bfloat16

MOBA_HEADS = 8
MOBA_HEAD_DIM = 64
MOBA_BLOCK = 256
MOBA_TOPK = 3
ROPE_THETA = 10000.0
RET_HEADS = 4
RET_QK_DIM = 128
RET_V_DIM = 256
RET_CHUNK = 128
RET_ANGLE_BASE = 10000.0
N_MOD = 6
LN_EPS = 1e-5
NEG_BIG = -1e30
BELOW_NEG_BIG = -3e38
CONV_WIDTH = 3

V7X_VMEM_LIMIT_BYTES = 56 * 1024 * 1024
LANES = 128
BF16_SUBLANES = 16
LOG2_E = 1.4426950408889634
ROW_TILE = 512
FF_CHUNK = 256
RET_ROW_TILE = 4096
MERGE_SUBTILES = 2
FFN_EARLY_CHUNKS = 3
MOBA_HEADS_PER_STEP = 4
MOBA_GROUP_LOG2 = 1


def _resident(shape):
    nd = len(shape)
    return pl.BlockSpec(shape, lambda *_: (0,) * nd, pipeline_mode=pl.Buffered(1))


def _layer_norm_rows(z, g, b):
    mu = jnp.mean(z, axis=-1, keepdims=True)
    zc = z - mu
    var = jnp.mean(zc * zc, axis=-1, keepdims=True)
    return zc * lax.rsqrt(var + LN_EPS) * g + b


def _ada_kernel(c_ref, w_ref, b_ref, o_ref):
    c = c_ref[...]
    a = c * jax.nn.sigmoid(c)
    o_ref[...] = jnp.dot(a, w_ref[...], preferred_element_type=F32,
                         precision=lax.Precision.HIGHEST) + b_ref[...]


def _ada(c, w_ada, b_ada):
    b, d = c.shape
    n = w_ada.shape[1]
    rows = 8
    cp = jnp.pad(c, ((0, rows - b), (0, 0)))
    nblk = 1536
    out = pl.pallas_call(
        _ada_kernel,
        out_shape=jax.ShapeDtypeStruct((rows, n), F32),
        grid=(n // nblk,),
        in_specs=[pl.BlockSpec((rows, d), lambda j: (0, 0)),
                  pl.BlockSpec((d, nblk), lambda j: (0, j)),
                  pl.BlockSpec((1, nblk), lambda j: (0, j))],
        out_specs=pl.BlockSpec((rows, nblk), lambda j: (0, j)),
        compiler_params=pltpu.CompilerParams(dimension_semantics=("arbitrary",),
                                             vmem_limit_bytes=V7X_VMEM_LIMIT_BYTES),
        name="ada",
    )(cp, w_ada, b_ada.reshape(1, n))
    return out[:b].reshape(b, 1, n)


def _inproj_kernel(x_ref, mod_ref, wn_ref, ck_ref, sk_ref, cr_ref, sr_ref,
                   qt_ref, k_ref, v_ref, rq_ref, rk_ref, rv_ref, sg_ref, ga_ref, gr_ref, *, d):
    tm = x_ref.shape[0]
    mw = MOBA_HEADS * MOBA_HEAD_DIM
    rw = RET_HEADS * RET_QK_DIM
    vw = RET_HEADS * RET_V_DIM
    sh1 = mod_ref[:, 0:d]
    sc1 = mod_ref[:, d:2 * d]
    h = (x_ref[...] * (1.0 + sc1) + sh1).astype(BF16)

    def nat(col0, width):
        return jnp.dot(h, wn_ref[:, col0:col0 + width], preferred_element_type=F32)

    lane = lax.broadcasted_iota(jnp.int32, (tm, LANES), 1)
    half = MOBA_HEAD_DIM // 2
    ck = ck_ref[...]
    sk = sk_ref[...]
    low = (lane & half) == 0

    def rope(xs):
        partner = jnp.where(low, pltpu.roll(xs, LANES - half, 1), pltpu.roll(xs, half, 1))
        return xs * ck + partner * sk

    q_scale = LOG2_E * MOBA_HEAD_DIM ** -0.5
    qn = nat(0, mw)
    qr = jnp.concatenate([rope(qn[:, g * LANES:(g + 1) * LANES]) for g in range(mw // LANES)], axis=1)
    qt = (qr * q_scale).T.astype(BF16)
    for blk in range(tm // MOBA_BLOCK):
        qt_ref[blk] = qt[:, blk * MOBA_BLOCK:(blk + 1) * MOBA_BLOCK]

    kn = nat(mw, mw)
    for g in range(mw // LANES):
        res = rope(kn[:, g * LANES:(g + 1) * LANES]).astype(BF16)
        for blk in range(tm // MOBA_BLOCK):
            k_ref[blk, :, g * LANES:(g + 1) * LANES] = res[blk * MOBA_BLOCK:(blk + 1) * MOBA_BLOCK, :]

    for c0 in range(0, mw, 256):
        vn = nat(2 * mw + c0, 256).astype(BF16)
        for blk in range(tm // MOBA_BLOCK):
            v_ref[blk, :, c0:c0 + 256] = vn[blk * MOBA_BLOCK:(blk + 1) * MOBA_BLOCK, :]

    cr = cr_ref[...]
    sr = sr_ref[...]
    even = (lane & 1) == 0
    for col0, dst, scale in ((3 * mw, rq_ref, None), (3 * mw + rw, rk_ref, RET_QK_DIM ** -0.5)):
        rn = nat(col0, rw)
        for g in range(rw // LANES):
            xs = rn[:, g * LANES:(g + 1) * LANES]
            partner = jnp.where(even, pltpu.roll(xs, LANES - 1, 1), pltpu.roll(xs, 1, 1))
            res = xs * cr + partner * sr
            if scale is not None:
                res = res * scale
            dst[:, g * LANES:(g + 1) * LANES] = res.astype(BF16)

    col = 3 * mw + 2 * rw
    for c0 in range(0, vw, 512):
        rv_ref[:, c0:c0 + 512] = nat(col + c0, 512).astype(BF16)
    col += vw
    for c0 in range(0, vw, 512):
        z = nat(col + c0, 512)
        sg_ref[:, c0:c0 + 512] = (z * jax.nn.sigmoid(z)).astype(BF16)
    col += vw
    for dst in (ga_ref, gr_ref):
        for c0 in range(0, d, 512):
            dst[:, c0:c0 + 512] = jax.nn.sigmoid(nat(col + c0, 512)).astype(BF16)
        col += d


def _rope_tables(s):
    f = np.float32
    pos = np.arange(s, dtype=f)
    hd = MOBA_HEAD_DIM
    inv_freq = (f(ROPE_THETA) ** (-np.arange(0, hd, 2, dtype=f) / f(hd))).astype(f)
    ang = pos[:, None] * inv_freq[None, :]
    cos, sin = np.cos(ang).astype(f), np.sin(ang).astype(f)
    ck = np.tile(cos, (1, 2 * LANES // hd))
    sk = np.tile(np.concatenate([-sin, sin], axis=1), (1, LANES // hd))
    dk = RET_QK_DIM
    freq = (f(1.0) / (f(RET_ANGLE_BASE) ** np.linspace(0.0, 1.0, dk // 2, dtype=f))).astype(f)
    angr = pos[:, None] * freq[None, :]
    cosr, sinr = np.cos(angr).astype(f), np.sin(angr).astype(f)
    cr = np.repeat(cosr, 2, axis=1)
    sr = np.stack([-sinr, sinr], axis=-1).reshape(s, dk)
    return ck, sk, cr, sr


def _inproj(x, mod, w_in):
    b, s, d = x.shape
    mw = MOBA_HEADS * MOBA_HEAD_DIM
    rw = RET_HEADS * RET_QK_DIM
    vw = RET_HEADS * RET_V_DIM
    tm = ROW_TILE
    nb = s // MOBA_BLOCK
    bpt = tm // MOBA_BLOCK
    w_n = w_in.astype(BF16)
    ck, sk, cr, sr = _rope_tables(s)
    row = lambda bi, t: (bi, t, 0)
    blk4 = lambda bi, t: (bi, t, 0, 0)
    tab = lambda bi, t: (t, 0)
    out_shapes = (
        jax.ShapeDtypeStruct((b, nb, mw, MOBA_BLOCK), BF16),
        jax.ShapeDtypeStruct((b, nb, MOBA_BLOCK, mw), BF16),
        jax.ShapeDtypeStruct((b, nb, MOBA_BLOCK, mw), BF16),
        jax.ShapeDtypeStruct((b, s, rw), BF16),
        jax.ShapeDtypeStruct((b, s, rw), BF16),
        jax.ShapeDtypeStruct((b, s, vw), BF16),
        jax.ShapeDtypeStruct((b, s, vw), BF16),
        jax.ShapeDtypeStruct((b, s, d), BF16),
        jax.ShapeDtypeStruct((b, s, d), BF16),
    )
    out_specs = (
        pl.BlockSpec((None, bpt, mw, MOBA_BLOCK), blk4),
        pl.BlockSpec((None, bpt, MOBA_BLOCK, mw), blk4),
        pl.BlockSpec((None, bpt, MOBA_BLOCK, mw), blk4),
        pl.BlockSpec((None, tm, rw), row),
        pl.BlockSpec((None, tm, rw), row),
        pl.BlockSpec((None, tm, vw), row),
        pl.BlockSpec((None, tm, vw), row),
        pl.BlockSpec((None, tm, d), row),
        pl.BlockSpec((None, tm, d), row),
    )
    in_specs = [
        pl.BlockSpec((None, tm, d), row),
        pl.BlockSpec((None, 1, N_MOD * d), lambda bi, t: (bi, 0, 0)),
        _resident(w_n.shape),
        pl.BlockSpec((tm, LANES), tab), pl.BlockSpec((tm, LANES), tab),
        pl.BlockSpec((tm, LANES), tab), pl.BlockSpec((tm, LANES), tab),
    ]
    return pl.pallas_call(
        functools.partial(_inproj_kernel, d=d),
        out_shape=out_shapes,
        grid=(b, s // tm),
        in_specs=in_specs,
        out_specs=out_specs,
        compiler_params=pltpu.CompilerParams(dimension_semantics=("arbitrary", "arbitrary"),
                                             vmem_limit_bytes=V7X_VMEM_LIMIT_BYTES),
        name="inproj",
    )(x, mod, w_n, ck, sk, cr, sr)


def _switch(index, branches, *operands):
    def build(lo, hi):
        if hi - lo == 1:
            return branches[lo]
        mid = (lo + hi) // 2
        return lambda *ops: lax.cond(index < mid, build(lo, mid), build(mid, hi), *ops)
    return build(0, len(branches))(*operands)


def _moba_kernel(qta_ref, qtb_ref, k_ref, v_ref, hot_ref, oa_ref, ob_ref, kmean_ref, vt_ref,
                 qaa_ref, qab_ref, sa_ref, sb_ref):
    step = pl.program_id(2)
    nb = k_ref.shape[0]
    hd = MOBA_HEAD_DIM
    blk = MOBA_BLOCK
    hs = MOBA_HEADS_PER_STEP
    grp = 1 << MOBA_GROUP_LOG2
    sub = 8

    @pl.when(step == 0)
    def _():
        for j in range(nb):
            kmean_ref[j:j + 1, :] = jnp.mean(k_ref[j].astype(F32), axis=0, keepdims=True)
            vt_ref[j] = v_ref[j].astype(F32).T.astype(BF16)

    feat = lax.broadcasted_iota(jnp.int32, (LANES, blk), 0)
    km = kmean_ref[...]
    km_lane = lax.broadcasted_iota(jnp.int32, km.shape, 1)
    blk_id = lax.broadcasted_iota(jnp.int32, (nb, blk), 0)
    blk_f = blk_id.astype(F32)

    def col_max(st):
        return jnp.max(st.reshape(st.shape[0] // sub, sub, blk), axis=0)

    key_i = lax.broadcasted_iota(jnp.int32, (blk, blk), 0)
    qry_i = lax.broadcasted_iota(jnp.int32, (blk, blk), 1)
    causal = key_i <= qry_i

    def row_max(st):
        return jnp.max(col_max(st), axis=0, keepdims=True)

    ones = jnp.ones((BF16_SUBLANES, blk), BF16)

    def absorb(acc, m_old, m_new, scores, values):
        p = jnp.exp2(scores - m_new).astype(BF16)
        return acc * jnp.exp2(m_old - m_new) + jnp.dot(values, p, preferred_element_type=F32)

    def attend(n, num_groups, qt_ref, o_ref, qa_ref, s_ref):
        qt = qt_ref[...]
        valid = blk_id < n
        pad_rows = jnp.zeros((LANES - nb, blk), BF16)
        km_rows = []
        for hh in range(hs):
            pair0 = (hh // 2) * LANES
            in_head = (feat >= hh * hd - pair0) & (feat < (hh + 1) * hd - pair0)
            qa_ref[hh, 0:LANES] = jnp.where(in_head, qt[pair0:pair0 + LANES], 0.0).astype(BF16)
            qa_ref[hh, LANES + nb:2 * LANES] = pad_rows
            kmh = jnp.where((km_lane >= hh * hd) & (km_lane < (hh + 1) * hd), km, 0.0)
            km_hi = kmh.astype(BF16)
            km_rows += [km_hi, (kmh - km_hi.astype(F32)).astype(BF16)]
        gate_all = jnp.dot(jnp.concatenate(km_rows, axis=0), qt, preferred_element_type=F32)
        gates = [gate_all[2 * hh * nb:(2 * hh + 1) * nb] + gate_all[(2 * hh + 1) * nb:(2 * hh + 2) * nb]
                 for hh in range(hs)]

        m_own, p_own = [], []
        for hh in range(hs):
            pair0 = (hh // 2) * LANES
            st = jnp.dot(k_ref[n, :, pair0:pair0 + LANES], qa_ref[hh, 0:LANES], preferred_element_type=F32)
            st = jnp.where(causal, st, NEG_BIG)
            m_own.append(row_max(st))
            p_own.append(jnp.exp2(st - m_own[hh]).astype(BF16))

        for hh in range(hs):
            gate = jnp.where(valid, gates[hh], NEG_BIG)
            taken = jnp.zeros((nb, blk), F32)
            for _ in range(MOBA_TOPK):
                best = jnp.max(gate, axis=0, keepdims=True)
                first = jnp.min(jnp.where(gate == best, blk_f, float(nb)), axis=0, keepdims=True)
                pick = blk_f == first
                taken = jnp.where(pick, 1.0, taken)
                gate = jnp.where(pick, BELOW_NEG_BIG, gate)
            qa_ref[hh, LANES:LANES + nb] = jnp.where(valid & (taken > 0.5), 0.0, NEG_BIG).astype(BF16)

        def score_group(gi, slot, m_seen):
            j0 = grp * gi
            hot = hot_ref[j0:j0 + grp].reshape(grp * blk, LANES)
            keys = [jnp.concatenate([k_ref[j0:j0 + grp, :, p0:p0 + LANES].reshape(grp * blk, LANES), hot],
                                    axis=1) for p0 in range(0, k_ref.shape[2], LANES)]
            out = []
            for hh in range(hs):
                st = jnp.dot(keys[hh // 2], qa_ref[hh], preferred_element_type=F32)
                s_ref[slot, hh] = st.reshape(grp, blk, blk)
                out.append(jnp.maximum(m_seen[hh], row_max(st)))
            return out

        m_used, m_seen = m_own, m_own
        if num_groups:
            m_seen = score_group(0, 0, m_own)
        v_own = vt_ref[n]
        acc = [jnp.dot(jnp.concatenate([v_own[hh * hd:(hh + 1) * hd, :], ones], axis=0), p_own[hh],
                       preferred_element_type=F32) for hh in range(hs)]
        for g in range(num_groups):
            cur = g % 2
            m_next = score_group(g + 1, 1 - cur, m_seen) if g + 1 < num_groups else m_seen
            new_acc = []
            for hh in range(hs):
                values = jnp.concatenate(
                    [jnp.concatenate([vt_ref[grp * g + i, hh * hd:(hh + 1) * hd, :], ones], axis=0)
                     for i in range(grp)], axis=1)
                new_acc.append(absorb(acc[hh], m_used[hh], m_seen[hh],
                                      s_ref[cur, hh].reshape(grp * blk, blk), values))
            m_used, m_seen, acc = m_seen, m_next, new_acc
        outs = [a[:hd, :] * (1.0 / a[hd:hd + 1, :]) for a in acc]
        o_ref[...] = jnp.concatenate(outs, axis=0).T.astype(BF16)

    all_groups = nb // grp

    def pair(first_groups):
        attend(step, first_groups, qta_ref, oa_ref, qaa_ref, sa_ref)
        attend(nb - 1 - step, all_groups - first_groups, qtb_ref, ob_ref, qab_ref, sb_ref)

    _switch(lax.shift_right_logical(step + (grp - 1), MOBA_GROUP_LOG2),
            [functools.partial(pair, g) for g in range(all_groups // 2 + 1)])


def _moba(qt, k, v):
    b, nb, mw, blk = qt.shape
    hs = MOBA_HEADS_PER_STEP
    grp = 1 << MOBA_GROUP_LOG2
    pair = hs * MOBA_HEAD_DIM
    assert nb <= LANES and nb % (2 * grp) == 0
    half = nb // 2
    hot = np.broadcast_to(np.arange(LANES)[None, None, :] == np.arange(nb)[:, None, None], (nb, blk, LANES))
    hot = jnp.asarray(hot.astype(np.float32), BF16)
    half_out = jax.ShapeDtypeStruct((b, half * blk, mw), BF16)
    return pl.pallas_call(
        _moba_kernel,
        out_shape=(half_out, half_out),
        grid=(b, mw // pair, half),
        in_specs=[pl.BlockSpec((None, None, pair, blk), lambda bi, hp, i: (bi, i, hp, 0)),
                  pl.BlockSpec((None, None, pair, blk), lambda bi, hp, i: (bi, nb - 1 - i, hp, 0)),
                  pl.BlockSpec((None, nb, blk, pair), lambda bi, hp, i: (bi, 0, 0, hp)),
                  pl.BlockSpec((None, nb, blk, pair), lambda bi, hp, i: (bi, 0, 0, hp)),
                  _resident(hot.shape)],
        out_specs=(pl.BlockSpec((None, blk, pair), lambda bi, hp, i: (bi, i, hp)),
                   pl.BlockSpec((None, blk, pair), lambda bi, hp, i: (bi, half - 1 - i, hp))),
        scratch_shapes=[pltpu.VMEM((nb, pair), F32),
                        pltpu.VMEM((nb, pair, blk), BF16),
                        pltpu.VMEM((hs, 2 * LANES, blk), BF16),
                        pltpu.VMEM((hs, 2 * LANES, blk), BF16),
                        pltpu.VMEM((2, hs, grp, blk, blk), F32),
                        pltpu.VMEM((2, hs, grp, blk, blk), F32)],
        compiler_params=pltpu.CompilerParams(dimension_semantics=("arbitrary", "arbitrary", "arbitrary"),
                                             vmem_limit_bytes=V7X_VMEM_LIMIT_BYTES),
        name="moba",
    )(qt, qt, k, v, hot)


def _ret_kernel(cdec_ref, q_ref, k_ref, v_ref, sg_ref, dint_ref, kdec_ref, qdec_ref, o_ref, state_ref):
    hi = pl.program_id(1)

    @pl.when(pl.program_id(2) == 0)
    def _():
        state_ref[...] = jnp.zeros_like(state_ref)

    c = RET_CHUNK
    dint = dint_ref[...]
    kdec = kdec_ref[...]
    qdec = qdec_ref[...]
    cd = cdec_ref[hi]
    s = state_ref[...]
    for ci in range(q_ref.shape[0] // c):
        rows = slice(ci * c, (ci + 1) * c)
        q = q_ref[rows, :]
        k = k_ref[rows, :]
        v = v_ref[rows, :]
        scores = lax.dot_general(q, k, (((1,), (1,)), ((), ())), preferred_element_type=F32) * dint
        inner = jnp.dot(scores.astype(BF16), v, preferred_element_type=F32)
        qd = (q.astype(F32) * qdec).astype(BF16)
        inner = inner + jnp.dot(qd, s.astype(BF16), preferred_element_type=F32)
        kd = (k.astype(F32) * kdec).astype(BF16)
        kv = lax.dot_general(kd, v, (((0,), (0,)), ((), ())), preferred_element_type=F32)
        s = cd * s + kv
        mu = jnp.mean(inner, axis=-1, keepdims=True)
        xc = inner - mu
        var = jnp.mean(xc * xc, axis=-1, keepdims=True)
        y = xc * lax.rsqrt(var + LN_EPS) * sg_ref[rows, :].astype(F32)
        o_ref[rows, :] = y.astype(BF16)
    state_ref[...] = s


def _retention(rq, rk, rv, sg):
    b, s, _ = rq.shape
    h, dk, dv, c = RET_HEADS, RET_QK_DIM, RET_V_DIM, RET_CHUNK
    tm = min(RET_ROW_TILE, s)
    f = np.float32
    log_g = np.log(f(1.0) - f(2.0) ** (f(-5.0) - np.arange(h, dtype=f))).astype(f)
    idx = np.arange(c, dtype=f)
    diff = idx[:, None] - idx[None, :]
    d_intra = np.where(diff >= 0, np.exp(log_g[:, None, None] * np.maximum(diff, f(0.0))), f(0.0)).astype(f)
    k_decay = np.exp(log_g[:, None] * (f(c - 1.0) - idx)[None, :]).astype(f)
    q_decay = np.exp(log_g[:, None] * (idx + f(1.0))[None, :]).astype(f)
    chunk_decay = np.exp(log_g * f(c)).astype(f)
    kdec = np.ascontiguousarray(np.broadcast_to(k_decay[:, :, None], (h, c, dk)))
    qdec = np.ascontiguousarray(np.broadcast_to(q_decay[:, :, None], (h, c, dk)))
    head_rows = lambda bi, hi, t: (bi, t, hi)
    head_tab = lambda bi, hi, t: (hi, 0, 0)
    return pl.pallas_call(
        _ret_kernel,
        out_shape=jax.ShapeDtypeStruct((b, s, h * dv), BF16),
        grid=(b, h, s // tm),
        in_specs=[pl.BlockSpec(memory_space=pltpu.SMEM),
                  pl.BlockSpec((None, tm, dk), head_rows),
                  pl.BlockSpec((None, tm, dk), head_rows),
                  pl.BlockSpec((None, tm, dv), head_rows),
                  pl.BlockSpec((None, tm, dv), head_rows),
                  pl.BlockSpec((None, c, c), head_tab),
                  pl.BlockSpec((None, c, dk), head_tab),
                  pl.BlockSpec((None, c, dk), head_tab)],
        out_specs=pl.BlockSpec((None, tm, dv), head_rows),
        scratch_shapes=[pltpu.VMEM((dk, dv), F32)],
        compiler_params=pltpu.CompilerParams(dimension_semantics=("arbitrary", "arbitrary", "arbitrary"),
                                             vmem_limit_bytes=V7X_VMEM_LIMIT_BYTES),
        name="ret",
    )(chunk_decay, rq, rk, rv, sg, d_intra, kdec, qdec)


def _shift_rows(g, prev, k):
    rolled = pltpu.roll(g, k, 0)
    sub = lax.broadcasted_iota(jnp.int32, prev.shape, 0)
    top = jnp.where(sub < k, pltpu.roll(prev, k, 0), rolled[:prev.shape[0], :])
    return jnp.concatenate([top, rolled[prev.shape[0]:, :]], axis=0)


def _mix_ffn_kernel(ya_lo_ref, ya_hi_ref, yr_ref, ga_ref, gr_ref, x_ref, mod_ref,
                    pa_ref, pr_ref, wo_ref, ln1g_ref, ln1b_ref,
                    wg_ref, wu_ref, wd_ref, cw_ref, cb_ref, ln2g_ref, ln2b_ref,
                    o_ref, tail_ref, x1_ref, h2_ref, act_ref, *, d, alpha):
    tm = x_ref.shape[0]

    @pl.when(pl.program_id(1) == 0)
    def _():
        tail_ref[...] = jnp.zeros_like(tail_ref)

    g1 = mod_ref[:, 2 * d:3 * d]
    sh2 = mod_ref[:, 3 * d:4 * d]
    sc2 = mod_ref[:, 4 * d:5 * d]
    first_half = pl.program_id(1) < pl.num_programs(1) // 2
    sub = tm // MERGE_SUBTILES
    for i in range(MERGE_SUBTILES):
        rows = slice(i * sub, (i + 1) * sub)
        ya = jnp.where(first_half, ya_lo_ref[rows, :], ya_hi_ref[rows, :])
        a = jnp.dot(ya, pa_ref[...], preferred_element_type=F32)
        r = jnp.dot(yr_ref[rows, :], pr_ref[...], preferred_element_type=F32)
        merged = ga_ref[rows, :].astype(F32) * a + gr_ref[rows, :].astype(F32) * r
        y = jnp.dot(merged.astype(BF16), wo_ref[...], preferred_element_type=F32)
        x1 = _layer_norm_rows(alpha * x_ref[rows, :] + g1 * y, ln1g_ref[...], ln1b_ref[...])
        x1_ref[rows, :] = x1
        h2_ref[rows, :] = (x1 * (1.0 + sc2) + sh2).astype(BF16)

    fc = FF_CHUNK
    n_chunks = wg_ref.shape[1] // fc
    early = {}
    for i in range(MERGE_SUBTILES):
        rows = slice(i * sub, (i + 1) * sub)
        for c in range(min(FFN_EARLY_CHUNKS, n_chunks)):
            cs = slice(c * fc, (c + 1) * fc)
            early[i, c] = (jnp.dot(h2_ref[rows, :], wg_ref[:, cs], preferred_element_type=F32),
                           jnp.dot(h2_ref[rows, :], wu_ref[:, cs], preferred_element_type=F32))
    h = h2_ref[...]
    for c in range(n_chunks):
        cs = slice(c * fc, (c + 1) * fc)
        if (0, c) in early:
            g = jnp.concatenate([early[i, c][0] for i in range(MERGE_SUBTILES)], axis=0)
            u = jnp.concatenate([early[i, c][1] for i in range(MERGE_SUBTILES)], axis=0)
        else:
            g = jnp.dot(h, wg_ref[:, cs], preferred_element_type=F32)
            u = jnp.dot(h, wu_ref[:, cs], preferred_element_type=F32)
        prev = tail_ref[c]
        tail_ref[c] = g[tm - 8:, :]
        gc = (_shift_rows(g, prev, 2) * cw_ref[0:1, cs] + _shift_rows(g, prev, 1) * cw_ref[1:2, cs]
              + g * cw_ref[2:3, cs] + cb_ref[:, cs])
        act_ref[:, cs] = (gc * (1.0 + lax.erf(gc * (2.0 ** -0.5))) * u).astype(BF16)
    half_g2 = 0.5 * mod_ref[:, 5 * d:6 * d]
    for i in range(MERGE_SUBTILES):
        rows = slice(i * sub, (i + 1) * sub)
        y = jnp.dot(act_ref[rows, :], wd_ref[...], preferred_element_type=F32)
        o_ref[rows, :] = _layer_norm_rows(alpha * x1_ref[rows, :] + half_g2 * y, ln2g_ref[...], ln2b_ref[...])


def _mix_ffn(ya_lo, ya_hi, yr, ga, gr, x, mod, w_pa, w_pr, w_o, ln1_g, ln1_b,
             w_gate, w_up, conv_w, conv_b, w_down, ln2_g, ln2_b, alpha):
    b, s, d = x.shape
    dff = w_gate.shape[1]
    tm = ROW_TILE
    row = lambda bi, t: (bi, t, 0)
    half_tiles = s // tm // 2
    assert ya_lo.shape[1] == half_tiles * tm
    pa, pr, wo = w_pa.astype(BF16), w_pr.astype(BF16), w_o.astype(BF16)
    wg, wu, wd = w_gate.astype(BF16), w_up.astype(BF16), w_down.astype(BF16)
    return pl.pallas_call(
        functools.partial(_mix_ffn_kernel, d=d, alpha=alpha),
        out_shape=jax.ShapeDtypeStruct((b, s, d), F32),
        grid=(b, s // tm),
        in_specs=[pl.BlockSpec((None, tm, ya_lo.shape[-1]), lambda bi, t: (bi, jnp.minimum(t, half_tiles - 1), 0)),
                  pl.BlockSpec((None, tm, ya_hi.shape[-1]), lambda bi, t: (bi, jnp.maximum(t - half_tiles, 0), 0)),
                  pl.BlockSpec((None, tm, yr.shape[-1]), row),
                  pl.BlockSpec((None, tm, d), row),
                  pl.BlockSpec((None, tm, d), row),
                  pl.BlockSpec((None, tm, d), row),
                  pl.BlockSpec((None, 1, N_MOD * d), lambda bi, t: (bi, 0, 0)),
                  _resident(pa.shape), _resident(pr.shape), _resident(wo.shape),
                  _resident((1, d)), _resident((1, d)),
                  _resident(wg.shape), _resident(wu.shape), _resident(wd.shape),
                  _resident(conv_w.shape), _resident((1, dff)),
                  _resident((1, d)), _resident((1, d))],
        out_specs=pl.BlockSpec((None, tm, d), row),
        scratch_shapes=[pltpu.VMEM((dff // FF_CHUNK, 8, FF_CHUNK), F32),
                        pltpu.VMEM((tm, d), F32),
                        pltpu.VMEM((tm, d), BF16),
                        pltpu.VMEM((tm, dff), BF16)],
        compiler_params=pltpu.CompilerParams(dimension_semantics=("arbitrary", "arbitrary"),
                                             vmem_limit_bytes=V7X_VMEM_LIMIT_BYTES),
        name="mix_ffn",
    )(ya_lo, ya_hi, yr, ga, gr, x, mod, pa, pr, wo, ln1_g.reshape(1, d), ln1_b.reshape(1, d),
      wg, wu, wd, conv_w, conv_b.reshape(1, dff), ln2_g.reshape(1, d), ln2_b.reshape(1, d))


def kernel(x, c, w_ada, b_ada, w_in, w_proj_moba, w_proj_ret, w_out, ln1_g, ln1_b, w_ff_gate, w_ff_up,
           ff_conv_w, ff_conv_b, w_ff_down, ln2_g, ln2_b):
    depth = w_ada.shape[0]
    alpha = (2.0 * depth) ** 0.25
    for l in range(depth):
        mod = _ada(c, w_ada[l], b_ada[l])
        qt, k, v, rq, rk, rv, sg, ga, gr = _inproj(x, mod, w_in[l])
        ya_lo, ya_hi = _moba(qt, k, v)
        yr = _retention(rq, rk, rv, sg)
        x = _mix_ffn(ya_lo, ya_hi, yr, ga, gr, x, mod, w_proj_moba[l], w_proj_ret[l], w_out[l], ln1_g[l], ln1_b[l],
                     w_ff_gate[l], w_ff_up[l], ff_conv_w[l], ff_conv_b[l], w_ff_down[l], ln2_g[l], ln2_b[l],
                     alpha)
    return x
```

```python
import functools

import jax
import jax.numpy as jnp
import numpy as np
from jax import lax
from jax.experimental import pallas as pl
from jax.experimental.pallas import tpu as pltpu

F32 = jnp.float32
BF16 = jnp.bfloat16

MOBA_HEADS = 8
MOBA_HEAD_DIM = 64
MOBA_BLOCK = 256
MOBA_TOPK = 3
ROPE_THETA = 10000.0
RET_HEADS = 4
RET_QK_DIM = 128
RET_V_DIM = 256
RET_CHUNK = 128
RET_ANGLE_BASE = 10000.0
N_MOD = 6
LN_EPS = 1e-5
NEG_BIG = -1e30
BELOW_NEG_BIG = -3e38
CONV_WIDTH = 3

V7X_VMEM_LIMIT_BYTES = 56 * 1024 * 1024
LANES = 128
BF16_SUBLANES = 16
LOG2_E = 1.4426950408889634
ROW_TILE = 512
FF_CHUNK = 256
RET_ROW_TILE = 4096
MERGE_SUBTILES = 2
FFN_EARLY_CHUNKS = 3
MOBA_HEADS_PER_STEP = 4
MOBA_GROUP_LOG2 = 1


def _resident(shape):
    nd = len(shape)
    return pl.BlockSpec(shape, lambda *_: (0,) * nd, pipeline_mode=pl.Buffered(1))


def _layer_norm_rows(z, g, b):
    mu = jnp.mean(z, axis=-1, keepdims=True)
    zc = z - mu
    var = jnp.mean(zc * zc, axis=-1, keepdims=True)
    return zc * lax.rsqrt(var + LN_EPS) * g + b


def _ada_kernel(c_ref, w_ref, b_ref, o_ref):
    c = c_ref[...]
    a = c * jax.nn.sigmoid(c)
    rows = a.shape[0]
    a_hi = a.astype(BF16)
    a_lo = (a - a_hi.astype(F32)).astype(BF16)
    w = w_ref[...]
    w_hi = w.astype(BF16)
    w_lo = (w - w_hi.astype(F32)).astype(BF16)
    both = jnp.dot(jnp.concatenate([a_hi, a_lo], axis=0), w_hi, preferred_element_type=F32)
    o_ref[...] = (both[:rows] + both[rows:]
                  + jnp.dot(a_hi, w_lo, preferred_element_type=F32)) + b_ref[...]


def _ada(c, w_ada, b_ada):
    b, d = c.shape
    n = w_ada.shape[1]
    rows = BF16_SUBLANES
    cp = jnp.pad(c, ((0, rows - b), (0, 0)))
    nblk = 1536
    out = pl.pallas_call(
        _ada_kernel,
        out_shape=jax.ShapeDtypeStruct((rows, n), F32),
        grid=(n // nblk,),
        in_specs=[pl.BlockSpec((rows, d), lambda j: (0, 0)),
                  pl.BlockSpec((d, nblk), lambda j: (0, j)),
                  pl.BlockSpec((1, nblk), lambda j: (0, j))],
        out_specs=pl.BlockSpec((rows, nblk), lambda j: (0, j)),
        compiler_params=pltpu.CompilerParams(dimension_semantics=("arbitrary",),
                                             vmem_limit_bytes=V7X_VMEM_LIMIT_BYTES),
        name="ada",
    )(cp, w_ada, b_ada.reshape(1, n))
    return out[:b].reshape(b, 1, n)


def _inproj_kernel(x_ref, mod_ref, wn_ref, ck_ref, sk_ref, cr_ref, sr_ref,
                   qt_ref, k_ref, v_ref, rq_ref, rk_ref, rv_ref, sg_ref, ga_ref, gr_ref, *, d):
    tm = x_ref.shape[0]
    mw = MOBA_HEADS * MOBA_HEAD_DIM
    rw = RET_HEADS * RET_QK_DIM
    vw = RET_HEADS * RET_V_DIM
    sh1 = mod_ref[:, 0:d]
    sc1 = mod_ref[:, d:2 * d]
    h = (x_ref[...] * (1.0 + sc1) + sh1).astype(BF16)

    def nat(col0, width):
        return jnp.dot(h, wn_ref[:, col0:col0 + width], preferred_element_type=F32)

    lane = lax.broadcasted_iota(jnp.int32, (tm, LANES), 1)
    half = MOBA_HEAD_DIM // 2
    ck = ck_ref[...]
    sk = sk_ref[...]
    low = (lane & half) == 0

    def rope(xs):
        partner = jnp.where(low, pltpu.roll(xs, LANES - half, 1), pltpu.roll(xs, half, 1))
        return xs * ck + partner * sk

    q_scale = LOG2_E * MOBA_HEAD_DIM ** -0.5
    qn = nat(0, mw)
    qr = jnp.concatenate([rope(qn[:, g * LANES:(g + 1) * LANES]) for g in range(mw // LANES)], axis=1)
    qt = (qr * q_scale).T.astype(BF16)
    for blk in range(tm // MOBA_BLOCK):
        qt_ref[blk] = qt[:, blk * MOBA_BLOCK:(blk + 1) * MOBA_BLOCK]

    kn = nat(mw, mw)
    for g in range(mw // LANES):
        res = rope(kn[:, g * LANES:(g + 1) * LANES]).astype(BF16)
        for blk in range(tm // MOBA_BLOCK):
            k_ref[blk, :, g * LANES:(g + 1) * LANES] = res[blk * MOBA_BLOCK:(blk + 1) * MOBA_BLOCK, :]

    for c0 in range(0, mw, 256):
        vn = nat(2 * mw + c0, 256).astype(BF16)
        for blk in range(tm // MOBA_BLOCK):
            v_ref[blk, :, c0:c0 + 256] = vn[blk * MOBA_BLOCK:(blk + 1) * MOBA_BLOCK, :]

    cr = cr_ref[...]
    sr = sr_ref[...]
    even = (lane & 1) == 0
    for col0, dst, scale in ((3 * mw, rq_ref, None), (3 * mw + rw, rk_ref, RET_QK_DIM ** -0.5)):
        rn = nat(col0, rw)
        for g in range(rw // LANES):
            xs = rn[:, g * LANES:(g + 1) * LANES]
            partner = jnp.where(even, pltpu.roll(xs, LANES - 1, 1), pltpu.roll(xs, 1, 1))
            res = xs * cr + partner * sr
            if scale is not None:
                res = res * scale
            dst[:, g * LANES:(g + 1) * LANES] = res.astype(BF16)

    col = 3 * mw + 2 * rw
    for c0 in range(0, vw, 512):
        rv_ref[:, c0:c0 + 512] = nat(col + c0, 512).astype(BF16)
    col += vw
    for c0 in range(0, vw, 512):
        z = nat(col + c0, 512)
        sg_ref[:, c0:c0 + 512] = (z * jax.nn.sigmoid(z)).astype(BF16)
    col += vw
    for dst in (ga_ref, gr_ref):
        for c0 in range(0, d, 512):
            dst[:, c0:c0 + 512] = jax.nn.sigmoid(nat(col + c0, 512)).astype(BF16)
        col += d


def _rope_tables(s):
    f = np.float32
    pos = np.arange(s, dtype=f)
    hd = MOBA_HEAD_DIM
    inv_freq = (f(ROPE_THETA) ** (-np.arange(0, hd, 2, dtype=f) / f(hd))).astype(f)
    ang = pos[:, None] * inv_freq[None, :]
    cos, sin = np.cos(ang).astype(f), np.sin(ang).astype(f)
    ck = np.tile(cos, (1, 2 * LANES // hd))
    sk = np.tile(np.concatenate([-sin, sin], axis=1), (1, LANES // hd))
    dk = RET_QK_DIM
    freq = (f(1.0) / (f(RET_ANGLE_BASE) ** np.linspace(0.0, 1.0, dk // 2, dtype=f))).astype(f)
    angr = pos[:, None] * freq[None, :]
    cosr, sinr = np.cos(angr).astype(f), np.sin(angr).astype(f)
    cr = np.repeat(cosr, 2, axis=1)
    sr = np.stack([-sinr, sinr], axis=-1).reshape(s, dk)
    return ck, sk, cr, sr


def _inproj(x, mod, w_in):
    b, s, d = x.shape
    mw = MOBA_HEADS * MOBA_HEAD_DIM
    rw = RET_HEADS * RET_QK_DIM
    vw = RET_HEADS * RET_V_DIM
    tm = ROW_TILE
    nb = s // MOBA_BLOCK
    bpt = tm // MOBA_BLOCK
    w_n = w_in.astype(BF16)
    ck, sk, cr, sr = _rope_tables(s)
    row = lambda bi, t: (bi, t, 0)
    blk4 = lambda bi, t: (bi, t, 0, 0)
    tab = lambda bi, t: (t, 0)
    out_shapes = (
        jax.ShapeDtypeStruct((b, nb, mw, MOBA_BLOCK), BF16),
        jax.ShapeDtypeStruct((b, nb, MOBA_BLOCK, mw), BF16),
        jax.ShapeDtypeStruct((b, nb, MOBA_BLOCK, mw), BF16),
        jax.ShapeDtypeStruct((b, s, rw), BF16),
        jax.ShapeDtypeStruct((b, s, rw), BF16),
        jax.ShapeDtypeStruct((b, s, vw), BF16),
        jax.ShapeDtypeStruct((b, s, vw), BF16),
        jax.ShapeDtypeStruct((b, s, d), BF16),
        jax.ShapeDtypeStruct((b, s, d), BF16),
    )
    out_specs = (
        pl.BlockSpec((None, bpt, mw, MOBA_BLOCK), blk4),
        pl.BlockSpec((None, bpt, MOBA_BLOCK, mw), blk4),
        pl.BlockSpec((None, bpt, MOBA_BLOCK, mw), blk4),
        pl.BlockSpec((None, tm, rw), row),
        pl.BlockSpec((None, tm, rw), row),
        pl.BlockSpec((None, tm, vw), row),
        pl.BlockSpec((None, tm, vw), row),
        pl.BlockSpec((None, tm, d), row),
        pl.BlockSpec((None, tm, d), row),
    )
    in_specs = [
        pl.BlockSpec((None, tm, d), row),
        pl.BlockSpec((None, 1, N_MOD * d), lambda bi, t: (bi, 0, 0)),
        _resident(w_n.shape),
        pl.BlockSpec((tm, LANES), tab), pl.BlockSpec((tm, LANES), tab),
        pl.BlockSpec((tm, LANES), tab), pl.BlockSpec((tm, LANES), tab),
    ]
    return pl.pallas_call(
        functools.partial(_inproj_kernel, d=d),
        out_shape=out_shapes,
        grid=(b, s // tm),
        in_specs=in_specs,
        out_specs=out_specs,
        compiler_params=pltpu.CompilerParams(dimension_semantics=("arbitrary", "arbitrary"),
                                             vmem_limit_bytes=V7X_VMEM_LIMIT_BYTES),
        name="inproj",
    )(x, mod, w_n, ck, sk, cr, sr)


def _switch(index, branches, *operands):
    def build(lo, hi):
        if hi - lo == 1:
            return branches[lo]
        mid = (lo + hi) // 2
        return lambda *ops: lax.cond(index < mid, build(lo, mid), build(mid, hi), *ops)
    return build(0, len(branches))(*operands)


def _moba_kernel(qt_ref, k_ref, v_ref, hot_ref, o_ref, kmean_ref, vt_ref, qa_ref, s_ref):
    n = pl.program_id(2)
    nb = k_ref.shape[0]
    hd = MOBA_HEAD_DIM
    blk = MOBA_BLOCK
    hs = MOBA_HEADS_PER_STEP
    grp = 1 << MOBA_GROUP_LOG2
    sub = 8

    @pl.when(n == 0)
    def _():
        for j in range(nb):
            kmean_ref[j:j + 1, :] = jnp.mean(k_ref[j].astype(F32), axis=0, keepdims=True)
            vt_ref[j] = v_ref[j].astype(F32).T.astype(BF16)

    qt = qt_ref[...]
    feat = lax.broadcasted_iota(jnp.int32, (LANES, blk), 0)
    km = kmean_ref[...]
    km_lane = lax.broadcasted_iota(jnp.int32, km.shape, 1)
    blk_id = lax.broadcasted_iota(jnp.int32, (nb, blk), 0)
    valid = blk_id < n
    blk_f = blk_id.astype(F32)

    def col_max(st):
        return jnp.max(st.reshape(st.shape[0] // sub, sub, blk), axis=0)

    key_i = lax.broadcasted_iota(jnp.int32, (blk, blk), 0)
    qry_i = lax.broadcasted_iota(jnp.int32, (blk, blk), 1)
    causal = key_i <= qry_i

    def row_max(st):
        return jnp.max(col_max(st), axis=0, keepdims=True)

    ones = jnp.ones((BF16_SUBLANES, blk), BF16)

    def prologue():
        pad_rows = jnp.zeros((LANES - nb, blk), BF16)
        km_rows = []
        for hh in range(hs):
            pair0 = (hh // 2) * LANES
            in_head = (feat >= hh * hd - pair0) & (feat < (hh + 1) * hd - pair0)
            qa_ref[hh, 0:LANES] = jnp.where(in_head, qt[pair0:pair0 + LANES], 0.0).astype(BF16)
            qa_ref[hh, LANES + nb:2 * LANES] = pad_rows
            kmh = jnp.where((km_lane >= hh * hd) & (km_lane < (hh + 1) * hd), km, 0.0)
            km_hi = kmh.astype(BF16)
            km_rows += [km_hi, (kmh - km_hi.astype(F32)).astype(BF16)]
        gate_all = jnp.dot(jnp.concatenate(km_rows, axis=0), qt, preferred_element_type=F32)
        gates = [gate_all[2 * hh * nb:(2 * hh + 1) * nb] + gate_all[(2 * hh + 1) * nb:(2 * hh + 2) * nb]
                 for hh in range(hs)]

        m_own, p_own = [], []
        for hh in range(hs):
            pair0 = (hh // 2) * LANES
            st = jnp.dot(k_ref[n, :, pair0:pair0 + LANES], qa_ref[hh, 0:LANES], preferred_element_type=F32)
            st = jnp.where(causal, st, NEG_BIG)
            m_own.append(row_max(st))
            p_own.append(jnp.exp2(st - m_own[hh]).astype(BF16))

        for hh in range(hs):
            gate = jnp.where(valid, gates[hh], NEG_BIG)
            taken = jnp.zeros((nb, blk), F32)
            for _ in range(MOBA_TOPK):
                best = jnp.max(gate, axis=0, keepdims=True)
                first = jnp.min(jnp.where(gate == best, blk_f, float(nb)), axis=0, keepdims=True)
                pick = blk_f == first
                taken = jnp.where(pick, 1.0, taken)
                gate = jnp.where(pick, BELOW_NEG_BIG, gate)
            qa_ref[hh, LANES:LANES + nb] = jnp.where(valid & (taken > 0.5), 0.0, NEG_BIG).astype(BF16)
        return m_own, p_own

    n_groups = lax.shift_right_logical(n + (grp - 1), MOBA_GROUP_LOG2)

    def score_group(gi, slot, m_seen):
        j0 = grp * gi
        hot = hot_ref[j0:j0 + grp].reshape(grp * blk, LANES)
        keys = [jnp.concatenate([k_ref[j0:j0 + grp, :, p0:p0 + LANES].reshape(grp * blk, LANES), hot], axis=1)
                for p0 in range(0, k_ref.shape[2], LANES)]
        out = []
        for hh in range(hs):
            st = jnp.dot(keys[hh // 2], qa_ref[hh], preferred_element_type=F32)
            s_ref[slot, hh] = st.reshape(grp, blk, blk)
            out.append(jnp.maximum(m_seen[hh], row_max(st)))
        return out

    def absorb(acc, m_old, m_new, scores, values):
        p = jnp.exp2(scores - m_new).astype(BF16)
        return acc * jnp.exp2(m_old - m_new) + jnp.dot(values, p, preferred_element_type=F32)

    def finish(num_groups):
        m_own, p_own = prologue()
        m_used, m_seen = m_own, m_own
        if num_groups:
            m_seen = score_group(0, 0, m_own)
        v_own = vt_ref[n]
        acc = [jnp.dot(jnp.concatenate([v_own[hh * hd:(hh + 1) * hd, :], ones], axis=0), p_own[hh],
                       preferred_element_type=F32) for hh in range(hs)]
        for g in range(num_groups):
            cur = g % 2
            m_next = score_group(g + 1, 1 - cur, m_seen) if g + 1 < num_groups else m_seen
            new_acc = []
            for hh in range(hs):
                values = jnp.concatenate(
                    [jnp.concatenate([vt_ref[grp * g + i, hh * hd:(hh + 1) * hd, :], ones], axis=0)
                     for i in range(grp)], axis=1)
                new_acc.append(absorb(acc[hh], m_used[hh], m_seen[hh],
                                      s_ref[cur, hh].reshape(grp * blk, blk), values))
            m_used, m_seen, acc = m_seen, m_next, new_acc
        outs = [a[:hd, :] * (1.0 / a[hd:hd + 1, :]) for a in acc]
        o_ref[...] = jnp.concatenate(outs, axis=0).T.astype(BF16)

    _switch(n_groups, [functools.partial(finish, g) for g in range(nb // grp + 1)])


def _moba(qt, k, vt):
    b, nb, mw, blk = qt.shape
    hs = MOBA_HEADS_PER_STEP
    pair = hs * MOBA_HEAD_DIM
    assert nb <= LANES
    hot = np.broadcast_to(np.arange(LANES)[None, None, :] == np.arange(nb)[:, None, None], (nb, blk, LANES))
    hot = jnp.asarray(hot.astype(np.float32), BF16)
    return pl.pallas_call(
        _moba_kernel,
        out_shape=jax.ShapeDtypeStruct((b, nb * blk, mw), BF16),
        grid=(b, mw // pair, nb),
        in_specs=[pl.BlockSpec((None, None, pair, blk), lambda bi, hp, n: (bi, n, hp, 0)),
                  pl.BlockSpec((None, nb, blk, pair), lambda bi, hp, n: (bi, 0, 0, hp)),
                  pl.BlockSpec((None, nb, blk, pair), lambda bi, hp, n: (bi, 0, 0, hp)),
                  _resident(hot.shape)],
        out_specs=pl.BlockSpec((None, blk, pair), lambda bi, hp, n: (bi, n, hp)),
        scratch_shapes=[pltpu.VMEM((nb, pair), F32),
                        pltpu.VMEM((nb, pair, blk), BF16),
                        pltpu.VMEM((hs, 2 * LANES, blk), BF16),
                        pltpu.VMEM((2, hs, 1 << MOBA_GROUP_LOG2, blk, blk), F32)],
        compiler_params=pltpu.CompilerParams(dimension_semantics=("arbitrary", "arbitrary", "arbitrary"),
                                             vmem_limit_bytes=V7X_VMEM_LIMIT_BYTES),
        name="moba",
    )(qt, k, vt, hot)


def _ret_kernel(cdec_ref, q_ref, k_ref, v_ref, sg_ref, dint_ref, kdec_ref, qdec_ref, o_ref, state_ref):
    hi = pl.program_id(1)

    @pl.when(pl.program_id(2) == 0)
    def _():
        state_ref[...] = jnp.zeros_like(state_ref)

    c = RET_CHUNK
    dint = dint_ref[...]
    kdec = kdec_ref[...]
    qdec = qdec_ref[...]
    cd = cdec_ref[hi]
    s = state_ref[...]
    for ci in range(q_ref.shape[0] // c):
        rows = slice(ci * c, (ci + 1) * c)
        q = q_ref[rows, :]
        k = k_ref[rows, :]
        v = v_ref[rows, :]
        scores = lax.dot_general(q, k, (((1,), (1,)), ((), ())), preferred_element_type=F32) * dint
        inner = jnp.dot(scores.astype(BF16), v, preferred_element_type=F32)
        qd = (q.astype(F32) * qdec).astype(BF16)
        inner = inner + jnp.dot(qd, s.astype(BF16), preferred_element_type=F32)
        kd = (k.astype(F32) * kdec).astype(BF16)
        kv = lax.dot_general(kd, v, (((0,), (0,)), ((), ())), preferred_element_type=F32)
        s = cd * s + kv
        mu = jnp.mean(inner, axis=-1, keepdims=True)
        xc = inner - mu
        var = jnp.mean(xc * xc, axis=-1, keepdims=True)
        y = xc * lax.rsqrt(var + LN_EPS) * sg_ref[rows, :].astype(F32)
        o_ref[rows, :] = y.astype(BF16)
    state_ref[...] = s


def _retention(rq, rk, rv, sg):
    b, s, _ = rq.shape
    h, dk, dv, c = RET_HEADS, RET_QK_DIM, RET_V_DIM, RET_CHUNK
    tm = min(RET_ROW_TILE, s)
    f = np.float32
    log_g = np.log(f(1.0) - f(2.0) ** (f(-5.0) - np.arange(h, dtype=f))).astype(f)
    idx = np.arange(c, dtype=f)
    diff = idx[:, None] - idx[None, :]
    d_intra = np.where(diff >= 0, np.exp(log_g[:, None, None] * np.maximum(diff, f(0.0))), f(0.0)).astype(f)
    k_decay = np.exp(log_g[:, None] * (f(c - 1.0) - idx)[None, :]).astype(f)
    q_decay = np.exp(log_g[:, None] * (idx + f(1.0))[None, :]).astype(f)
    chunk_decay = np.exp(log_g * f(c)).astype(f)
    kdec = np.ascontiguousarray(np.broadcast_to(k_decay[:, :, None], (h, c, dk)))
    qdec = np.ascontiguousarray(np.broadcast_to(q_decay[:, :, None], (h, c, dk)))
    head_rows = lambda bi, hi, t: (bi, t, hi)
    head_tab = lambda bi, hi, t: (hi, 0, 0)
    return pl.pallas_call(
        _ret_kernel,
        out_shape=jax.ShapeDtypeStruct((b, s, h * dv), BF16),
        grid=(b, h, s // tm),
        in_specs=[pl.BlockSpec(memory_space=pltpu.SMEM),
                  pl.BlockSpec((None, tm, dk), head_rows),
                  pl.BlockSpec((None, tm, dk), head_rows),
                  pl.BlockSpec((None, tm, dv), head_rows),
                  pl.BlockSpec((None, tm, dv), head_rows),
                  pl.BlockSpec((None, c, c), head_tab),
                  pl.BlockSpec((None, c, dk), head_tab),
                  pl.BlockSpec((None, c, dk), head_tab)],
        out_specs=pl.BlockSpec((None, tm, dv), head_rows),
        scratch_shapes=[pltpu.VMEM((dk, dv), F32)],
        compiler_params=pltpu.CompilerParams(dimension_semantics=("arbitrary", "arbitrary", "arbitrary"),
                                             vmem_limit_bytes=V7X_VMEM_LIMIT_BYTES),
        name="ret",
    )(chunk_decay, rq, rk, rv, sg, d_intra, kdec, qdec)


def _shift_rows(g, prev, k):
    rolled = pltpu.roll(g, k, 0)
    sub = lax.broadcasted_iota(jnp.int32, prev.shape, 0)
    top = jnp.where(sub < k, pltpu.roll(prev, k, 0), rolled[:prev.shape[0], :])
    return jnp.concatenate([top, rolled[prev.shape[0]:, :]], axis=0)


def _mix_ffn_kernel(ya_ref, yr_ref, ga_ref, gr_ref, x_ref, mod_ref, pa_ref, pr_ref, wo_ref, ln1g_ref, ln1b_ref,
                    wg_ref, wu_ref, wd_ref, cw_ref, cb_ref, ln2g_ref, ln2b_ref,
                    o_ref, tail_ref, x1_ref, h2_ref, act_ref, *, d, alpha):
    tm = x_ref.shape[0]

    @pl.when(pl.program_id(1) == 0)
    def _():
        tail_ref[...] = jnp.zeros_like(tail_ref)

    g1 = mod_ref[:, 2 * d:3 * d]
    sh2 = mod_ref[:, 3 * d:4 * d]
    sc2 = mod_ref[:, 4 * d:5 * d]
    sub = tm // MERGE_SUBTILES
    for i in range(MERGE_SUBTILES):
        rows = slice(i * sub, (i + 1) * sub)
        a = jnp.dot(ya_ref[rows, :], pa_ref[...], preferred_element_type=F32)
        r = jnp.dot(yr_ref[rows, :], pr_ref[...], preferred_element_type=F32)
        merged = ga_ref[rows, :].astype(F32) * a + gr_ref[rows, :].astype(F32) * r
        y = jnp.dot(merged.astype(BF16), wo_ref[...], preferred_element_type=F32)
        x1 = _layer_norm_rows(alpha * x_ref[rows, :] + g1 * y, ln1g_ref[...], ln1b_ref[...])
        x1_ref[rows, :] = x1
        h2_ref[rows, :] = (x1 * (1.0 + sc2) + sh2).astype(BF16)

    fc = FF_CHUNK
    n_chunks = wg_ref.shape[1] // fc
    early = {}
    for i in range(MERGE_SUBTILES):
        rows = slice(i * sub, (i + 1) * sub)
        for c in range(min(FFN_EARLY_CHUNKS, n_chunks)):
            cs = slice(c * fc, (c + 1) * fc)
            early[i, c] = (jnp.dot(h2_ref[rows, :], wg_ref[:, cs], preferred_element_type=F32),
                           jnp.dot(h2_ref[rows, :], wu_ref[:, cs], preferred_element_type=F32))
    h = h2_ref[...]
    for c in range(n_chunks):
        cs = slice(c * fc, (c + 1) * fc)
        if (0, c) in early:
            g = jnp.concatenate([early[i, c][0] for i in range(MERGE_SUBTILES)], axis=0)
            u = jnp.concatenate([early[i, c][1] for i in range(MERGE_SUBTILES)], axis=0)
        else:
            g = jnp.dot(h, wg_ref[:, cs], preferred_element_type=F32)
            u = jnp.dot(h, wu_ref[:, cs], preferred_element_type=F32)
        prev = tail_ref[c]
        tail_ref[c] = g[tm - 8:, :]
        gc = (_shift_rows(g, prev, 2) * cw_ref[0:1, cs] + _shift_rows(g, prev, 1) * cw_ref[1:2, cs]
              + g * cw_ref[2:3, cs] + cb_ref[:, cs])
        act_ref[:, cs] = (gc * (1.0 + lax.erf(gc * (2.0 ** -0.5))) * u).astype(BF16)
    half_g2 = 0.5 * mod_ref[:, 5 * d:6 * d]
    for i in range(MERGE_SUBTILES):
        rows = slice(i * sub, (i + 1) * sub)
        y = jnp.dot(act_ref[rows, :], wd_ref[...], preferred_element_type=F32)
        o_ref[rows, :] = _layer_norm_rows(alpha * x1_ref[rows, :] + half_g2 * y, ln2g_ref[...], ln2b_ref[...])


def _mix_ffn(ya, yr, ga, gr, x, mod, w_pa, w_pr, w_o, ln1_g, ln1_b,
             w_gate, w_up, conv_w, conv_b, w_down, ln2_g, ln2_b, alpha):
    b, s, d = x.shape
    dff = w_gate.shape[1]
    tm = ROW_TILE
    row = lambda bi, t: (bi, t, 0)
    pa, pr, wo = w_pa.astype(BF16), w_pr.astype(BF16), w_o.astype(BF16)
    wg, wu, wd = w_gate.astype(BF16), w_up.astype(BF16), w_down.astype(BF16)
    return pl.pallas_call(
        functools.partial(_mix_ffn_kernel, d=d, alpha=alpha),
        out_shape=jax.ShapeDtypeStruct((b, s, d), F32),
        grid=(b, s // tm),
        in_specs=[pl.BlockSpec((None, tm, ya.shape[-1]), row),
                  pl.BlockSpec((None, tm, yr.shape[-1]), row),
                  pl.BlockSpec((None, tm, d), row),
                  pl.BlockSpec((None, tm, d), row),
                  pl.BlockSpec((None, tm, d), row),
                  pl.BlockSpec((None, 1, N_MOD * d), lambda bi, t: (bi, 0, 0)),
                  _resident(pa.shape), _resident(pr.shape), _resident(wo.shape),
                  _resident((1, d)), _resident((1, d)),
                  _resident(wg.shape), _resident(wu.shape), _resident(wd.shape),
                  _resident(conv_w.shape), _resident((1, dff)),
                  _resident((1, d)), _resident((1, d))],
        out_specs=pl.BlockSpec((None, tm, d), row),
        scratch_shapes=[pltpu.VMEM((dff // FF_CHUNK, 8, FF_CHUNK), F32),
                        pltpu.VMEM((tm, d), F32),
                        pltpu.VMEM((tm, d), BF16),
                        pltpu.VMEM((tm, dff), BF16)],
        compiler_params=pltpu.CompilerParams(dimension_semantics=("arbitrary", "arbitrary"),
                                             vmem_limit_bytes=V7X_VMEM_LIMIT_BYTES),
        name="mix_ffn",
    )(ya, yr, ga, gr, x, mod, pa, pr, wo, ln1_g.reshape(1, d), ln1_b.reshape(1, d),
      wg, wu, wd, conv_w, conv_b.reshape(1, dff), ln2_g.reshape(1, d), ln2_b.reshape(1, d))


def kernel(x, c, w_ada, b_ada, w_in, w_proj_moba, w_proj_ret, w_out, ln1_g, ln1_b, w_ff_gate, w_ff_up,
           ff_conv_w, ff_conv_b, w_ff_down, ln2_g, ln2_b):
    depth = w_ada.shape[0]
    alpha = (2.0 * depth) ** 0.25
    for l in range(depth):
        mod = _ada(c, w_ada[l], b_ada[l])
        qt, k, v, rq, rk, rv, sg, ga, gr = _inproj(x, mod, w_in[l])
        ya = _moba(qt, k, v)
        yr = _retention(rq, rk, rv, sg)
        x = _mix_ffn(ya, yr, ga, gr, x, mod, w_proj_moba[l], w_proj_ret[l], w_out[l], ln1_g[l], ln1_b[l],
                     w_ff_gate[l], w_ff_up[l], ff_conv_w[l], ff_conv_b[l], w_ff_down[l], ln2_g[l], ln2_b[l],
                     alpha)
    return x
```

```python
import functools

import jax
import jax.numpy as jnp
import numpy as np
from jax import lax
from jax.experimental import pallas as pl
from jax.experimental.pallas import tpu as pltpu

F32 = jnp.float32
BF16 = jnp.bfloat16

MOBA_HEADS = 8
MOBA_HEAD_DIM = 64
MOBA_BLOCK = 256
MOBA_TOPK = 3
ROPE_THETA = 10000.0
RET_HEADS = 4
RET_QK_DIM = 128
RET_V_DIM = 256
RET_CHUNK = 128
RET_ANGLE_BASE = 10000.0
N_MOD = 6
LN_EPS = 1e-5
NEG_BIG = -1e30
BELOW_NEG_BIG = -3e38
CONV_WIDTH = 3

V7X_VMEM_LIMIT_BYTES = 56 * 1024 * 1024
LANES = 128
BF16_SUBLANES = 16
LOG2_E = 1.4426950408889634
ROW_TILE = 512
FF_CHUNK = 256
RET_ROW_TILE = 4096
MERGE_SUBTILES = 2
FFN_EARLY_CHUNKS = 3
MOBA_HEADS_PER_STEP = 4
MOBA_GROUP_LOG2 = 1


def _resident(shape):
    nd = len(shape)
    return pl.BlockSpec(shape, lambda *_: (0,) * nd, pipeline_mode=pl.Buffered(1))


def _layer_norm_rows(z, g, b):
    mu = jnp.mean(z, axis=-1, keepdims=True)
    zc = z - mu
    var = jnp.mean(zc * zc, axis=-1, keepdims=True)
    return zc * lax.rsqrt(var + LN_EPS) * g + b


def _ada_kernel(c_ref, w_ref, b_ref, o_ref):
    c = c_ref[...]
    a = c * jax.nn.sigmoid(c)
    rows = a.shape[0]
    a_hi = a.astype(BF16)
    a_lo = (a - a_hi.astype(F32)).astype(BF16)
    w = w_ref[...]
    w_hi = w.astype(BF16)
    w_lo = (w - w_hi.astype(F32)).astype(BF16)
    both = jnp.dot(jnp.concatenate([a_hi, a_lo], axis=0), w_hi, preferred_element_type=F32)
    o_ref[...] = (both[:rows] + both[rows:]
                  + jnp.dot(a_hi, w_lo, preferred_element_type=F32)) + b_ref[...]


def _ada(c, w_ada, b_ada):
    b, d = c.shape
    n = w_ada.shape[1]
    rows = BF16_SUBLANES
    cp = jnp.pad(c, ((0, rows - b), (0, 0)))
    nblk = 1536
    out = pl.pallas_call(
        _ada_kernel,
        out_shape=jax.ShapeDtypeStruct((rows, n), F32),
        grid=(n // nblk,),
        in_specs=[pl.BlockSpec((rows, d), lambda j: (0, 0)),
                  pl.BlockSpec((d, nblk), lambda j: (0, j)),
                  pl.BlockSpec((1, nblk), lambda j: (0, j))],
        out_specs=pl.BlockSpec((rows, nblk), lambda j: (0, j)),
        compiler_params=pltpu.CompilerParams(dimension_semantics=("arbitrary",),
                                             vmem_limit_bytes=V7X_VMEM_LIMIT_BYTES),
        name="ada",
    )(cp, w_ada, b_ada.reshape(1, n))
    return out[:b].reshape(b, 1, n)


def _inproj_kernel(x_ref, mod_ref, wn_ref, ck_ref, sk_ref, cr_ref, sr_ref,
                   qt_ref, k_ref, v_ref, rq_ref, rk_ref, rv_ref, sg_ref, ga_ref, gr_ref, *, d):
    tm = x_ref.shape[0]
    mw = MOBA_HEADS * MOBA_HEAD_DIM
    rw = RET_HEADS * RET_QK_DIM
    vw = RET_HEADS * RET_V_DIM
    sh1 = mod_ref[:, 0:d]
    sc1 = mod_ref[:, d:2 * d]
    h = (x_ref[...] * (1.0 + sc1) + sh1).astype(BF16)

    def nat(col0, width):
        return jnp.dot(h, wn_ref[:, col0:col0 + width].astype(BF16), preferred_element_type=F32)

    lane = lax.broadcasted_iota(jnp.int32, (tm, LANES), 1)
    half = MOBA_HEAD_DIM // 2
    ck = ck_ref[...]
    sk = sk_ref[...]
    low = (lane & half) == 0

    def rope(xs):
        partner = jnp.where(low, pltpu.roll(xs, LANES - half, 1), pltpu.roll(xs, half, 1))
        return xs * ck + partner * sk

    q_scale = LOG2_E * MOBA_HEAD_DIM ** -0.5
    qn = nat(0, mw)
    qr = jnp.concatenate([rope(qn[:, g * LANES:(g + 1) * LANES]) for g in range(mw // LANES)], axis=1)
    qt = (qr * q_scale).T.astype(BF16)
    for blk in range(tm // MOBA_BLOCK):
        qt_ref[blk] = qt[:, blk * MOBA_BLOCK:(blk + 1) * MOBA_BLOCK]

    kn = nat(mw, mw)
    for g in range(mw // LANES):
        res = rope(kn[:, g * LANES:(g + 1) * LANES]).astype(BF16)
        for blk in range(tm // MOBA_BLOCK):
            k_ref[blk, :, g * LANES:(g + 1) * LANES] = res[blk * MOBA_BLOCK:(blk + 1) * MOBA_BLOCK, :]

    for c0 in range(0, mw, 256):
        vn = nat(2 * mw + c0, 256).astype(BF16)
        for blk in range(tm // MOBA_BLOCK):
            v_ref[blk, :, c0:c0 + 256] = vn[blk * MOBA_BLOCK:(blk + 1) * MOBA_BLOCK, :]

    cr = cr_ref[...]
    sr = sr_ref[...]
    even = (lane & 1) == 0
    for col0, dst, scale in ((3 * mw, rq_ref, None), (3 * mw + rw, rk_ref, RET_QK_DIM ** -0.5)):
        rn = nat(col0, rw)
        for g in range(rw // LANES):
            xs = rn[:, g * LANES:(g + 1) * LANES]
            partner = jnp.where(even, pltpu.roll(xs, LANES - 1, 1), pltpu.roll(xs, 1, 1))
            res = xs * cr + partner * sr
            if scale is not None:
                res = res * scale
            dst[:, g * LANES:(g + 1) * LANES] = res.astype(BF16)

    col = 3 * mw + 2 * rw
    for c0 in range(0, vw, 512):
        rv_ref[:, c0:c0 + 512] = nat(col + c0, 512).astype(BF16)
    col += vw
    for c0 in range(0, vw, 512):
        z = nat(col + c0, 512)
        sg_ref[:, c0:c0 + 512] = (z * jax.nn.sigmoid(z)).astype(BF16)
    col += vw
    for dst in (ga_ref, gr_ref):
        for c0 in range(0, d, 512):
            dst[:, c0:c0 + 512] = jax.nn.sigmoid(nat(col + c0, 512)).astype(BF16)
        col += d


def _rope_tables(s):
    f = np.float32
    pos = np.arange(s, dtype=f)
    hd = MOBA_HEAD_DIM
    inv_freq = (f(ROPE_THETA) ** (-np.arange(0, hd, 2, dtype=f) / f(hd))).astype(f)
    ang = pos[:, None] * inv_freq[None, :]
    cos, sin = np.cos(ang).astype(f), np.sin(ang).astype(f)
    ck = np.tile(cos, (1, 2 * LANES // hd))
    sk = np.tile(np.concatenate([-sin, sin], axis=1), (1, LANES // hd))
    dk = RET_QK_DIM
    freq = (f(1.0) / (f(RET_ANGLE_BASE) ** np.linspace(0.0, 1.0, dk // 2, dtype=f))).astype(f)
    angr = pos[:, None] * freq[None, :]
    cosr, sinr = np.cos(angr).astype(f), np.sin(angr).astype(f)
    cr = np.repeat(cosr, 2, axis=1)
    sr = np.stack([-sinr, sinr], axis=-1).reshape(s, dk)
    return ck, sk, cr, sr


def _inproj(x, mod, w_in):
    b, s, d = x.shape
    mw = MOBA_HEADS * MOBA_HEAD_DIM
    rw = RET_HEADS * RET_QK_DIM
    vw = RET_HEADS * RET_V_DIM
    tm = ROW_TILE
    nb = s // MOBA_BLOCK
    bpt = tm // MOBA_BLOCK
    w_n = w_in
    ck, sk, cr, sr = _rope_tables(s)
    row = lambda bi, t: (bi, t, 0)
    blk4 = lambda bi, t: (bi, t, 0, 0)
    tab = lambda bi, t: (t, 0)
    out_shapes = (
        jax.ShapeDtypeStruct((b, nb, mw, MOBA_BLOCK), BF16),
        jax.ShapeDtypeStruct((b, nb, MOBA_BLOCK, mw), BF16),
        jax.ShapeDtypeStruct((b, nb, MOBA_BLOCK, mw), BF16),
        jax.ShapeDtypeStruct((b, s, rw), BF16),
        jax.ShapeDtypeStruct((b, s, rw), BF16),
        jax.ShapeDtypeStruct((b, s, vw), BF16),
        jax.ShapeDtypeStruct((b, s, vw), BF16),
        jax.ShapeDtypeStruct((b, s, d), BF16),
        jax.ShapeDtypeStruct((b, s, d), BF16),
    )
    out_specs = (
        pl.BlockSpec((None, bpt, mw, MOBA_BLOCK), blk4),
        pl.BlockSpec((None, bpt, MOBA_BLOCK, mw), blk4),
        pl.BlockSpec((None, bpt, MOBA_BLOCK, mw), blk4),
        pl.BlockSpec((None, tm, rw), row),
        pl.BlockSpec((None, tm, rw), row),
        pl.BlockSpec((None, tm, vw), row),
        pl.BlockSpec((None, tm, vw), row),
        pl.BlockSpec((None, tm, d), row),
        pl.BlockSpec((None, tm, d), row),
    )
    in_specs = [
        pl.BlockSpec((None, tm, d), row),
        pl.BlockSpec((None, 1, N_MOD * d), lambda bi, t: (bi, 0, 0)),
        _resident(w_n.shape),
        pl.BlockSpec((tm, LANES), tab), pl.BlockSpec((tm, LANES), tab),
        pl.BlockSpec((tm, LANES), tab), pl.BlockSpec((tm, LANES), tab),
    ]
    return pl.pallas_call(
        functools.partial(_inproj_kernel, d=d),
        out_shape=out_shapes,
        grid=(b, s // tm),
        in_specs=in_specs,
        out_specs=out_specs,
        compiler_params=pltpu.CompilerParams(dimension_semantics=("arbitrary", "arbitrary"),
                                             vmem_limit_bytes=V7X_VMEM_LIMIT_BYTES),
        name="inproj",
    )(x, mod, w_n, ck, sk, cr, sr)


def _switch(index, branches, *operands):
    def build(lo, hi):
        if hi - lo == 1:
            return branches[lo]
        mid = (lo + hi) // 2
        return lambda *ops: lax.cond(index < mid, build(lo, mid), build(mid, hi), *ops)
    return build(0, len(branches))(*operands)


def _moba_kernel(qt_ref, k_ref, v_ref, hot_ref, o_ref, kmean_ref, vt_ref, qa_ref, s_ref):
    n = pl.program_id(2)
    nb = k_ref.shape[0]
    hd = MOBA_HEAD_DIM
    blk = MOBA_BLOCK
    hs = MOBA_HEADS_PER_STEP
    grp = 1 << MOBA_GROUP_LOG2
    sub = 8

    @pl.when(n == 0)
    def _():
        for j in range(nb):
            kmean_ref[j:j + 1, :] = jnp.mean(k_ref[j].astype(F32), axis=0, keepdims=True)
            vt_ref[j] = v_ref[j].astype(F32).T.astype(BF16)

    qt = qt_ref[...]
    feat = lax.broadcasted_iota(jnp.int32, (LANES, blk), 0)
    km = kmean_ref[...]
    km_lane = lax.broadcasted_iota(jnp.int32, km.shape, 1)
    blk_id = lax.broadcasted_iota(jnp.int32, (nb, blk), 0)
    valid = blk_id < n
    blk_f = blk_id.astype(F32)

    def col_max(st):
        return jnp.max(st.reshape(st.shape[0] // sub, sub, blk), axis=0)

    key_i = lax.broadcasted_iota(jnp.int32, (blk, blk), 0)
    qry_i = lax.broadcasted_iota(jnp.int32, (blk, blk), 1)
    causal = key_i <= qry_i

    def row_max(st):
        return jnp.max(col_max(st), axis=0, keepdims=True)

    ones = jnp.ones((BF16_SUBLANES, blk), BF16)

    def prologue():
        pad_rows = jnp.zeros((LANES - nb, blk), BF16)
        km_rows = []
        for hh in range(hs):
            pair0 = (hh // 2) * LANES
            in_head = (feat >= hh * hd - pair0) & (feat < (hh + 1) * hd - pair0)
            qa_ref[hh, 0:LANES] = jnp.where(in_head, qt[pair0:pair0 + LANES], 0.0).astype(BF16)
            qa_ref[hh, LANES + nb:2 * LANES] = pad_rows
            kmh = jnp.where((km_lane >= hh * hd) & (km_lane < (hh + 1) * hd), km, 0.0)
            km_hi = kmh.astype(BF16)
            km_rows += [km_hi, (kmh - km_hi.astype(F32)).astype(BF16)]
        gate_all = jnp.dot(jnp.concatenate(km_rows, axis=0), qt, preferred_element_type=F32)
        gates = [gate_all[2 * hh * nb:(2 * hh + 1) * nb] + gate_all[(2 * hh + 1) * nb:(2 * hh + 2) * nb]
                 for hh in range(hs)]

        m_own, p_own = [], []
        for hh in range(hs):
            pair0 = (hh // 2) * LANES
            st = jnp.dot(k_ref[n, :, pair0:pair0 + LANES], qa_ref[hh, 0:LANES], preferred_element_type=F32)
            st = jnp.where(causal, st, NEG_BIG)
            m_own.append(row_max(st))
            p_own.append(jnp.exp2(st - m_own[hh]).astype(BF16))

        for hh in range(hs):
            gate = jnp.where(valid, gates[hh], NEG_BIG)
            taken = jnp.zeros((nb, blk), F32)
            for _ in range(MOBA_TOPK):
                best = jnp.max(gate, axis=0, keepdims=True)
                first = jnp.min(jnp.where(gate == best, blk_f, float(nb)), axis=0, keepdims=True)
                pick = blk_f == first
                taken = jnp.where(pick, 1.0, taken)
                gate = jnp.where(pick, BELOW_NEG_BIG, gate)
            qa_ref[hh, LANES:LANES + nb] = jnp.where(valid & (taken > 0.5), 0.0, NEG_BIG).astype(BF16)
        return m_own, p_own

    n_groups = lax.shift_right_logical(n + (grp - 1), MOBA_GROUP_LOG2)

    def score_group(gi, slot, m_seen):
        j0 = grp * gi
        hot = hot_ref[j0:j0 + grp].reshape(grp * blk, LANES)
        keys = [jnp.concatenate([k_ref[j0:j0 + grp, :, p0:p0 + LANES].reshape(grp * blk, LANES), hot], axis=1)
                for p0 in range(0, k_ref.shape[2], LANES)]
        out = []
        for hh in range(hs):
            st = jnp.dot(keys[hh // 2], qa_ref[hh], preferred_element_type=F32)
            s_ref[slot, hh] = st.reshape(grp, blk, blk)
            out.append(jnp.maximum(m_seen[hh], row_max(st)))
        return out

    def absorb(acc, m_old, m_new, scores, values):
        p = jnp.exp2(scores - m_new).astype(BF16)
        return acc * jnp.exp2(m_old - m_new) + jnp.dot(values, p, preferred_element_type=F32)

    def finish(num_groups):
        m_own, p_own = prologue()
        m_used, m_seen = m_own, m_own
        if num_groups:
            m_seen = score_group(0, 0, m_own)
        v_own = vt_ref[n]
        acc = [jnp.dot(jnp.concatenate([v_own[hh * hd:(hh + 1) * hd, :], ones], axis=0), p_own[hh],
                       preferred_element_type=F32) for hh in range(hs)]
        for g in range(num_groups):
            cur = g % 2
            m_next = score_group(g + 1, 1 - cur, m_seen) if g + 1 < num_groups else m_seen
            new_acc = []
            for hh in range(hs):
                values = jnp.concatenate(
                    [jnp.concatenate([vt_ref[grp * g + i, hh * hd:(hh + 1) * hd, :], ones], axis=0)
                     for i in range(grp)], axis=1)
                new_acc.append(absorb(acc[hh], m_used[hh], m_seen[hh],
                                      s_ref[cur, hh].reshape(grp * blk, blk), values))
            m_used, m_seen, acc = m_seen, m_next, new_acc
        outs = [a[:hd, :] * (1.0 / a[hd:hd + 1, :]) for a in acc]
        o_ref[...] = jnp.concatenate(outs, axis=0).T.astype(BF16)

    _switch(n_groups, [functools.partial(finish, g) for g in range(nb // grp + 1)])


def _moba(qt, k, vt):
    b, nb, mw, blk = qt.shape
    hs = MOBA_HEADS_PER_STEP
    pair = hs * MOBA_HEAD_DIM
    assert nb <= LANES
    hot = np.broadcast_to(np.arange(LANES)[None, None, :] == np.arange(nb)[:, None, None], (nb, blk, LANES))
    hot = jnp.asarray(hot.astype(np.float32), BF16)
    return pl.pallas_call(
        _moba_kernel,
        out_shape=jax.ShapeDtypeStruct((b, nb * blk, mw), BF16),
        grid=(b, mw // pair, nb),
        in_specs=[pl.BlockSpec((None, None, pair, blk), lambda bi, hp, n: (bi, n, hp, 0)),
                  pl.BlockSpec((None, nb, blk, pair), lambda bi, hp, n: (bi, 0, 0, hp)),
                  pl.BlockSpec((None, nb, blk, pair), lambda bi, hp, n: (bi, 0, 0, hp)),
                  _resident(hot.shape)],
        out_specs=pl.BlockSpec((None, blk, pair), lambda bi, hp, n: (bi, n, hp)),
        scratch_shapes=[pltpu.VMEM((nb, pair), F32),
                        pltpu.VMEM((nb, pair, blk), BF16),
                        pltpu.VMEM((hs, 2 * LANES, blk), BF16),
                        pltpu.VMEM((2, hs, 1 << MOBA_GROUP_LOG2, blk, blk), F32)],
        compiler_params=pltpu.CompilerParams(dimension_semantics=("arbitrary", "arbitrary", "arbitrary"),
                                             vmem_limit_bytes=V7X_VMEM_LIMIT_BYTES),
        name="moba",
    )(qt, k, vt, hot)


def _ret_kernel(cdec_ref, q_ref, k_ref, v_ref, sg_ref, dint_ref, kdec_ref, qdec_ref, o_ref, state_ref):
    hi = pl.program_id(1)

    @pl.when(pl.program_id(2) == 0)
    def _():
        state_ref[...] = jnp.zeros_like(state_ref)

    c = RET_CHUNK
    dint = dint_ref[...]
    kdec = kdec_ref[...]
    qdec = qdec_ref[...]
    cd = cdec_ref[hi]
    s = state_ref[...]
    for ci in range(q_ref.shape[0] // c):
        rows = slice(ci * c, (ci + 1) * c)
        q = q_ref[rows, :]
        k = k_ref[rows, :]
        v = v_ref[rows, :]
        scores = lax.dot_general(q, k, (((1,), (1,)), ((), ())), preferred_element_type=F32) * dint
        inner = jnp.dot(scores.astype(BF16), v, preferred_element_type=F32)
        qd = (q.astype(F32) * qdec).astype(BF16)
        inner = inner + jnp.dot(qd, s.astype(BF16), preferred_element_type=F32)
        kd = (k.astype(F32) * kdec).astype(BF16)
        kv = lax.dot_general(kd, v, (((0,), (0,)), ((), ())), preferred_element_type=F32)
        s = cd * s + kv
        mu = jnp.mean(inner, axis=-1, keepdims=True)
        xc = inner - mu
        var = jnp.mean(xc * xc, axis=-1, keepdims=True)
        y = xc * lax.rsqrt(var + LN_EPS) * sg_ref[rows, :].astype(F32)
        o_ref[rows, :] = y.astype(BF16)
    state_ref[...] = s


def _retention(rq, rk, rv, sg):
    b, s, _ = rq.shape
    h, dk, dv, c = RET_HEADS, RET_QK_DIM, RET_V_DIM, RET_CHUNK
    tm = min(RET_ROW_TILE, s)
    f = np.float32
    log_g = np.log(f(1.0) - f(2.0) ** (f(-5.0) - np.arange(h, dtype=f))).astype(f)
    idx = np.arange(c, dtype=f)
    diff = idx[:, None] - idx[None, :]
    d_intra = np.where(diff >= 0, np.exp(log_g[:, None, None] * np.maximum(diff, f(0.0))), f(0.0)).astype(f)
    k_decay = np.exp(log_g[:, None] * (f(c - 1.0) - idx)[None, :]).astype(f)
    q_decay = np.exp(log_g[:, None] * (idx + f(1.0))[None, :]).astype(f)
    chunk_decay = np.exp(log_g * f(c)).astype(f)
    kdec = np.ascontiguousarray(np.broadcast_to(k_decay[:, :, None], (h, c, dk)))
    qdec = np.ascontiguousarray(np.broadcast_to(q_decay[:, :, None], (h, c, dk)))
    head_rows = lambda bi, hi, t: (bi, t, hi)
    head_tab = lambda bi, hi, t: (hi, 0, 0)
    return pl.pallas_call(
        _ret_kernel,
        out_shape=jax.ShapeDtypeStruct((b, s, h * dv), BF16),
        grid=(b, h, s // tm),
        in_specs=[pl.BlockSpec(memory_space=pltpu.SMEM),
                  pl.BlockSpec((None, tm, dk), head_rows),
                  pl.BlockSpec((None, tm, dk), head_rows),
                  pl.BlockSpec((None, tm, dv), head_rows),
                  pl.BlockSpec((None, tm, dv), head_rows),
                  pl.BlockSpec((None, c, c), head_tab),
                  pl.BlockSpec((None, c, dk), head_tab),
                  pl.BlockSpec((None, c, dk), head_tab)],
        out_specs=pl.BlockSpec((None, tm, dv), head_rows),
        scratch_shapes=[pltpu.VMEM((dk, dv), F32)],
        compiler_params=pltpu.CompilerParams(dimension_semantics=("arbitrary", "arbitrary", "arbitrary"),
                                             vmem_limit_bytes=V7X_VMEM_LIMIT_BYTES),
        name="ret",
    )(chunk_decay, rq, rk, rv, sg, d_intra, kdec, qdec)


def _shift_rows(g, prev, k):
    rolled = pltpu.roll(g, k, 0)
    sub = lax.broadcasted_iota(jnp.int32, prev.shape, 0)
    top = jnp.where(sub < k, pltpu.roll(prev, k, 0), rolled[:prev.shape[0], :])
    return jnp.concatenate([top, rolled[prev.shape[0]:, :]], axis=0)


def _mix_ffn_kernel(ya_ref, yr_ref, ga_ref, gr_ref, x_ref, mod_ref, pa_ref, pr_ref, wo_ref, ln1g_ref, ln1b_ref,
                    wg_ref, wu_ref, wd_ref, cw_ref, cb_ref, ln2g_ref, ln2b_ref,
                    o_ref, tail_ref, x1_ref, h2_ref, act_ref, *, d, alpha):
    tm = x_ref.shape[0]

    @pl.when(pl.program_id(1) == 0)
    def _():
        tail_ref[...] = jnp.zeros_like(tail_ref)

    g1 = mod_ref[:, 2 * d:3 * d]
    sh2 = mod_ref[:, 3 * d:4 * d]
    sc2 = mod_ref[:, 4 * d:5 * d]
    sub = tm // MERGE_SUBTILES
    for i in range(MERGE_SUBTILES):
        rows = slice(i * sub, (i + 1) * sub)
        a = jnp.dot(ya_ref[rows, :], pa_ref[...].astype(BF16), preferred_element_type=F32)
        r = jnp.dot(yr_ref[rows, :], pr_ref[...].astype(BF16), preferred_element_type=F32)
        merged = ga_ref[rows, :].astype(F32) * a + gr_ref[rows, :].astype(F32) * r
        y = jnp.dot(merged.astype(BF16), wo_ref[...].astype(BF16), preferred_element_type=F32)
        x1 = _layer_norm_rows(alpha * x_ref[rows, :] + g1 * y, ln1g_ref[...], ln1b_ref[...])
        x1_ref[rows, :] = x1
        h2_ref[rows, :] = (x1 * (1.0 + sc2) + sh2).astype(BF16)

    fc = FF_CHUNK
    n_chunks = wg_ref.shape[1] // fc
    early = {}
    for i in range(MERGE_SUBTILES):
        rows = slice(i * sub, (i + 1) * sub)
        for c in range(min(FFN_EARLY_CHUNKS, n_chunks)):
            cs = slice(c * fc, (c + 1) * fc)
            early[i, c] = (jnp.dot(h2_ref[rows, :], wg_ref[:, cs], preferred_element_type=F32),
                           jnp.dot(h2_ref[rows, :], wu_ref[:, cs], preferred_element_type=F32))
    h = h2_ref[...]
    for c in range(n_chunks):
        cs = slice(c * fc, (c + 1) * fc)
        if (0, c) in early:
            g = jnp.concatenate([early[i, c][0] for i in range(MERGE_SUBTILES)], axis=0)
            u = jnp.concatenate([early[i, c][1] for i in range(MERGE_SUBTILES)], axis=0)
        else:
            g = jnp.dot(h, wg_ref[:, cs], preferred_element_type=F32)
            u = jnp.dot(h, wu_ref[:, cs], preferred_element_type=F32)
        prev = tail_ref[c]
        tail_ref[c] = g[tm - 8:, :]
        gc = (_shift_rows(g, prev, 2) * cw_ref[0:1, cs] + _shift_rows(g, prev, 1) * cw_ref[1:2, cs]
              + g * cw_ref[2:3, cs] + cb_ref[:, cs])
        act_ref[:, cs] = (gc * (1.0 + lax.erf(gc * (2.0 ** -0.5))) * u).astype(BF16)
    half_g2 = 0.5 * mod_ref[:, 5 * d:6 * d]
    for i in range(MERGE_SUBTILES):
        rows = slice(i * sub, (i + 1) * sub)
        y = jnp.dot(act_ref[rows, :], wd_ref[...], preferred_element_type=F32)
        o_ref[rows, :] = _layer_norm_rows(alpha * x1_ref[rows, :] + half_g2 * y, ln2g_ref[...], ln2b_ref[...])


def _mix_ffn(ya, yr, ga, gr, x, mod, w_pa, w_pr, w_o, ln1_g, ln1_b,
             w_gate, w_up, conv_w, conv_b, w_down, ln2_g, ln2_b, alpha):
    b, s, d = x.shape
    dff = w_gate.shape[1]
    tm = ROW_TILE
    row = lambda bi, t: (bi, t, 0)
    pa, pr, wo = w_pa, w_pr, w_o
    wg, wu, wd = w_gate.astype(BF16), w_up.astype(BF16), w_down.astype(BF16)
    return pl.pallas_call(
        functools.partial(_mix_ffn_kernel, d=d, alpha=alpha),
        out_shape=jax.ShapeDtypeStruct((b, s, d), F32),
        grid=(b, s // tm),
        in_specs=[pl.BlockSpec((None, tm, ya.shape[-1]), row),
                  pl.BlockSpec((None, tm, yr.shape[-1]), row),
                  pl.BlockSpec((None, tm, d), row),
                  pl.BlockSpec((None, tm, d), row),
                  pl.BlockSpec((None, tm, d), row),
                  pl.BlockSpec((None, 1, N_MOD * d), lambda bi, t: (bi, 0, 0)),
                  _resident(pa.shape), _resident(pr.shape), _resident(wo.shape),
                  _resident((1, d)), _resident((1, d)),
                  _resident(wg.shape), _resident(wu.shape), _resident(wd.shape),
                  _resident(conv_w.shape), _resident((1, dff)),
                  _resident((1, d)), _resident((1, d))],
        out_specs=pl.BlockSpec((None, tm, d), row),
        scratch_shapes=[pltpu.VMEM((dff // FF_CHUNK, 8, FF_CHUNK), F32),
                        pltpu.VMEM((tm, d), F32),
                        pltpu.VMEM((tm, d), BF16),
                        pltpu.VMEM((tm, dff), BF16)],
        compiler_params=pltpu.CompilerParams(dimension_semantics=("arbitrary", "arbitrary"),
                                             vmem_limit_bytes=V7X_VMEM_LIMIT_BYTES),
        name="mix_ffn",
    )(ya, yr, ga, gr, x, mod, pa, pr, wo, ln1_g.reshape(1, d), ln1_b.reshape(1, d),
      wg, wu, wd, conv_w, conv_b.reshape(1, dff), ln2_g.reshape(1, d), ln2_b.reshape(1, d))


def kernel(x, c, w_ada, b_ada, w_in, w_proj_moba, w_proj_ret, w_out, ln1_g, ln1_b, w_ff_gate, w_ff_up,
           ff_conv_w, ff_conv_b, w_ff_down, ln2_g, ln2_b):
    depth = w_ada.shape[0]
    alpha = (2.0 * depth) ** 0.25
    for l in range(depth):
        mod = _ada(c, w_ada[l], b_ada[l])
        qt, k, v, rq, rk, rv, sg, ga, gr = _inproj(x, mod, w_in[l])
        ya = _moba(qt, k, v)
        yr = _retention(rq, rk, rv, sg)
        x = _mix_ffn(ya, yr, ga, gr, x, mod, w_proj_moba[l], w_proj_ret[l], w_out[l], ln1_g[l], ln1_b[l],
                     w_ff_gate[l], w_ff_up[l], ff_conv_w[l], ff_conv_b[l], w_ff_down[l], ln2_g[l], ln2_b[l],
                     alpha)
    return x
```

```python
import functools

import jax
import jax.numpy as jnp
import numpy as np
from jax import lax
from jax.experimental import pallas as pl
from jax.experimental.pallas import tpu as pltpu

F32 = jnp.float32
BF16 = jnp.bfloat16

MOBA_HEADS = 8
MOBA_HEAD_DIM = 64
MOBA_BLOCK = 256
MOBA_TOPK = 3
ROPE_THETA = 10000.0
RET_HEADS = 4
RET_QK_DIM = 128
RET_V_DIM = 256
RET_CHUNK = 128
RET_ANGLE_BASE = 10000.0
N_MOD = 6
LN_EPS = 1e-5
NEG_BIG = -1e30
BELOW_NEG_BIG = -3e38
CONV_WIDTH = 3

V7X_VMEM_LIMIT_BYTES = 56 * 1024 * 1024
LANES = 128
BF16_SUBLANES = 16
LOG2_E = 1.4426950408889634
ROW_TILE = 512
FF_CHUNK = 256
RET_ROW_TILE = 4096
MERGE_SUBTILES = 2
FFN_EARLY_CHUNKS = 3
MOBA_HEADS_PER_STEP = 4
MOBA_GROUP_LOG2 = 1


def _resident(shape):
    nd = len(shape)
    return pl.BlockSpec(shape, lambda *_: (0,) * nd, pipeline_mode=pl.Buffered(1))


def _layer_norm_rows(z, g, b):
    mu = jnp.mean(z, axis=-1, keepdims=True)
    zc = z - mu
    var = jnp.mean(zc * zc, axis=-1, keepdims=True)
    return zc * lax.rsqrt(var + LN_EPS) * g + b


def _ada_kernel(c_ref, w_ref, b_ref, o_ref):
    c = c_ref[...]
    a = c * jax.nn.sigmoid(c)
    rows = a.shape[0]
    a_hi = a.astype(BF16)
    a_lo = (a - a_hi.astype(F32)).astype(BF16)
    w = w_ref[...]
    w_hi = w.astype(BF16)
    w_lo = (w - w_hi.astype(F32)).astype(BF16)
    both = jnp.dot(jnp.concatenate([a_hi, a_lo], axis=0), w_hi, preferred_element_type=F32)
    o_ref[...] = (both[:rows] + both[rows:]
                  + jnp.dot(a_hi, w_lo, preferred_element_type=F32)) + b_ref[...]


def _ada(c, w_ada, b_ada):
    b, d = c.shape
    n = w_ada.shape[1]
    rows = BF16_SUBLANES
    cp = jnp.pad(c, ((0, rows - b), (0, 0)))
    nblk = 1536
    out = pl.pallas_call(
        _ada_kernel,
        out_shape=jax.ShapeDtypeStruct((rows, n), F32),
        grid=(n // nblk,),
        in_specs=[pl.BlockSpec((rows, d), lambda j: (0, 0)),
                  pl.BlockSpec((d, nblk), lambda j: (0, j)),
                  pl.BlockSpec((1, nblk), lambda j: (0, j))],
        out_specs=pl.BlockSpec((rows, nblk), lambda j: (0, j)),
        compiler_params=pltpu.CompilerParams(dimension_semantics=("arbitrary",),
                                             vmem_limit_bytes=V7X_VMEM_LIMIT_BYTES),
        name="ada",
    )(cp, w_ada, b_ada.reshape(1, n))
    return out[:b].reshape(b, 1, n)


def _inproj_kernel(x_ref, mod_ref, wn_ref, ck_ref, sk_ref, cr_ref, sr_ref,
                   qt_ref, k_ref, v_ref, rq_ref, rk_ref, rv_ref, sg_ref, ga_ref, gr_ref, *, d):
    tm = x_ref.shape[0]
    mw = MOBA_HEADS * MOBA_HEAD_DIM
    rw = RET_HEADS * RET_QK_DIM
    vw = RET_HEADS * RET_V_DIM
    sh1 = mod_ref[:, 0:d]
    sc1 = mod_ref[:, d:2 * d]
    h = (x_ref[...] * (1.0 + sc1) + sh1).astype(BF16)

    def nat(col0, width):
        return jnp.dot(h, wn_ref[:, col0:col0 + width].astype(BF16), preferred_element_type=F32)

    lane = lax.broadcasted_iota(jnp.int32, (tm, LANES), 1)
    half = MOBA_HEAD_DIM // 2
    ck = ck_ref[...]
    sk = sk_ref[...]
    low = (lane & half) == 0

    def rope(xs):
        partner = jnp.where(low, pltpu.roll(xs, LANES - half, 1), pltpu.roll(xs, half, 1))
        return xs * ck + partner * sk

    q_scale = LOG2_E * MOBA_HEAD_DIM ** -0.5
    qn = nat(0, mw)
    qr = jnp.concatenate([rope(qn[:, g * LANES:(g + 1) * LANES]) for g in range(mw // LANES)], axis=1)
    qt = (qr * q_scale).T.astype(BF16)
    for blk in range(tm // MOBA_BLOCK):
        qt_ref[blk] = qt[:, blk * MOBA_BLOCK:(blk + 1) * MOBA_BLOCK]

    kn = nat(mw, mw)
    for g in range(mw // LANES):
        res = rope(kn[:, g * LANES:(g + 1) * LANES]).astype(BF16)
        for blk in range(tm // MOBA_BLOCK):
            k_ref[blk, :, g * LANES:(g + 1) * LANES] = res[blk * MOBA_BLOCK:(blk + 1) * MOBA_BLOCK, :]

    for c0 in range(0, mw, 256):
        vn = nat(2 * mw + c0, 256).astype(BF16)
        for blk in range(tm // MOBA_BLOCK):
            v_ref[blk, :, c0:c0 + 256] = vn[blk * MOBA_BLOCK:(blk + 1) * MOBA_BLOCK, :]

    cr = cr_ref[...]
    sr = sr_ref[...]
    even = (lane & 1) == 0
    for col0, dst, scale in ((3 * mw, rq_ref, None), (3 * mw + rw, rk_ref, RET_QK_DIM ** -0.5)):
        rn = nat(col0, rw)
        for g in range(rw // LANES):
            xs = rn[:, g * LANES:(g + 1) * LANES]
            partner = jnp.where(even, pltpu.roll(xs, LANES - 1, 1), pltpu.roll(xs, 1, 1))
            res = xs * cr + partner * sr
            if scale is not None:
                res = res * scale
            dst[:, g * LANES:(g + 1) * LANES] = res.astype(BF16)

    col = 3 * mw + 2 * rw
    for c0 in range(0, vw, 512):
        rv_ref[:, c0:c0 + 512] = nat(col + c0, 512).astype(BF16)
    col += vw
    for c0 in range(0, vw, 512):
        z = nat(col + c0, 512)
        sg_ref[:, c0:c0 + 512] = (z * jax.nn.sigmoid(z)).astype(BF16)
    col += vw
    for dst in (ga_ref, gr_ref):
        for c0 in range(0, d, 512):
            dst[:, c0:c0 + 512] = jax.nn.sigmoid(nat(col + c0, 512)).astype(BF16)
        col += d


def _rope_tables(s):
    f = np.float32
    pos = np.arange(s, dtype=f)
    hd = MOBA_HEAD_DIM
    inv_freq = (f(ROPE_THETA) ** (-np.arange(0, hd, 2, dtype=f) / f(hd))).astype(f)
    ang = pos[:, None] * inv_freq[None, :]
    cos, sin = np.cos(ang).astype(f), np.sin(ang).astype(f)
    ck = np.tile(cos, (1, 2 * LANES // hd))
    sk = np.tile(np.concatenate([-sin, sin], axis=1), (1, LANES // hd))
    dk = RET_QK_DIM
    freq = (f(1.0) / (f(RET_ANGLE_BASE) ** np.linspace(0.0, 1.0, dk // 2, dtype=f))).astype(f)
    angr = pos[:, None] * freq[None, :]
    cosr, sinr = np.cos(angr).astype(f), np.sin(angr).astype(f)
    cr = np.repeat(cosr, 2, axis=1)
    sr = np.stack([-sinr, sinr], axis=-1).reshape(s, dk)
    return ck, sk, cr, sr


def _inproj(x, mod, w_in):
    b, s, d = x.shape
    mw = MOBA_HEADS * MOBA_HEAD_DIM
    rw = RET_HEADS * RET_QK_DIM
    vw = RET_HEADS * RET_V_DIM
    tm = ROW_TILE
    nb = s // MOBA_BLOCK
    bpt = tm // MOBA_BLOCK
    w_n = w_in
    ck, sk, cr, sr = _rope_tables(s)
    row = lambda bi, t: (bi, t, 0)
    blk4 = lambda bi, t: (bi, t, 0, 0)
    tab = lambda bi, t: (t, 0)
    out_shapes = (
        jax.ShapeDtypeStruct((b, nb, mw, MOBA_BLOCK), BF16),
        jax.ShapeDtypeStruct((b, nb, MOBA_BLOCK, mw), BF16),
        jax.ShapeDtypeStruct((b, nb, MOBA_BLOCK, mw), BF16),
        jax.ShapeDtypeStruct((b, s, rw), BF16),
        jax.ShapeDtypeStruct((b, s, rw), BF16),
        jax.ShapeDtypeStruct((b, s, vw), BF16),
        jax.ShapeDtypeStruct((b, s, vw), BF16),
        jax.ShapeDtypeStruct((b, s, d), BF16),
        jax.ShapeDtypeStruct((b, s, d), BF16),
    )
    out_specs = (
        pl.BlockSpec((None, bpt, mw, MOBA_BLOCK), blk4),
        pl.BlockSpec((None, bpt, MOBA_BLOCK, mw), blk4),
        pl.BlockSpec((None, bpt, MOBA_BLOCK, mw), blk4),
        pl.BlockSpec((None, tm, rw), row),
        pl.BlockSpec((None, tm, rw), row),
        pl.BlockSpec((None, tm, vw), row),
        pl.BlockSpec((None, tm, vw), row),
        pl.BlockSpec((None, tm, d), row),
        pl.BlockSpec((None, tm, d), row),
    )
    in_specs = [
        pl.BlockSpec((None, tm, d), row),
        pl.BlockSpec((None, 1, N_MOD * d), lambda bi, t: (bi, 0, 0)),
        _resident(w_n.shape),
        pl.BlockSpec((tm, LANES), tab), pl.BlockSpec((tm, LANES), tab),
        pl.BlockSpec((tm, LANES), tab), pl.BlockSpec((tm, LANES), tab),
    ]
    return pl.pallas_call(
        functools.partial(_inproj_kernel, d=d),
        out_shape=out_shapes,
        grid=(b, s // tm),
        in_specs=in_specs,
        out_specs=out_specs,
        compiler_params=pltpu.CompilerParams(dimension_semantics=("arbitrary", "arbitrary"),
                                             vmem_limit_bytes=V7X_VMEM_LIMIT_BYTES),
        name="inproj",
    )(x, mod, w_n, ck, sk, cr, sr)


def _switch(index, branches, *operands):
    def build(lo, hi):
        if hi - lo == 1:
            return branches[lo]
        mid = (lo + hi) // 2
        return lambda *ops: lax.cond(index < mid, build(lo, mid), build(mid, hi), *ops)
    return build(0, len(branches))(*operands)


def _moba_kernel(qt_ref, k_ref, v_ref, hot_ref, o_ref, kmean_ref, vt_ref, qa_ref, s_ref):
    n = pl.program_id(2)
    nb = k_ref.shape[0]
    hd = MOBA_HEAD_DIM
    blk = MOBA_BLOCK
    hs = MOBA_HEADS_PER_STEP
    grp = 1 << MOBA_GROUP_LOG2
    sub = 8

    @pl.when(n == 0)
    def _():
        for j in range(nb):
            kmean_ref[j:j + 1, :] = jnp.mean(k_ref[j].astype(F32), axis=0, keepdims=True)
            vt_ref[j] = v_ref[j].astype(F32).T.astype(BF16)

    qt = qt_ref[...]
    feat = lax.broadcasted_iota(jnp.int32, (LANES, blk), 0)
    km = kmean_ref[...]
    km_lane = lax.broadcasted_iota(jnp.int32, km.shape, 1)
    blk_id = lax.broadcasted_iota(jnp.int32, (nb, blk), 0)
    valid = blk_id < n
    blk_f = blk_id.astype(F32)

    def col_max(st):
        return jnp.max(st.reshape(st.shape[0] // sub, sub, blk), axis=0)

    key_i = lax.broadcasted_iota(jnp.int32, (blk, blk), 0)
    qry_i = lax.broadcasted_iota(jnp.int32, (blk, blk), 1)
    causal = key_i <= qry_i

    def row_max(st):
        return jnp.max(col_max(st), axis=0, keepdims=True)

    ones = jnp.ones((BF16_SUBLANES, blk), BF16)

    def prologue():
        pad_rows = jnp.zeros((LANES - nb, blk), BF16)
        km_rows = []
        for hh in range(hs):
            pair0 = (hh // 2) * LANES
            in_head = (feat >= hh * hd - pair0) & (feat < (hh + 1) * hd - pair0)
            qa_ref[hh, 0:LANES] = jnp.where(in_head, qt[pair0:pair0 + LANES], 0.0).astype(BF16)
            qa_ref[hh, LANES + nb:2 * LANES] = pad_rows
            kmh = jnp.where((km_lane >= hh * hd) & (km_lane < (hh + 1) * hd), km, 0.0)
            km_hi = kmh.astype(BF16)
            km_rows += [km_hi, (kmh - km_hi.astype(F32)).astype(BF16)]
        gate_all = jnp.dot(jnp.concatenate(km_rows, axis=0), qt, preferred_element_type=F32)
        gates = [gate_all[2 * hh * nb:(2 * hh + 1) * nb] + gate_all[(2 * hh + 1) * nb:(2 * hh + 2) * nb]
                 for hh in range(hs)]

        m_own, p_own = [], []
        for hh in range(hs):
            pair0 = (hh // 2) * LANES
            st = jnp.dot(k_ref[n, :, pair0:pair0 + LANES], qa_ref[hh, 0:LANES], preferred_element_type=F32)
            st = jnp.where(causal, st, NEG_BIG)
            m_own.append(row_max(st))
            p_own.append(jnp.exp2(st - m_own[hh]).astype(BF16))

        for hh in range(hs):
            gate = jnp.where(valid, gates[hh], NEG_BIG)
            taken = jnp.zeros((nb, blk), F32)
            for _ in range(MOBA_TOPK):
                best = jnp.max(gate, axis=0, keepdims=True)
                first = jnp.min(jnp.where(gate == best, blk_f, float(nb)), axis=0, keepdims=True)
                pick = blk_f == first
                taken = jnp.where(pick, 1.0, taken)
                gate = jnp.where(pick, BELOW_NEG_BIG, gate)
            qa_ref[hh, LANES:LANES + nb] = jnp.where(valid & (taken > 0.5), 0.0, NEG_BIG).astype(BF16)
        return m_own, p_own

    n_groups = lax.shift_right_logical(n + (grp - 1), MOBA_GROUP_LOG2)

    def score_group(gi, slot, m_seen):
        j0 = grp * gi
        hot = hot_ref[j0:j0 + grp].reshape(grp * blk, LANES)
        keys = [jnp.concatenate([k_ref[j0:j0 + grp, :, p0:p0 + LANES].reshape(grp * blk, LANES), hot], axis=1)
                for p0 in range(0, k_ref.shape[2], LANES)]
        out = []
        for hh in range(hs):
            st = jnp.dot(keys[hh // 2], qa_ref[hh], preferred_element_type=F32)
            s_ref[slot, hh] = st.reshape(grp, blk, blk)
            out.append(jnp.maximum(m_seen[hh], row_max(st)))
        return out

    def absorb(acc, m_old, m_new, scores, values):
        p = jnp.exp2(scores - m_new).astype(BF16)
        return acc * jnp.exp2(m_old - m_new) + jnp.dot(values, p, preferred_element_type=F32)

    def finish(num_groups):
        m_own, p_own = prologue()
        m_used, m_seen = m_own, m_own
        if num_groups:
            m_seen = score_group(0, 0, m_own)
        v_own = vt_ref[n]
        acc = [jnp.dot(jnp.concatenate([v_own[hh * hd:(hh + 1) * hd, :], ones], axis=0), p_own[hh],
                       preferred_element_type=F32) for hh in range(hs)]
        for g in range(num_groups):
            cur = g % 2
            m_next = score_group(g + 1, 1 - cur, m_seen) if g + 1 < num_groups else m_seen
            new_acc = []
            for hh in range(hs):
                values = jnp.concatenate(
                    [jnp.concatenate([vt_ref[grp * g + i, hh * hd:(hh + 1) * hd, :], ones], axis=0)
                     for i in range(grp)], axis=1)
                new_acc.append(absorb(acc[hh], m_used[hh], m_seen[hh],
                                      s_ref[cur, hh].reshape(grp * blk, blk), values))
            m_used, m_seen, acc = m_seen, m_next, new_acc
        outs = [a[:hd, :] * (1.0 / a[hd:hd + 1, :]) for a in acc]
        o_ref[...] = jnp.concatenate(outs, axis=0).T.astype(BF16)

    _switch(n_groups, [functools.partial(finish, g) for g in range(nb // grp + 1)])


def _moba(qt, k, vt):
    b, nb, mw, blk = qt.shape
    hs = MOBA_HEADS_PER_STEP
    pair = hs * MOBA_HEAD_DIM
    assert nb <= LANES
    hot = np.broadcast_to(np.arange(LANES)[None, None, :] == np.arange(nb)[:, None, None], (nb, blk, LANES))
    hot = jnp.asarray(hot.astype(np.float32), BF16)
    return pl.pallas_call(
        _moba_kernel,
        out_shape=jax.ShapeDtypeStruct((b, nb * blk, mw), BF16),
        grid=(b, mw // pair, nb),
        in_specs=[pl.BlockSpec((None, None, pair, blk), lambda bi, hp, n: (bi, n, hp, 0)),
                  pl.BlockSpec((None, nb, blk, pair), lambda bi, hp, n: (bi, 0, 0, hp)),
                  pl.BlockSpec((None, nb, blk, pair), lambda bi, hp, n: (bi, 0, 0, hp)),
                  _resident(hot.shape)],
        out_specs=pl.BlockSpec((None, blk, pair), lambda bi, hp, n: (bi, n, hp)),
        scratch_shapes=[pltpu.VMEM((nb, pair), F32),
                        pltpu.VMEM((nb, pair, blk), BF16),
                        pltpu.VMEM((hs, 2 * LANES, blk), BF16),
                        pltpu.VMEM((2, hs, 1 << MOBA_GROUP_LOG2, blk, blk), F32)],
        compiler_params=pltpu.CompilerParams(dimension_semantics=("arbitrary", "arbitrary", "arbitrary"),
                                             vmem_limit_bytes=V7X_VMEM_LIMIT_BYTES),
        name="moba",
    )(qt, k, vt, hot)


def _ret_kernel(cdec_ref, q_ref, k_ref, v_ref, sg_ref, dint_ref, kdec_ref, qdec_ref, o_ref, state_ref):
    hi = pl.program_id(1)

    @pl.when(pl.program_id(2) == 0)
    def _():
        state_ref[...] = jnp.zeros_like(state_ref)

    c = RET_CHUNK
    dint = dint_ref[...]
    kdec = kdec_ref[...]
    qdec = qdec_ref[...]
    cd = cdec_ref[hi]
    s = state_ref[...]
    for ci in range(q_ref.shape[0] // c):
        rows = slice(ci * c, (ci + 1) * c)
        q = q_ref[rows, :]
        k = k_ref[rows, :]
        v = v_ref[rows, :]
        scores = lax.dot_general(q, k, (((1,), (1,)), ((), ())), preferred_element_type=F32) * dint
        inner = jnp.dot(scores.astype(BF16), v, preferred_element_type=F32)
        qd = (q.astype(F32) * qdec).astype(BF16)
        inner = inner + jnp.dot(qd, s.astype(BF16), preferred_element_type=F32)
        kd = (k.astype(F32) * kdec).astype(BF16)
        kv = lax.dot_general(kd, v, (((0,), (0,)), ((), ())), preferred_element_type=F32)
        s = cd * s + kv
        mu = jnp.mean(inner, axis=-1, keepdims=True)
        xc = inner - mu
        var = jnp.mean(xc * xc, axis=-1, keepdims=True)
        y = xc * lax.rsqrt(var + LN_EPS) * sg_ref[rows, :].astype(F32)
        o_ref[rows, :] = y.astype(BF16)
    state_ref[...] = s


def _retention(rq, rk, rv, sg):
    b, s, _ = rq.shape
    h, dk, dv, c = RET_HEADS, RET_QK_DIM, RET_V_DIM, RET_CHUNK
    tm = min(RET_ROW_TILE, s)
    f = np.float32
    log_g = np.log(f(1.0) - f(2.0) ** (f(-5.0) - np.arange(h, dtype=f))).astype(f)
    idx = np.arange(c, dtype=f)
    diff = idx[:, None] - idx[None, :]
    d_intra = np.where(diff >= 0, np.exp(log_g[:, None, None] * np.maximum(diff, f(0.0))), f(0.0)).astype(f)
    k_decay = np.exp(log_g[:, None] * (f(c - 1.0) - idx)[None, :]).astype(f)
    q_decay = np.exp(log_g[:, None] * (idx + f(1.0))[None, :]).astype(f)
    chunk_decay = np.exp(log_g * f(c)).astype(f)
    kdec = np.ascontiguousarray(np.broadcast_to(k_decay[:, :, None], (h, c, dk)))
    qdec = np.ascontiguousarray(np.broadcast_to(q_decay[:, :, None], (h, c, dk)))
    head_rows = lambda bi, hi, t: (bi, t, hi)
    head_tab = lambda bi, hi, t: (hi, 0, 0)
    return pl.pallas_call(
        _ret_kernel,
        out_shape=jax.ShapeDtypeStruct((b, s, h * dv), BF16),
        grid=(b, h, s // tm),
        in_specs=[pl.BlockSpec(memory_space=pltpu.SMEM),
                  pl.BlockSpec((None, tm, dk), head_rows),
                  pl.BlockSpec((None, tm, dk), head_rows),
                  pl.BlockSpec((None, tm, dv), head_rows),
                  pl.BlockSpec((None, tm, dv), head_rows),
                  pl.BlockSpec((None, c, c), head_tab),
                  pl.BlockSpec((None, c, dk), head_tab),
                  pl.BlockSpec((None, c, dk), head_tab)],
        out_specs=pl.BlockSpec((None, tm, dv), head_rows),
        scratch_shapes=[pltpu.VMEM((dk, dv), F32)],
        compiler_params=pltpu.CompilerParams(dimension_semantics=("arbitrary", "arbitrary", "arbitrary"),
                                             vmem_limit_bytes=V7X_VMEM_LIMIT_BYTES),
        name="ret",
    )(chunk_decay, rq, rk, rv, sg, d_intra, kdec, qdec)


def _shift_rows(g, prev, k):
    rolled = pltpu.roll(g, k, 0)
    sub = lax.broadcasted_iota(jnp.int32, prev.shape, 0)
    top = jnp.where(sub < k, pltpu.roll(prev, k, 0), rolled[:prev.shape[0], :])
    return jnp.concatenate([top, rolled[prev.shape[0]:, :]], axis=0)


def _mix_ffn_kernel(ya_ref, yr_ref, ga_ref, gr_ref, x_ref, mod_ref, pa_ref, pr_ref, wo_ref, ln1g_ref, ln1b_ref,
                    wg_ref, wu_ref, wd_ref, cw_ref, cb_ref, ln2g_ref, ln2b_ref,
                    o_ref, tail_ref, x1_ref, h2_ref, act_ref, *, d, alpha):
    tm = x_ref.shape[0]

    @pl.when(pl.program_id(1) == 0)
    def _():
        tail_ref[...] = jnp.zeros_like(tail_ref)

    g1 = mod_ref[:, 2 * d:3 * d]
    sh2 = mod_ref[:, 3 * d:4 * d]
    sc2 = mod_ref[:, 4 * d:5 * d]
    sub = tm // MERGE_SUBTILES
    for i in range(MERGE_SUBTILES):
        rows = slice(i * sub, (i + 1) * sub)
        a = jnp.dot(ya_ref[rows, :], pa_ref[...].astype(BF16), preferred_element_type=F32)
        r = jnp.dot(yr_ref[rows, :], pr_ref[...].astype(BF16), preferred_element_type=F32)
        merged = ga_ref[rows, :].astype(F32) * a + gr_ref[rows, :].astype(F32) * r
        y = jnp.dot(merged.astype(BF16), wo_ref[...].astype(BF16), preferred_element_type=F32)
        x1 = _layer_norm_rows(alpha * x_ref[rows, :] + g1 * y, ln1g_ref[...], ln1b_ref[...])
        x1_ref[rows, :] = x1
        h2_ref[rows, :] = (x1 * (1.0 + sc2) + sh2).astype(BF16)

    fc = FF_CHUNK
    n_chunks = wg_ref.shape[1] // fc
    early = {}
    for i in range(MERGE_SUBTILES):
        rows = slice(i * sub, (i + 1) * sub)
        for c in range(min(FFN_EARLY_CHUNKS, n_chunks)):
            cs = slice(c * fc, (c + 1) * fc)
            early[i, c] = (jnp.dot(h2_ref[rows, :], wg_ref[:, cs], preferred_element_type=F32),
                           jnp.dot(h2_ref[rows, :], wu_ref[:, cs], preferred_element_type=F32))
    h = h2_ref[...]
    for c in range(n_chunks):
        cs = slice(c * fc, (c + 1) * fc)
        if (0, c) in early:
            g = jnp.concatenate([early[i, c][0] for i in range(MERGE_SUBTILES)], axis=0)
            u = jnp.concatenate([early[i, c][1] for i in range(MERGE_SUBTILES)], axis=0)
        else:
            g = jnp.dot(h, wg_ref[:, cs], preferred_element_type=F32)
            u = jnp.dot(h, wu_ref[:, cs], preferred_element_type=F32)
        prev = tail_ref[c]
        tail_ref[c] = g[tm - 8:, :]
        gc = (_shift_rows(g, prev, 2) * cw_ref[0:1, cs] + _shift_rows(g, prev, 1) * cw_ref[1:2, cs]
              + g * cw_ref[2:3, cs] + cb_ref[:, cs])
        act_ref[:, cs] = (gc * (1.0 + lax.erf(gc * (2.0 ** -0.5))) * u).astype(BF16)
    half_g2 = 0.5 * mod_ref[:, 5 * d:6 * d]
    for i in range(MERGE_SUBTILES):
        rows = slice(i * sub, (i + 1) * sub)
        y = jnp.dot(act_ref[rows, :], wd_ref[...].astype(BF16), preferred_element_type=F32)
        o_ref[rows, :] = _layer_norm_rows(alpha * x1_ref[rows, :] + half_g2 * y, ln2g_ref[...], ln2b_ref[...])


def _mix_ffn(ya, yr, ga, gr, x, mod, w_pa, w_pr, w_o, ln1_g, ln1_b,
             w_gate, w_up, conv_w, conv_b, w_down, ln2_g, ln2_b, alpha):
    b, s, d = x.shape
    dff = w_gate.shape[1]
    tm = ROW_TILE
    row = lambda bi, t: (bi, t, 0)
    pa, pr, wo = w_pa, w_pr, w_o
    wg, wu, wd = w_gate.astype(BF16), w_up.astype(BF16), w_down
    return pl.pallas_call(
        functools.partial(_mix_ffn_kernel, d=d, alpha=alpha),
        out_shape=jax.ShapeDtypeStruct((b, s, d), F32),
        grid=(b, s // tm),
        in_specs=[pl.BlockSpec((None, tm, ya.shape[-1]), row),
                  pl.BlockSpec((None, tm, yr.shape[-1]), row),
                  pl.BlockSpec((None, tm, d), row),
                  pl.BlockSpec((None, tm, d), row),
                  pl.BlockSpec((None, tm, d), row),
                  pl.BlockSpec((None, 1, N_MOD * d), lambda bi, t: (bi, 0, 0)),
                  _resident(pa.shape), _resident(pr.shape), _resident(wo.shape),
                  _resident((1, d)), _resident((1, d)),
                  _resident(wg.shape), _resident(wu.shape), _resident(wd.shape),
                  _resident(conv_w.shape), _resident((1, dff)),
                  _resident((1, d)), _resident((1, d))],
        out_specs=pl.BlockSpec((None, tm, d), row),
        scratch_shapes=[pltpu.VMEM((dff // FF_CHUNK, 8, FF_CHUNK), F32),
                        pltpu.VMEM((tm, d), F32),
                        pltpu.VMEM((tm, d), BF16),
                        pltpu.VMEM((tm, dff), BF16)],
        compiler_params=pltpu.CompilerParams(dimension_semantics=("arbitrary", "arbitrary"),
                                             vmem_limit_bytes=V7X_VMEM_LIMIT_BYTES),
        name="mix_ffn",
    )(ya, yr, ga, gr, x, mod, pa, pr, wo, ln1_g.reshape(1, d), ln1_b.reshape(1, d),
      wg, wu, wd, conv_w, conv_b.reshape(1, dff), ln2_g.reshape(1, d), ln2_b.reshape(1, d))


def kernel(x, c, w_ada, b_ada, w_in, w_proj_moba, w_proj_ret, w_out, ln1_g, ln1_b, w_ff_gate, w_ff_up,
           ff_conv_w, ff_conv_b, w_ff_down, ln2_g, ln2_b):
    depth = w_ada.shape[0]
    alpha = (2.0 * depth) ** 0.25
    for l in range(depth):
        mod = _ada(c, w_ada[l], b_ada[l])
        qt, k, v, rq, rk, rv, sg, ga, gr = _inproj(x, mod, w_in[l])
        ya = _moba(qt, k, v)
        yr = _retention(rq, rk, rv, sg)
        x = _mix_ffn(ya, yr, ga, gr, x, mod, w_proj_moba[l], w_proj_ret[l], w_out[l], ln1_g[l], ln1_b[l],
                     w_ff_gate[l], w_ff_up[l], ff_conv_w[l], ff_conv_b[l], w_ff_down[l], ln2_g[l], ln2_b[l],
                     alpha)
    return x
```

```python
import functools

import jax
import jax.numpy as jnp
import numpy as np
from jax import lax
from jax.experimental import pallas as pl
from jax.experimental.pallas import tpu as pltpu

F32 = jnp.float32
BF16 = jnp.bfloat16

MOBA_HEADS = 8
MOBA_HEAD_DIM = 64
MOBA_BLOCK = 256
MOBA_TOPK = 3
ROPE_THETA = 10000.0
RET_HEADS = 4
RET_QK_DIM = 128
RET_V_DIM = 256
RET_CHUNK = 128
RET_ANGLE_BASE = 10000.0
N_MOD = 6
LN_EPS = 1e-5
NEG_BIG = -1e30
BELOW_NEG_BIG = -3e38
CONV_WIDTH = 3

V7X_VMEM_LIMIT_BYTES = 56 * 1024 * 1024
LANES = 128
BF16_SUBLANES = 16
LOG2_E = 1.4426950408889634
ROW_TILE = 512
FF_CHUNK = 256
RET_ROW_TILE = 4096
MERGE_SUBTILES = 2
FFN_EARLY_CHUNKS = 3
MOBA_HEADS_PER_STEP = 4
MOBA_GROUP_LOG2 = 1


def _resident(shape):
    nd = len(shape)
    return pl.BlockSpec(shape, lambda *_: (0,) * nd, pipeline_mode=pl.Buffered(1))


def _layer_norm_rows(z, g, b):
    mu = jnp.mean(z, axis=-1, keepdims=True)
    zc = z - mu
    var = jnp.mean(zc * zc, axis=-1, keepdims=True)
    return zc * lax.rsqrt(var + LN_EPS) * g + b


def _ada_kernel(c_ref, w_ref, b_ref, o_ref):
    c = c_ref[...]
    a = c * jax.nn.sigmoid(c)
    rows = a.shape[0]
    a_hi = a.astype(BF16)
    a_lo = (a - a_hi.astype(F32)).astype(BF16)
    w = w_ref[...]
    w_hi = w.astype(BF16)
    w_lo = (w - w_hi.astype(F32)).astype(BF16)
    both = jnp.dot(jnp.concatenate([a_hi, a_lo], axis=0), w_hi, preferred_element_type=F32)
    o_ref[...] = (both[:rows] + both[rows:]
                  + jnp.dot(a_hi, w_lo, preferred_element_type=F32)) + b_ref[...]


def _ada(c, w_ada, b_ada):
    b, d = c.shape
    n = w_ada.shape[1]
    rows = BF16_SUBLANES
    cp = jnp.pad(c, ((0, rows - b), (0, 0)))
    nblk = 1536
    out = pl.pallas_call(
        _ada_kernel,
        out_shape=jax.ShapeDtypeStruct((rows, n), F32),
        grid=(n // nblk,),
        in_specs=[pl.BlockSpec((rows, d), lambda j: (0, 0)),
                  pl.BlockSpec((d, nblk), lambda j: (0, j)),
                  pl.BlockSpec((1, nblk), lambda j: (0, j))],
        out_specs=pl.BlockSpec((rows, nblk), lambda j: (0, j)),
        compiler_params=pltpu.CompilerParams(dimension_semantics=("arbitrary",),
                                             vmem_limit_bytes=V7X_VMEM_LIMIT_BYTES),
        name="ada",
    )(cp, w_ada, b_ada.reshape(1, n))
    return out[:b].reshape(b, 1, n)


def _inproj_kernel(x_ref, mod_ref, wn_ref, ck_ref, sk_ref, cr_ref, sr_ref,
                   qt_ref, k_ref, v_ref, rq_ref, rk_ref, rv_ref, sg_ref, ga_ref, gr_ref, *, d):
    tm = x_ref.shape[0]
    mw = MOBA_HEADS * MOBA_HEAD_DIM
    rw = RET_HEADS * RET_QK_DIM
    vw = RET_HEADS * RET_V_DIM
    sh1 = mod_ref[:, 0:d]
    sc1 = mod_ref[:, d:2 * d]
    h = (x_ref[...] * (1.0 + sc1) + sh1).astype(BF16)

    def nat(col0, width):
        return jnp.dot(h, wn_ref[:, col0:col0 + width].astype(BF16), preferred_element_type=F32)

    lane = lax.broadcasted_iota(jnp.int32, (tm, LANES), 1)
    half = MOBA_HEAD_DIM // 2
    ck = ck_ref[...]
    sk = sk_ref[...]
    low = (lane & half) == 0

    def rope(xs):
        partner = jnp.where(low, pltpu.roll(xs, LANES - half, 1), pltpu.roll(xs, half, 1))
        return xs * ck + partner * sk

    q_scale = LOG2_E * MOBA_HEAD_DIM ** -0.5
    qn = nat(0, mw)
    qr = jnp.concatenate([rope(qn[:, g * LANES:(g + 1) * LANES]) for g in range(mw // LANES)], axis=1)
    qt = (qr * q_scale).T.astype(BF16)
    for blk in range(tm // MOBA_BLOCK):
        qt_ref[blk] = qt[:, blk * MOBA_BLOCK:(blk + 1) * MOBA_BLOCK]

    kn = nat(mw, mw)
    for g in range(mw // LANES):
        res = rope(kn[:, g * LANES:(g + 1) * LANES]).astype(BF16)
        for blk in range(tm // MOBA_BLOCK):
            k_ref[blk, :, g * LANES:(g + 1) * LANES] = res[blk * MOBA_BLOCK:(blk + 1) * MOBA_BLOCK, :]

    for c0 in range(0, mw, 256):
        vn = nat(2 * mw + c0, 256).astype(BF16)
        for blk in range(tm // MOBA_BLOCK):
            v_ref[blk, :, c0:c0 + 256] = vn[blk * MOBA_BLOCK:(blk + 1) * MOBA_BLOCK, :]

    cr = cr_ref[...]
    sr = sr_ref[...]
    even = (lane & 1) == 0
    for col0, dst, scale in ((3 * mw, rq_ref, None), (3 * mw + rw, rk_ref, RET_QK_DIM ** -0.5)):
        rn = nat(col0, rw)
        for g in range(rw // LANES):
            xs = rn[:, g * LANES:(g + 1) * LANES]
            partner = jnp.where(even, pltpu.roll(xs, LANES - 1, 1), pltpu.roll(xs, 1, 1))
            res = xs * cr + partner * sr
            if scale is not None:
                res = res * scale
            dst[:, g * LANES:(g + 1) * LANES] = res.astype(BF16)

    col = 3 * mw + 2 * rw
    for c0 in range(0, vw, 512):
        rv_ref[:, c0:c0 + 512] = nat(col + c0, 512).astype(BF16)
    col += vw
    for c0 in range(0, vw, 512):
        z = nat(col + c0, 512)
        sg_ref[:, c0:c0 + 512] = (z * jax.nn.sigmoid(z)).astype(BF16)
    col += vw
    for dst in (ga_ref, gr_ref):
        for c0 in range(0, d, 512):
            dst[:, c0:c0 + 512] = jax.nn.sigmoid(nat(col + c0, 512)).astype(BF16)
        col += d


def _rope_tables(s):
    f = np.float32
    pos = np.arange(s, dtype=f)
    hd = MOBA_HEAD_DIM
    inv_freq = (f(ROPE_THETA) ** (-np.arange(0, hd, 2, dtype=f) / f(hd))).astype(f)
    ang = pos[:, None] * inv_freq[None, :]
    cos, sin = np.cos(ang).astype(f), np.sin(ang).astype(f)
    ck = np.tile(cos, (1, 2 * LANES // hd))
    sk = np.tile(np.concatenate([-sin, sin], axis=1), (1, LANES // hd))
    dk = RET_QK_DIM
    freq = (f(1.0) / (f(RET_ANGLE_BASE) ** np.linspace(0.0, 1.0, dk // 2, dtype=f))).astype(f)
    angr = pos[:, None] * freq[None, :]
    cosr, sinr = np.cos(angr).astype(f), np.sin(angr).astype(f)
    cr = np.repeat(cosr, 2, axis=1)
    sr = np.stack([-sinr, sinr], axis=-1).reshape(s, dk)
    return ck, sk, cr, sr


def _inproj(x, mod, w_in):
    b, s, d = x.shape
    mw = MOBA_HEADS * MOBA_HEAD_DIM
    rw = RET_HEADS * RET_QK_DIM
    vw = RET_HEADS * RET_V_DIM
    tm = ROW_TILE
    nb = s // MOBA_BLOCK
    bpt = tm // MOBA_BLOCK
    w_n = w_in
    ck, sk, cr, sr = _rope_tables(s)
    row = lambda bi, t: (bi, t, 0)
    blk4 = lambda bi, t: (bi, t, 0, 0)
    tab = lambda bi, t: (t, 0)
    out_shapes = (
        jax.ShapeDtypeStruct((b, nb, mw, MOBA_BLOCK), BF16),
        jax.ShapeDtypeStruct((b, nb, MOBA_BLOCK, mw), BF16),
        jax.ShapeDtypeStruct((b, nb, MOBA_BLOCK, mw), BF16),
        jax.ShapeDtypeStruct((b, s, rw), BF16),
        jax.ShapeDtypeStruct((b, s, rw), BF16),
        jax.ShapeDtypeStruct((b, s, vw), BF16),
        jax.ShapeDtypeStruct((b, s, vw), BF16),
        jax.ShapeDtypeStruct((b, s, d), BF16),
        jax.ShapeDtypeStruct((b, s, d), BF16),
    )
    out_specs = (
        pl.BlockSpec((None, bpt, mw, MOBA_BLOCK), blk4),
        pl.BlockSpec((None, bpt, MOBA_BLOCK, mw), blk4),
        pl.BlockSpec((None, bpt, MOBA_BLOCK, mw), blk4),
        pl.BlockSpec((None, tm, rw), row),
        pl.BlockSpec((None, tm, rw), row),
        pl.BlockSpec((None, tm, vw), row),
        pl.BlockSpec((None, tm, vw), row),
        pl.BlockSpec((None, tm, d), row),
        pl.BlockSpec((None, tm, d), row),
    )
    in_specs = [
        pl.BlockSpec((None, tm, d), row),
        pl.BlockSpec((None, 1, N_MOD * d), lambda bi, t: (bi, 0, 0)),
        _resident(w_n.shape),
        pl.BlockSpec((tm, LANES), tab), pl.BlockSpec((tm, LANES), tab),
        pl.BlockSpec((tm, LANES), tab), pl.BlockSpec((tm, LANES), tab),
    ]
    return pl.pallas_call(
        functools.partial(_inproj_kernel, d=d),
        out_shape=out_shapes,
        grid=(b, s // tm),
        in_specs=in_specs,
        out_specs=out_specs,
        compiler_params=pltpu.CompilerParams(dimension_semantics=("arbitrary", "arbitrary"),
                                             vmem_limit_bytes=V7X_VMEM_LIMIT_BYTES),
        name="inproj",
    )(x, mod, w_n, ck, sk, cr, sr)


def _switch(index, branches, *operands):
    def build(lo, hi):
        if hi - lo == 1:
            return branches[lo]
        mid = (lo + hi) // 2
        return lambda *ops: lax.cond(index < mid, build(lo, mid), build(mid, hi), *ops)
    return build(0, len(branches))(*operands)


def _moba_kernel(qt_ref, k_ref, v_ref, hot_ref, o_ref, kmean_ref, vt_ref, qa_ref, s_ref):
    n = pl.program_id(2)
    nb = k_ref.shape[0]
    hd = MOBA_HEAD_DIM
    blk = MOBA_BLOCK
    hs = MOBA_HEADS_PER_STEP
    grp = 1 << MOBA_GROUP_LOG2
    sub = 8

    @pl.when(n == 0)
    def _():
        for j in range(nb):
            kmean_ref[j:j + 1, :] = jnp.mean(k_ref[j].astype(F32), axis=0, keepdims=True)
            vt_ref[j] = v_ref[j].astype(F32).T.astype(BF16)

    qt = qt_ref[...]
    feat = lax.broadcasted_iota(jnp.int32, (LANES, blk), 0)
    km = kmean_ref[...]
    km_lane = lax.broadcasted_iota(jnp.int32, km.shape, 1)
    blk_id = lax.broadcasted_iota(jnp.int32, (nb, blk), 0)
    valid = blk_id < n
    blk_f = blk_id.astype(F32)

    def col_max(st):
        return jnp.max(st.reshape(st.shape[0] // sub, sub, blk), axis=0)

    key_i = lax.broadcasted_iota(jnp.int32, (blk, blk), 0)
    qry_i = lax.broadcasted_iota(jnp.int32, (blk, blk), 1)
    causal = key_i <= qry_i

    def row_max(st):
        return jnp.max(col_max(st), axis=0, keepdims=True)

    ones = jnp.ones((BF16_SUBLANES, blk), BF16)

    def prologue():
        pad_rows = jnp.zeros((LANES - nb, blk), BF16)
        km_rows = []
        for hh in range(hs):
            pair0 = (hh // 2) * LANES
            in_head = (feat >= hh * hd - pair0) & (feat < (hh + 1) * hd - pair0)
            qa_ref[hh, 0:LANES] = jnp.where(in_head, qt[pair0:pair0 + LANES], 0.0).astype(BF16)
            qa_ref[hh, LANES + nb:2 * LANES] = pad_rows
            kmh = jnp.where((km_lane >= hh * hd) & (km_lane < (hh + 1) * hd), km, 0.0)
            km_hi = kmh.astype(BF16)
            km_rows += [km_hi, (kmh - km_hi.astype(F32)).astype(BF16)]
        gate_all = jnp.dot(jnp.concatenate(km_rows, axis=0), qt, preferred_element_type=F32)
        gates = [gate_all[2 * hh * nb:(2 * hh + 1) * nb] + gate_all[(2 * hh + 1) * nb:(2 * hh + 2) * nb]
                 for hh in range(hs)]

        m_own, p_own = [], []
        for hh in range(hs):
            pair0 = (hh // 2) * LANES
            st = jnp.dot(k_ref[n, :, pair0:pair0 + LANES], qa_ref[hh, 0:LANES], preferred_element_type=F32)
            st = jnp.where(causal, st, NEG_BIG)
            m_own.append(row_max(st))
            p_own.append(jnp.exp2(st - m_own[hh]).astype(BF16))

        for hh in range(hs):
            gate = jnp.where(valid, gates[hh], NEG_BIG)
            taken = jnp.zeros((nb, blk), F32)
            for _ in range(MOBA_TOPK):
                best = jnp.max(gate, axis=0, keepdims=True)
                first = jnp.min(jnp.where(gate == best, blk_f, float(nb)), axis=0, keepdims=True)
                pick = blk_f == first
                taken = jnp.where(pick, 1.0, taken)
                gate = jnp.where(pick, BELOW_NEG_BIG, gate)
            qa_ref[hh, LANES:LANES + nb] = jnp.where(valid & (taken > 0.5), 0.0, NEG_BIG).astype(BF16)
        return m_own, p_own

    n_groups = lax.shift_right_logical(n + (grp - 1), MOBA_GROUP_LOG2)

    def score_group(gi, slot, m_seen):
        j0 = grp * gi
        hot = hot_ref[j0:j0 + grp].reshape(grp * blk, LANES)
        keys = [jnp.concatenate([k_ref[j0:j0 + grp, :, p0:p0 + LANES].reshape(grp * blk, LANES), hot], axis=1)
                for p0 in range(0, k_ref.shape[2], LANES)]
        out = []
        for hh in range(hs):
            st = jnp.dot(keys[hh // 2], qa_ref[hh], preferred_element_type=F32)
            s_ref[slot, hh] = st.reshape(grp, blk, blk)
            out.append(jnp.maximum(m_seen[hh], row_max(st)))
        return out

    def absorb(acc, m_old, m_new, scores, values):
        p = jnp.exp2(scores - m_new).astype(BF16)
        return acc * jnp.exp2(m_old - m_new) + jnp.dot(values, p, preferred_element_type=F32)

    def finish(num_groups):
        m_own, p_own = prologue()
        m_used, m_seen = m_own, m_own
        if num_groups:
            m_seen = score_group(0, 0, m_own)
        v_own = vt_ref[n]
        acc = [jnp.dot(jnp.concatenate([v_own[hh * hd:(hh + 1) * hd, :], ones], axis=0), p_own[hh],
                       preferred_element_type=F32) for hh in range(hs)]
        for g in range(num_groups):
            cur = g % 2
            m_next = score_group(g + 1, 1 - cur, m_seen) if g + 1 < num_groups else m_seen
            new_acc = []
            for hh in range(hs):
                values = jnp.concatenate(
                    [jnp.concatenate([vt_ref[grp * g + i, hh * hd:(hh + 1) * hd, :], ones], axis=0)
                     for i in range(grp)], axis=1)
                new_acc.append(absorb(acc[hh], m_used[hh], m_seen[hh],
                                      s_ref[cur, hh].reshape(grp * blk, blk), values))
            m_used, m_seen, acc = m_seen, m_next, new_acc
        outs = [a[:hd, :] * (1.0 / a[hd:hd + 1, :]) for a in acc]
        o_ref[...] = jnp.concatenate(outs, axis=0).T.astype(BF16)

    _switch(n_groups, [functools.partial(finish, g) for g in range(nb // grp + 1)])


def _moba(qt, k, vt):
    b, nb, mw, blk = qt.shape
    hs = MOBA_HEADS_PER_STEP
    pair = hs * MOBA_HEAD_DIM
    assert nb <= LANES
    hot = np.broadcast_to(np.arange(LANES)[None, None, :] == np.arange(nb)[:, None, None], (nb, blk, LANES))
    hot = jnp.asarray(hot.astype(np.float32), BF16)
    return pl.pallas_call(
        _moba_kernel,
        out_shape=jax.ShapeDtypeStruct((b, nb * blk, mw), BF16),
        grid=(b, mw // pair, nb),
        in_specs=[pl.BlockSpec((None, None, pair, blk), lambda bi, hp, n: (bi, n, hp, 0)),
                  pl.BlockSpec((None, nb, blk, pair), lambda bi, hp, n: (bi, 0, 0, hp)),
                  pl.BlockSpec((None, nb, blk, pair), lambda bi, hp, n: (bi, 0, 0, hp)),
                  _resident(hot.shape)],
        out_specs=pl.BlockSpec((None, blk, pair), lambda bi, hp, n: (bi, n, hp)),
        scratch_shapes=[pltpu.VMEM((nb, pair), F32),
                        pltpu.VMEM((nb, pair, blk), BF16),
                        pltpu.VMEM((hs, 2 * LANES, blk), BF16),
                        pltpu.VMEM((2, hs, 1 << MOBA_GROUP_LOG2, blk, blk), F32)],
        compiler_params=pltpu.CompilerParams(dimension_semantics=("arbitrary", "arbitrary", "arbitrary"),
                                             vmem_limit_bytes=V7X_VMEM_LIMIT_BYTES),
        name="moba",
    )(qt, k, vt, hot)


def _ret_kernel(cdec_ref, q_ref, k_ref, v_ref, sg_ref, dint_ref, kdec_ref, qdec_ref, o_ref, state_ref):
    hi = pl.program_id(1)

    @pl.when(pl.program_id(2) == 0)
    def _():
        state_ref[...] = jnp.zeros_like(state_ref)

    c = RET_CHUNK
    dint = dint_ref[...]
    kdec = kdec_ref[...]
    qdec = qdec_ref[...]
    cd = cdec_ref[hi]
    s = state_ref[...]
    n_chunks = q_ref.shape[0] // c

    def front(ci):
        rows = slice(ci * c, (ci + 1) * c)
        q = q_ref[rows, :]
        k = k_ref[rows, :]
        v = v_ref[rows, :]
        scores = lax.dot_general(q, k, (((1,), (1,)), ((), ())), preferred_element_type=F32) * dint
        kd = (k.astype(F32) * kdec).astype(BF16)
        kv = lax.dot_general(kd, v, (((0,), (0,)), ((), ())), preferred_element_type=F32)
        return q, v, scores.astype(BF16), kv

    pending = front(0)
    for ci in range(n_chunks):
        rows = slice(ci * c, (ci + 1) * c)
        following = front(ci + 1) if ci + 1 < n_chunks else None
        q, v, scores, kv = pending
        qd = (q.astype(F32) * qdec).astype(BF16)
        inner = (jnp.dot(scores, v, preferred_element_type=F32)
                 + jnp.dot(qd, s.astype(BF16), preferred_element_type=F32))
        s = cd * s + kv
        mu = jnp.mean(inner, axis=-1, keepdims=True)
        xc = inner - mu
        var = jnp.mean(xc * xc, axis=-1, keepdims=True)
        y = xc * lax.rsqrt(var + LN_EPS) * sg_ref[rows, :].astype(F32)
        o_ref[rows, :] = y.astype(BF16)
        pending = following
    state_ref[...] = s


def _retention(rq, rk, rv, sg):
    b, s, _ = rq.shape
    h, dk, dv, c = RET_HEADS, RET_QK_DIM, RET_V_DIM, RET_CHUNK
    tm = min(RET_ROW_TILE, s)
    f = np.float32
    log_g = np.log(f(1.0) - f(2.0) ** (f(-5.0) - np.arange(h, dtype=f))).astype(f)
    idx = np.arange(c, dtype=f)
    diff = idx[:, None] - idx[None, :]
    d_intra = np.where(diff >= 0, np.exp(log_g[:, None, None] * np.maximum(diff, f(0.0))), f(0.0)).astype(f)
    k_decay = np.exp(log_g[:, None] * (f(c - 1.0) - idx)[None, :]).astype(f)
    q_decay = np.exp(log_g[:, None] * (idx + f(1.0))[None, :]).astype(f)
    chunk_decay = np.exp(log_g * f(c)).astype(f)
    kdec = np.ascontiguousarray(np.broadcast_to(k_decay[:, :, None], (h, c, dk)))
    qdec = np.ascontiguousarray(np.broadcast_to(q_decay[:, :, None], (h, c, dk)))
    head_rows = lambda bi, hi, t: (bi, t, hi)
    head_tab = lambda bi, hi, t: (hi, 0, 0)
    return pl.pallas_call(
        _ret_kernel,
        out_shape=jax.ShapeDtypeStruct((b, s, h * dv), BF16),
        grid=(b, h, s // tm),
        in_specs=[pl.BlockSpec(memory_space=pltpu.SMEM),
                  pl.BlockSpec((None, tm, dk), head_rows),
                  pl.BlockSpec((None, tm, dk), head_rows),
                  pl.BlockSpec((None, tm, dv), head_rows),
                  pl.BlockSpec((None, tm, dv), head_rows),
                  pl.BlockSpec((None, c, c), head_tab),
                  pl.BlockSpec((None, c, dk), head_tab),
                  pl.BlockSpec((None, c, dk), head_tab)],
        out_specs=pl.BlockSpec((None, tm, dv), head_rows),
        scratch_shapes=[pltpu.VMEM((dk, dv), F32)],
        compiler_params=pltpu.CompilerParams(dimension_semantics=("arbitrary", "arbitrary", "arbitrary"),
                                             vmem_limit_bytes=V7X_VMEM_LIMIT_BYTES),
        name="ret",
    )(chunk_decay, rq, rk, rv, sg, d_intra, kdec, qdec)


def _shift_rows(g, prev, k):
    rolled = pltpu.roll(g, k, 0)
    sub = lax.broadcasted_iota(jnp.int32, prev.shape, 0)
    top = jnp.where(sub < k, pltpu.roll(prev, k, 0), rolled[:prev.shape[0], :])
    return jnp.concatenate([top, rolled[prev.shape[0]:, :]], axis=0)


def _mix_ffn_kernel(ya_ref, yr_ref, ga_ref, gr_ref, x_ref, mod_ref, pa_ref, pr_ref, wo_ref, ln1g_ref, ln1b_ref,
                    wg_ref, wu_ref, wd_ref, cw_ref, cb_ref, ln2g_ref, ln2b_ref,
                    o_ref, tail_ref, x1_ref, h2_ref, act_ref, *, d, alpha):
    tm = x_ref.shape[0]

    @pl.when(pl.program_id(1) == 0)
    def _():
        tail_ref[...] = jnp.zeros_like(tail_ref)

    g1 = mod_ref[:, 2 * d:3 * d]
    sh2 = mod_ref[:, 3 * d:4 * d]
    sc2 = mod_ref[:, 4 * d:5 * d]
    sub = tm // MERGE_SUBTILES
    for i in range(MERGE_SUBTILES):
        rows = slice(i * sub, (i + 1) * sub)
        a = jnp.dot(ya_ref[rows, :], pa_ref[...].astype(BF16), preferred_element_type=F32)
        r = jnp.dot(yr_ref[rows, :], pr_ref[...].astype(BF16), preferred_element_type=F32)
        merged = ga_ref[rows, :].astype(F32) * a + gr_ref[rows, :].astype(F32) * r
        y = jnp.dot(merged.astype(BF16), wo_ref[...].astype(BF16), preferred_element_type=F32)
        x1 = _layer_norm_rows(alpha * x_ref[rows, :] + g1 * y, ln1g_ref[...], ln1b_ref[...])
        x1_ref[rows, :] = x1
        h2_ref[rows, :] = (x1 * (1.0 + sc2) + sh2).astype(BF16)

    fc = FF_CHUNK
    n_chunks = wg_ref.shape[1] // fc
    early = {}
    for i in range(MERGE_SUBTILES):
        rows = slice(i * sub, (i + 1) * sub)
        for c in range(min(FFN_EARLY_CHUNKS, n_chunks)):
            cs = slice(c * fc, (c + 1) * fc)
            early[i, c] = (jnp.dot(h2_ref[rows, :], wg_ref[:, cs], preferred_element_type=F32),
                           jnp.dot(h2_ref[rows, :], wu_ref[:, cs], preferred_element_type=F32))
    h = h2_ref[...]
    for c in range(n_chunks):
        cs = slice(c * fc, (c + 1) * fc)
        if (0, c) in early:
            g = jnp.concatenate([early[i, c][0] for i in range(MERGE_SUBTILES)], axis=0)
            u = jnp.concatenate([early[i, c][1] for i in range(MERGE_SUBTILES)], axis=0)
        else:
            g = jnp.dot(h, wg_ref[:, cs], preferred_element_type=F32)
            u = jnp.dot(h, wu_ref[:, cs], preferred_element_type=F32)
        prev = tail_ref[c]
        tail_ref[c] = g[tm - 8:, :]
        gc = (_shift_rows(g, prev, 2) * cw_ref[0:1, cs] + _shift_rows(g, prev, 1) * cw_ref[1:2, cs]
              + g * cw_ref[2:3, cs] + cb_ref[:, cs])
        act_ref[:, cs] = (gc * (1.0 + lax.erf(gc * (2.0 ** -0.5))) * u).astype(BF16)
    half_g2 = 0.5 * mod_ref[:, 5 * d:6 * d]
    for i in range(MERGE_SUBTILES):
        rows = slice(i * sub, (i + 1) * sub)
        y = jnp.dot(act_ref[rows, :], wd_ref[...].astype(BF16), preferred_element_type=F32)
        o_ref[rows, :] = _layer_norm_rows(alpha * x1_ref[rows, :] + half_g2 * y, ln2g_ref[...], ln2b_ref[...])


def _mix_ffn(ya, yr, ga, gr, x, mod, w_pa, w_pr, w_o, ln1_g, ln1_b,
             w_gate, w_up, conv_w, conv_b, w_down, ln2_g, ln2_b, alpha):
    b, s, d = x.shape
    dff = w_gate.shape[1]
    tm = ROW_TILE
    row = lambda bi, t: (bi, t, 0)
    pa, pr, wo = w_pa, w_pr, w_o
    wg, wu, wd = w_gate.astype(BF16), w_up.astype(BF16), w_down
    return pl.pallas_call(
        functools.partial(_mix_ffn_kernel, d=d, alpha=alpha),
        out_shape=jax.ShapeDtypeStruct((b, s, d), F32),
        grid=(b, s // tm),
        in_specs=[pl.BlockSpec((None, tm, ya.shape[-1]), row),
                  pl.BlockSpec((None, tm, yr.shape[-1]), row),
                  pl.BlockSpec((None, tm, d), row),
                  pl.BlockSpec((None, tm, d), row),
                  pl.BlockSpec((None, tm, d), row),
                  pl.BlockSpec((None, 1, N_MOD * d), lambda bi, t: (bi, 0, 0)),
                  _resident(pa.shape), _resident(pr.shape), _resident(wo.shape),
                  _resident((1, d)), _resident((1, d)),
                  _resident(wg.shape), _resident(wu.shape), _resident(wd.shape),
                  _resident(conv_w.shape), _resident((1, dff)),
                  _resident((1, d)), _resident((1, d))],
        out_specs=pl.BlockSpec((None, tm, d), row),
        scratch_shapes=[pltpu.VMEM((dff // FF_CHUNK, 8, FF_CHUNK), F32),
                        pltpu.VMEM((tm, d), F32),
                        pltpu.VMEM((tm, d), BF16),
                        pltpu.VMEM((tm, dff), BF16)],
        compiler_params=pltpu.CompilerParams(dimension_semantics=("arbitrary", "arbitrary"),
                                             vmem_limit_bytes=V7X_VMEM_LIMIT_BYTES),
        name="mix_ffn",
    )(ya, yr, ga, gr, x, mod, pa, pr, wo, ln1_g.reshape(1, d), ln1_b.reshape(1, d),
      wg, wu, wd, conv_w, conv_b.reshape(1, dff), ln2_g.reshape(1, d), ln2_b.reshape(1, d))


def kernel(x, c, w_ada, b_ada, w_in, w_proj_moba, w_proj_ret, w_out, ln1_g, ln1_b, w_ff_gate, w_ff_up,
           ff_conv_w, ff_conv_b, w_ff_down, ln2_g, ln2_b):
    depth = w_ada.shape[0]
    alpha = (2.0 * depth) ** 0.25
    for l in range(depth):
        mod = _ada(c, w_ada[l], b_ada[l])
        qt, k, v, rq, rk, rv, sg, ga, gr = _inproj(x, mod, w_in[l])
        ya = _moba(qt, k, v)
        yr = _retention(rq, rk, rv, sg)
        x = _mix_ffn(ya, yr, ga, gr, x, mod, w_proj_moba[l], w_proj_ret[l], w_out[l], ln1_g[l], ln1_b[l],
                     w_ff_gate[l], w_ff_up[l], ff_conv_w[l], ff_conv_b[l], w_ff_down[l], ln2_g[l], ln2_b[l],
                     alpha)
    return x
```
